```python
import functools
import jax, jax.numpy as jnp
from jax import lax
import numpy as np

D_MODEL = 1024
BATCH = 1
SEQ = 16384
DEPTH = 1
DEC_BATCH = 128
DEC_SEQ = 1
PAST_LEN = 16384
PAGE_SIZE = 128

A_HEADS = 8
A_NOPE = 64
A_ROPE = 32
A_VDIM = 64
Q_RANK = 384
KV_RANK = 256
ROPE_THETA = 10000.0
Q_BLOCK = 128
B_HEADS = 8
B_HDIM = 64
B_WIDTH = B_HEADS * B_HDIM
DECAY_LORA = 64
AAA_LORA = 64
GATE_LORA = 160
GN_EPS = 64e-5
N_MEM = 256
X_HEADS = 4
X_HDIM = D_MODEL // X_HEADS
D_FF = 4 * D_MODEL
NORM_EPS = 1e-6
RW_COLS = 3 * B_WIDTH + DECAY_LORA + AAA_LORA + GATE_LORA
N_IN = Q_RANK + KV_RANK + A_ROPE + RW_COLS + 2 * D_MODEL

kernel_name = 'hybrid_mla_rwkv7_gated_decoder_step'

F32 = jnp.float32


def rms_norm(x, g):
    xf = x.astype(F32)
    y = xf * lax.rsqrt(jnp.mean(xf * xf, axis=-1, keepdims=True) + NORM_EPS)
    return y.astype(x.dtype) * g


def rope_tables(pos):
    inv = ROPE_THETA ** (-jnp.arange(0, A_ROPE, 2, dtype=F32) / A_ROPE)
    ang = pos.astype(F32)[:, None] * inv[None, :]
    return jnp.cos(ang), jnp.sin(ang)


def apply_rope(x, cos, sin):
    xf = x.astype(F32)
    x1, x2 = xf[..., :A_ROPE // 2], xf[..., A_ROPE // 2:]
    return jnp.concatenate([x1 * cos - x2 * sin, x2 * cos + x1 * sin], axis=-1).astype(x.dtype)


def mixer_projections(h, pos, w_in, q_norm, w_uq, kv_norm):
    n, t = h.shape[:2]
    p = h @ w_in
    o = 0
    cq = p[..., o:o + Q_RANK]; o += Q_RANK
    ckv = p[..., o:o + KV_RANK]; o += KV_RANK
    kpe = p[..., o:o + A_ROPE]; o += A_ROPE
    rw = p[..., o:o + RW_COLS]; o += RW_COLS
    gate_a = jax.nn.sigmoid(p[..., o:o + D_MODEL]); o += D_MODEL
    gate_b = jax.nn.sigmoid(p[..., o:o + D_MODEL])
    cos, sin = rope_tables(pos)
    q = (rms_norm(cq, q_norm) @ w_uq).reshape(n, t, A_HEADS, A_NOPE + A_ROPE)
    q_nope = q[..., :A_NOPE]
    q_pe = apply_rope(q[..., A_NOPE:], cos[:, None, :], sin[:, None, :])
    ckv = rms_norm(ckv, kv_norm)
    kpe = apply_rope(kpe, cos, sin)
    return q_nope, q_pe, ckv, kpe, rw, gate_a, gate_b


def mla_prompt(q_nope, q_pe, ckv, kpe, *, w_uk, w_uv):
    n, t = q_nope.shape[:2]
    k_nope = jnp.einsum('btr,rhd->bthd', ckv, w_uk)
    v = jnp.einsum('btr,rhd->bthd', ckv, w_uv)
    k = jnp.concatenate([k_nope, jnp.broadcast_to(kpe[:, :, None, :], (n, t, A_HEADS, A_ROPE))], axis=-1)
    q = jnp.concatenate([q_nope, q_pe], axis=-1)
    n_blk = t // Q_BLOCK
    qb = jnp.moveaxis(q.reshape(n, n_blk, Q_BLOCK, A_HEADS, A_NOPE + A_ROPE), 1, 0)
    key_pos = jnp.arange(t)
    scale = (A_NOPE + A_ROPE) ** -0.5

    def block(args):
        i, q_i = args
        s = jnp.einsum('bqhd,bkhd->bhqk', q_i, k, preferred_element_type=F32) * scale
        q_pos = i * Q_BLOCK + jnp.arange(Q_BLOCK)
        s = jnp.where(key_pos[None, :] <= q_pos[:, None], s, -jnp.inf)
        p = jax.nn.softmax(s, axis=-1).astype(v.dtype)
        return jnp.einsum('bhqk,bkhd->bqhd', p, v)

    o = lax.map(block, (jnp.arange(n_blk), qb))
    return jnp.moveaxis(o, 0, 1).reshape(n, t, A_HEADS * A_VDIM)


def mla_sample(q_nope, q_pe, ckv_new, kpe_new, *, cache_ckv, cache_kpe, layer, page_table, w_uk, w_uv):
    n, t = q_nope.shape[:2]
    scale = (A_NOPE + A_ROPE) ** -0.5
    q_abs = jnp.einsum('bqhd,rhd->bqhr', q_nope, w_uk)

    def scores(ckv_rows, kpe_rows):
        return (jnp.einsum('bqhr,bkr->bhqk', q_abs, ckv_rows, preferred_element_type=F32)
                + jnp.einsum('bqhd,bkd->bhqk', q_pe, kpe_rows, preferred_element_type=F32)) * scale

    def update(carry, s, ckv_rows):
        m, l, acc = carry
        m_new = jnp.maximum(m, jnp.max(s, axis=-1))
        alpha = jnp.exp(m - m_new)
        p = jnp.exp(s - m_new[..., None])
        l = l * alpha + jnp.sum(p, axis=-1)
        acc = acc * alpha[..., None] + jnp.einsum('bhqk,bkr->bhqr', p, ckv_rows.astype(F32))
        return (m_new, l, acc)

    def page_step(carry, phys):
        ckv_pg = cache_ckv[layer, phys]
        kpe_pg = cache_kpe[layer, phys]
        return update(carry, scores(ckv_pg, kpe_pg), ckv_pg), None

    init = (jnp.full((n, A_HEADS, t), -jnp.inf, F32),
            jnp.zeros((n, A_HEADS, t), F32),
            jnp.zeros((n, A_HEADS, t, KV_RANK), F32))
    carry, _ = lax.scan(page_step, init, page_table.T)
    causal = jnp.tril(jnp.ones((t, t), dtype=bool))
    s_self = jnp.where(causal, scores(ckv_new, kpe_new), -jnp.inf)
    _, l, acc = update(carry, s_self, ckv_new)
    o_lat = (acc / l[..., None]).astype(q_nope.dtype)
    o = jnp.einsum('bhqr,rhd->bqhd', o_lat, w_uv)
    return o.reshape(n, t, A_HEADS * A_VDIM)


def rwkv7(rw, prev_row, s0, mu, w0, decay_up, a0, a_up, g_up, k_k, k_a, r_k, ln_w, ln_b):
    n, t, _ = rw.shape
    shifted = jnp.concatenate([prev_row[:, None, :], rw[:, :-1]], axis=1)
    xs = rw + (shifted - rw) * mu
    r = xs[..., :B_WIDTH]
    k = xs[..., B_WIDTH:2 * B_WIDTH]
    v = xs[..., 2 * B_WIDTH:3 * B_WIDTH]
    o = 3 * B_WIDTH
    wd = xs[..., o:o + DECAY_LORA]; o += DECAY_LORA
    ad = xs[..., o:o + AAA_LORA]; o += AAA_LORA
    gd = xs[..., o:o + GATE_LORA]
    w = -jax.nn.softplus(-(w0 + jnp.tanh(wd) @ decay_up)) - 0.5
    decay = jnp.exp(-jnp.exp(w.astype(F32)))
    a = jax.nn.sigmoid(a0 + ad @ a_up)
    g = jax.nn.sigmoid(gd) @ g_up

    def heads(z):
        return z.reshape(n, t, B_HEADS, B_HDIM).astype(F32)

    kk = heads(k * k_k)
    kk = kk / jnp.maximum(jnp.sqrt(jnp.sum(kk * kk, axis=-1, keepdims=True)), 1e-12)
    k = k * (1 + (a - 1) * k_a)
    r_h, k_h, v_h, a_h, w_h = heads(r), heads(k), heads(v), heads(a), heads(decay)

    def step(S, inp):
        r_t, w_t, k_t, v_t, kk_t, a_t = inp
        sa = jnp.einsum('nhvk,nhk->nhv', S, -kk_t)
        S = (S * w_t[:, :, None, :] + sa[..., None] * (kk_t * a_t)[:, :, None, :]
             + v_t[..., None] * k_t[:, :, None, :])
        return S, jnp.einsum('nhvk,nhk->nhv', S, r_t)

    seq_first = lambda z: jnp.moveaxis(z, 1, 0)
    s_T, y = lax.scan(step, s0.astype(F32),
                      (seq_first(r_h), seq_first(w_h), seq_first(k_h), seq_first(v_h), seq_first(kk), seq_first(a_h)))
    y = jnp.moveaxis(y, 0, 1)
    mean = jnp.mean(y, axis=-1, keepdims=True)
    var = jnp.mean(jnp.square(y - mean), axis=-1, keepdims=True)
    y = ((y - mean) * lax.rsqrt(var + GN_EPS)).reshape(n, t, B_WIDTH) * ln_w + ln_b
    bonus = (jnp.sum(r_h * k_h * r_k, axis=-1, keepdims=True) * v_h).reshape(n, t, B_WIDTH)
    out = (y + bonus).astype(rw.dtype) * g
    return out, rw[:, -1], s_T.astype(s0.dtype)


def memory_kv(mem, mem_norm, w_mk, w_mv):
    n, m = mem.shape[:2]
    mh = rms_norm(mem, mem_norm)
    return ((mh @ w_mk).reshape(n, m, X_HEADS, X_HDIM), (mh @ w_mv).reshape(n, m, X_HEADS, X_HDIM))


def memory_attention(h, mk, mv, w_mq, w_mo):
    n, t = h.shape[:2]
    q = (h @ w_mq).reshape(n, t, X_HEADS, X_HDIM)
    s = jnp.einsum('bthd,bmhd->bhtm', q, mk, preferred_element_type=F32) * (X_HDIM ** -0.5)
    p = jax.nn.softmax(s, axis=-1).astype(mv.dtype)
    o = jnp.einsum('bhtm,bmhd->bthd', p, mv).reshape(n, t, X_HEADS * X_HDIM)
    return o @ w_mo


def decoder_layer(x, pos, attend, mk, mv, prev_row, s0, lp):
    h = rms_norm(x, lp['norm_pre_mix'])
    q_nope, q_pe, ckv, kpe, rw, gate_a, gate_b = mixer_projections(
        h, pos, lp['w_in'], lp['q_norm'], lp['w_uq'], lp['kv_norm'])
    o_a = attend(q_nope, q_pe, ckv, kpe)
    o_b, last_row, s_T = rwkv7(rw, prev_row, s0, lp['rw_mu'], lp['rw_w0'], lp['rw_decay_up'], lp['rw_a0'],
                               lp['rw_a_up'], lp['rw_g_up'], lp['rw_k_k'], lp['rw_k_a'], lp['rw_r_k'],
                               lp['rw_ln_w'], lp['rw_ln_b'])
    mixed = (gate_a * (o_a @ lp['w_proj_a']) + gate_b * (o_b @ lp['w_proj_b'])) @ lp['w_out']
    x = x + rms_norm(mixed, lp['norm_post_mix'])
    h = rms_norm(x, lp['norm_pre_mem'])
    x = x + rms_norm(memory_attention(h, mk, mv, lp['w_mq'], lp['w_mo']), lp['norm_post_mem'])
    h = rms_norm(x, lp['norm_pre_mlp'])
    ff = jnp.square(jax.nn.relu(h @ lp['w_ff_up'])) @ lp['w_ff_down']
    x = x + rms_norm(ff, lp['norm_post_mlp'])
    return x, ckv, kpe, last_row, s_T


def setup_inputs(seed: int = 0) -> dict:
    key = jax.random.key(seed)
    ks = iter(jax.random.split(key, 64))
    L = DEPTH
    nrm = lambda shape, scale: jax.random.normal(next(ks), shape, F32) * scale
    gain = lambda shape: 1.0 + nrm(shape, 0.02)
    n_pages = PAST_LEN // PAGE_SIZE
    n_used = DEC_BATCH * n_pages
    n_pool = n_used + max(1, n_used // 4)
    page_table = jax.random.permutation(next(ks), n_pool)[:n_used].reshape(DEC_BATCH, n_pages).astype(jnp.int32)
    return {
        'x_prompt': nrm((BATCH, SEQ, D_MODEL), 1.0),
        'x_sample': nrm((DEC_BATCH, DEC_SEQ, D_MODEL), 1.0),
        'mem_prompt': nrm((BATCH, N_MEM, D_MODEL), 1.0),
        'cache_ckv': nrm((L, n_pool, PAGE_SIZE, KV_RANK), 1.0),
        'cache_kpe': nrm((L, n_pool, PAGE_SIZE, A_ROPE), 1.0),
        'state_wkv': nrm((L, DEC_BATCH, B_HEADS, B_HDIM, B_HDIM), 0.3),
        'state_shift': nrm((L, DEC_BATCH, RW_COLS), 1.0),
        'cache_mem_k': nrm((L, DEC_BATCH, N_MEM, X_HEADS, X_HDIM), 1.0),
        'cache_mem_v': nrm((L, DEC_BATCH, N_MEM, X_HEADS, X_HDIM), 1.0),
        'page_table': page_table,
        'norm_pre_mix': gain((L, D_MODEL)),
        'w_in': nrm((L, D_MODEL, N_IN), D_MODEL ** -0.5),
        'q_norm': gain((L, Q_RANK)),
        'w_uq': nrm((L, Q_RANK, A_HEADS * (A_NOPE + A_ROPE)), Q_RANK ** -0.5),
        'kv_norm': gain((L, KV_RANK)),
        'w_uk': nrm((L, KV_RANK, A_HEADS, A_NOPE), KV_RANK ** -0.5),
        'w_uv': nrm((L, KV_RANK, A_HEADS, A_VDIM), KV_RANK ** -0.5),
        'rw_mu': jax.random.uniform(next(ks), (L, RW_COLS), F32, 0.0, 1.0),
        'rw_w0': jax.random.uniform(next(ks), (L, B_WIDTH), F32, -5.0, 0.0),
        'rw_decay_up': nrm((L, DECAY_LORA, B_WIDTH), 0.1),
        'rw_a0': nrm((L, B_WIDTH), 0.1),
        'rw_a_up': nrm((L, AAA_LORA, B_WIDTH), 0.1),
        'rw_g_up': nrm((L, GATE_LORA, B_WIDTH), GATE_LORA ** -0.5),
        'rw_k_k': 0.85 + nrm((L, B_WIDTH), 0.02),
        'rw_k_a': gain((L, B_WIDTH)),
        'rw_r_k': nrm((L, B_HEADS, B_HDIM), 0.1),
        'rw_ln_w': gain((L, B_WIDTH)),
        'rw_ln_b': nrm((L, B_WIDTH), 0.02),
        'w_proj_a': nrm((L, A_HEADS * A_VDIM, D_MODEL), (A_HEADS * A_VDIM) ** -0.5),
        'w_proj_b': nrm((L, B_WIDTH, D_MODEL), B_WIDTH ** -0.5),
        'w_out': nrm((L, D_MODEL, D_MODEL), D_MODEL ** -0.5),
        'norm_post_mix': gain((L, D_MODEL)),
        'norm_pre_mem': gain((L, D_MODEL)),
        'mem_norm': gain((L, D_MODEL)),
        'w_mq': nrm((L, D_MODEL, X_HEADS * X_HDIM), D_MODEL ** -0.5),
        'w_mk': nrm((L, D_MODEL, X_HEADS * X_HDIM), D_MODEL ** -0.5),
        'w_mv': nrm((L, D_MODEL, X_HEADS * X_HDIM), D_MODEL ** -0.5),
        'w_mo': nrm((L, X_HEADS * X_HDIM, D_MODEL), (X_HEADS * X_HDIM) ** -0.5),
        'norm_post_mem': gain((L, D_MODEL)),
        'norm_pre_mlp': gain((L, D_MODEL)),
        'w_ff_up': nrm((L, D_MODEL, D_FF), D_MODEL ** -0.5),
        'w_ff_down': nrm((L, D_FF, D_MODEL), D_FF ** -0.5),
        'norm_post_mlp': gain((L, D_MODEL)),
    }


def reference(x_prompt, x_sample, mem_prompt, cache_ckv, cache_kpe, state_wkv, state_shift,
              cache_mem_k, cache_mem_v, page_table, norm_pre_mix, w_in, q_norm, w_uq, kv_norm,
              w_uk, w_uv, rw_mu, rw_w0, rw_decay_up, rw_a0, rw_a_up, rw_g_up, rw_k_k, rw_k_a,
              rw_r_k, rw_ln_w, rw_ln_b, w_proj_a, w_proj_b, w_out, norm_post_mix, norm_pre_mem,
              mem_norm, w_mq, w_mk, w_mv, w_mo, norm_post_mem, norm_pre_mlp, w_ff_up, w_ff_down,
              norm_post_mlp):
    n_p, seq = x_prompt.shape[:2]
    past_len = page_table.shape[1] * cache_ckv.shape[2]
    pos_p = jnp.arange(seq)
    pos_s = past_len + jnp.arange(x_sample.shape[1])
    y_p, y_s = x_prompt, x_sample
    outs = [[] for _ in range(10)]
    for l in range(DEPTH):
        lp = {
            'norm_pre_mix': norm_pre_mix[l], 'w_in': w_in[l], 'q_norm': q_norm[l], 'w_uq': w_uq[l],
            'kv_norm': kv_norm[l], 'rw_mu': rw_mu[l], 'rw_w0': rw_w0[l], 'rw_decay_up': rw_decay_up[l],
            'rw_a0': rw_a0[l], 'rw_a_up': rw_a_up[l], 'rw_g_up': rw_g_up[l], 'rw_k_k': rw_k_k[l],
            'rw_k_a': rw_k_a[l], 'rw_r_k': rw_r_k[l], 'rw_ln_w': rw_ln_w[l], 'rw_ln_b': rw_ln_b[l],
            'w_proj_a': w_proj_a[l], 'w_proj_b': w_proj_b[l], 'w_out': w_out[l],
            'norm_post_mix': norm_post_mix[l], 'norm_pre_mem': norm_pre_mem[l], 'w_mq': w_mq[l],
            'w_mo': w_mo[l], 'norm_post_mem': norm_post_mem[l], 'norm_pre_mlp': norm_pre_mlp[l],
            'w_ff_up': w_ff_up[l], 'w_ff_down': w_ff_down[l], 'norm_post_mlp': norm_post_mlp[l],
        }
        mk_p, mv_p = memory_kv(mem_prompt, mem_norm[l], w_mk[l], w_mv[l])
        attend_p = functools.partial(mla_prompt, w_uk=w_uk[l], w_uv=w_uv[l])
        attend_s = functools.partial(mla_sample, cache_ckv=cache_ckv, cache_kpe=cache_kpe, layer=l,
                                     page_table=page_table, w_uk=w_uk[l], w_uv=w_uv[l])
        zero_row = jnp.zeros((n_p, RW_COLS), x_prompt.dtype)
        zero_wkv = jnp.zeros((n_p, B_HEADS, B_HDIM, B_HDIM), state_wkv.dtype)
        y_p, ckv_p, kpe_p, row_p, wkv_p = decoder_layer(y_p, pos_p, attend_p, mk_p, mv_p, zero_row, zero_wkv, lp)
        y_s, ckv_s, kpe_s, row_s, wkv_s = decoder_layer(y_s, pos_s, attend_s, cache_mem_k[l], cache_mem_v[l],
                                                        state_shift[l], state_wkv[l], lp)
        for lst, val in zip(outs, (ckv_p, kpe_p, wkv_p, row_p, mk_p, mv_p, ckv_s, kpe_s, wkv_s, row_s)):
            lst.append(val)
    ckv_prompt = jnp.stack(outs[0])
    kpe_prompt = jnp.stack(outs[1])
    wkv_prompt = jnp.stack(outs[2])
    shift_prompt = jnp.stack(outs[3])
    mem_k_prompt = jnp.stack(outs[4])
    mem_v_prompt = jnp.stack(outs[5])
    ckv_sample = jnp.stack(outs[6])
    kpe_sample = jnp.stack(outs[7])
    wkv_sample = jnp.stack(outs[8])
    shift_sample = jnp.stack(outs[9])
    return (y_p, y_s, ckv_prompt, kpe_prompt, wkv_prompt, shift_prompt, mem_k_prompt, mem_v_prompt,
            ckv_sample, kpe_sample, wkv_sample, shift_sample)
```

```python
import functools

import jax
import jax.numpy as jnp
from jax import lax
from jax.experimental import pallas as pl
from jax.experimental.pallas import tpu as pltpu

F32 = jnp.float32
BF16 = jnp.bfloat16

A_HEADS = 8
A_NOPE = 64
A_ROPE = 32
A_VDIM = 64
Q_RANK = 384
KV_RANK = 256
ROPE_THETA = 10000.0
B_HEADS = 8
B_HDIM = 64
B_WIDTH = B_HEADS * B_HDIM
DECAY_LORA = 64
AAA_LORA = 64
GATE_LORA = 160
LORA_IN = DECAY_LORA + AAA_LORA + GATE_LORA
GN_EPS = 64e-5
X_HEADS = 4
NORM_EPS = 1e-6
RW_COLS = 3 * B_WIDTH + LORA_IN

LANES = 128
QK_PAD = 128
RW_PAD = 15 * LANES
RWKV_CHUNK = 64
ATTN_BLOCK = 256
DECODE_PAGES = 8
ROW_TILE = 256
VMEM_LIMIT = 48 * 1024 * 1024


def _params(*sem):
    return pltpu.CompilerParams(dimension_semantics=sem, vmem_limit_bytes=VMEM_LIMIT)


def _full(shape):
    zeros = (0,) * len(shape)
    return pl.BlockSpec(shape, lambda *_: zeros)


def _rows(tm, width):
    return pl.BlockSpec((tm, width), lambda i: (i, 0))


def _rms(x, g):
    return x * lax.rsqrt(jnp.mean(x * x, axis=-1, keepdims=True) + NORM_EPS) * g


def _dot(a, b):
    return jnp.dot(a.astype(BF16), b.astype(BF16), preferred_element_type=F32)


def _dot_nt(a, b):
    return lax.dot_general(a.astype(BF16), b.astype(BF16), (((1,), (1,)), ((), ())),
                           preferred_element_type=F32)


def _dot_tn(a, b):
    return lax.dot_general(a.astype(BF16), b.astype(BF16), (((0,), (0,)), ((), ())),
                           preferred_element_type=F32)


def _sigmoid(x):
    return 1.0 / (1.0 + jnp.exp(-x))


def _seg_sum(x, e):
    hi = x.astype(BF16)
    lo = (x - hi.astype(F32)).astype(BF16)
    return (jnp.dot(hi, e, preferred_element_type=F32)
            + jnp.dot(lo, e, preferred_element_type=F32))


def _proj_in_kernel(prompt, x_ref, g_ref, wlat_ref, wrw_ref, qn_ref, kvn_ref, wqa_ref, wqb_ref,
                    ck_ref, sk_ref, wk_ref, wv_ref, *outs):
    if prompt:
        q_ref, ckv_ref, kpe_ref, rw_ref, k_ref, v_ref = outs
    else:
        q_ref, ckv_ref, kpe_ref, rw_ref, qabs_ref = outs
    scale = (A_NOPE + A_ROPE) ** -0.5
    h = _rms(x_ref[...], g_ref[...]).astype(BF16)
    lat = jnp.dot(h, wlat_ref[...], preferred_element_type=F32)
    cq = lat[:, :Q_RANK]
    ckv = _rms(lat[:, Q_RANK:Q_RANK + KV_RANK], kvn_ref[...])
    ck = ck_ref[...]
    sk = sk_ref[...]
    kpe = lat[:, 640:768] * ck + lat[:, 768:896] * sk
    ckv_ref[...] = ckv
    kpe_ref[...] = kpe[:, A_NOPE:A_NOPE + A_ROPE]
    rw = jnp.dot(h, wrw_ref[...], preferred_element_type=F32)
    rw_ref[...] = rw[:, :RW_COLS]
    qn = _rms(cq, qn_ref[...]).astype(BF16)
    qa = jnp.dot(qn, wqa_ref[...], preferred_element_type=F32)
    qb = jnp.dot(qn, wqb_ref[...], preferred_element_type=F32)
    lane = lax.broadcasted_iota(jnp.int32, ck.shape, 1)
    cq_t = scale * (ck + jnp.where(lane < A_NOPE, 1.0, 0.0))
    sq_t = scale * sk
    ckv_b = ckv.astype(BF16)
    if prompt:
        kn = jnp.dot(ckv_b, wk_ref[...], preferred_element_type=F32)
        v_ref[...] = jnp.dot(ckv_b, wv_ref[...], preferred_element_type=F32).astype(BF16)
    for hh in range(A_HEADS):
        sl = slice(hh * QK_PAD, (hh + 1) * QK_PAD)
        qh = (qa[:, sl] * cq_t + qb[:, sl] * sq_t).astype(BF16)
        q_ref[hh] = qh
        if prompt:
            k_ref[hh] = (kn[:, sl] + kpe).astype(BF16)
        else:
            qabs_ref[hh] = jnp.dot(qh, wk_ref[hh], preferred_element_type=F32)


def _proj_in(x, g, wlat, wrw, qn, kvn, wqa, wqb, ck, sk, wk, wv, *, prompt):
    m, d = x.shape
    tm = min(ROW_TILE, m)
    heads_out = lambda w, dt: jax.ShapeDtypeStruct((A_HEADS, m, w), dt)
    heads_spec = lambda w: pl.BlockSpec((A_HEADS, tm, w), lambda i: (0, i, 0))
    out_shape = [heads_out(QK_PAD, BF16), jax.ShapeDtypeStruct((m, KV_RANK), F32),
                 jax.ShapeDtypeStruct((m, A_ROPE), F32), jax.ShapeDtypeStruct((m, RW_COLS), F32)]
    out_specs = [heads_spec(QK_PAD), _rows(tm, KV_RANK), _rows(tm, A_ROPE), _rows(tm, RW_COLS)]
    if prompt:
        out_shape += [heads_out(QK_PAD, BF16), jax.ShapeDtypeStruct((m, A_HEADS * A_VDIM), BF16)]
        out_specs += [heads_spec(QK_PAD), _rows(tm, A_HEADS * A_VDIM)]
    else:
        out_shape += [heads_out(KV_RANK, F32)]
        out_specs += [heads_spec(KV_RANK)]
    return pl.pallas_call(
        functools.partial(_proj_in_kernel, prompt),
        grid=(m // tm,),
        in_specs=[_rows(tm, d), _full(g.shape), _full(wlat.shape), _full(wrw.shape), _full(qn.shape),
                  _full(kvn.shape), _full(wqa.shape), _full(wqb.shape), _rows(tm, LANES),
                  _rows(tm, LANES), _full(wk.shape), _full(wv.shape)],
        out_specs=out_specs, out_shape=out_shape,
        compiler_params=_params("parallel"), name="proj_in",
    )(x, g, wlat, wrw, qn, kvn, wqa, wqb, ck, sk, wk, wv)


def _rwkv_prep_kernel(rw_ref, sh_ref, mu_ref, w0_ref, a0_ref, wl_ref, kk_ref, ka_ref, e_ref,
                      r_ref, lw_ref, k2_ref, v_ref, na_ref, b_ref, g_ref):
    rw = rw_ref[...]
    xs = rw + (sh_ref[...] - rw) * mu_ref[...]
    r = xs[:, :B_WIDTH]
    k = xs[:, B_WIDTH:2 * B_WIDTH]
    v = xs[:, 2 * B_WIDTH:3 * B_WIDTH]
    lo = xs[:, 3 * B_WIDTH:]
    lane = lax.broadcasted_iota(jnp.int32, lo.shape, 1)
    t = jnp.where(lane < DECAY_LORA, jnp.tanh(lo),
                  jnp.where(lane < DECAY_LORA + AAA_LORA, lo, _sigmoid(lo)))
    l3 = jnp.dot(t.astype(BF16), wl_ref[...], preferred_element_type=F32)
    z = -(w0_ref[...] + l3[:, :B_WIDTH])
    softplus = jnp.maximum(z, 0.0) + jnp.log(1.0 + jnp.exp(-jnp.abs(z)))
    w = -softplus - 0.5
    a = _sigmoid(a0_ref[...] + l3[:, B_WIDTH:2 * B_WIDTH])
    kkr = k * kk_ref[...]
    norm = jnp.sqrt(_seg_sum(kkr * kkr, e_ref[...]))
    kk = kkr / jnp.maximum(norm, 1e-12)
    r_ref[...] = r
    lw_ref[...] = -jnp.exp(w)
    k2_ref[...] = k * (1.0 + (a - 1.0) * ka_ref[...])
    v_ref[...] = v
    na_ref[...] = -kk
    b_ref[...] = kk * a
    g_ref[...] = l3[:, 2 * B_WIDTH:]


def _rwkv_prep(rw, shifted, mu, w0, a0, wl, k_k, k_a, e):
    m = rw.shape[0]
    tm = min(ROW_TILE, m)
    o = jax.ShapeDtypeStruct((m, B_WIDTH), F32)
    return pl.pallas_call(
        _rwkv_prep_kernel, grid=(m // tm,),
        in_specs=[_rows(tm, RW_COLS), _rows(tm, RW_COLS), _full(mu.shape), _full(w0.shape),
                  _full(a0.shape), _full(wl.shape), _full(k_k.shape), _full(k_a.shape), _full(e.shape)],
        out_specs=[_rows(tm, B_WIDTH)] * 7, out_shape=[o] * 7,
        compiler_params=_params("parallel"), name="rwkv_prep",
    )(rw, shifted, mu, w0, a0, wl, k_k, k_a, e)


def _rwkv_chunk_kernel(r_ref, lw_ref, k2_ref, v_ref, na_ref, b_ref, y_ref, sout_ref, s_ref):
    c = pl.program_id(0)
    C = RWKV_CHUNK

    @pl.when(c == 0)
    def _():
        s_ref[...] = jnp.zeros_like(s_ref)

    row = lax.broadcasted_iota(jnp.int32, (2 * C, 2 * C), 0)
    col = lax.broadcasted_iota(jnp.int32, (2 * C, 2 * C), 1)
    tok_r = jnp.bitwise_and(row, C - 1)
    tok_c = jnp.bitwise_and(col, C - 1)
    strict = tok_c < tok_r
    incl = tok_c <= tok_r
    eye = jnp.where(row == col, 1.0, 0.0)
    crow = lax.broadcasted_iota(jnp.int32, (C, C), 0)
    ccol = lax.broadcasted_iota(jnp.int32, (C, C), 1)
    cum = jnp.where(ccol <= crow, 1.0, 0.0).astype(BF16)
    head0 = lax.broadcasted_iota(jnp.int32, (C, LANES), 1) < B_HDIM

    def stack(x):
        return jnp.concatenate([jnp.where(head0, x, 0.0), jnp.where(head0, 0.0, x)], axis=0)

    for p in range(B_HEADS // 2):
        sl = slice(p * LANES, (p + 1) * LANES)
        lw = lw_ref[:, sl]
        l_hi = lw.astype(BF16)
        rem = lw - l_hi.astype(F32)
        l_mid = rem.astype(BF16)
        l_lo = (rem - l_mid.astype(F32)).astype(BF16)
        G = (jnp.dot(cum, l_hi, preferred_element_type=F32)
             + jnp.dot(cum, l_mid, preferred_element_type=F32)
             + jnp.dot(cum, l_lo, preferred_element_type=F32))
        GC = G[C - 1:C, :]
        e_pos = jnp.exp(G)
        e_neg = jnp.exp(-G)
        e_prev = jnp.exp(G - lw)
        e_tail = jnp.exp(GC - G)
        r, k2, v, na, b = r_ref[:, sl], k2_ref[:, sl], v_ref[:, sl], na_ref[:, sl], b_ref[:, sl]
        At = stack(na * e_prev)
        Rt = stack(r * e_pos)
        Bt = stack(b * e_neg)
        Kt = stack(k2 * e_neg)
        Bh = stack(b * e_tail)
        Kh = stack(k2 * e_tail)
        Vs = stack(v)
        AA = _dot_nt(jnp.concatenate([At, Rt], axis=0), jnp.concatenate([Bt, Kt], axis=0))
        Aab = jnp.where(strict, AA[:2 * C, :2 * C], 0.0)
        Aak = jnp.where(strict, AA[:2 * C, 2 * C:], 0.0)
        Arb = jnp.where(incl, AA[2 * C:, :2 * C], 0.0)
        Ark = jnp.where(incl, AA[2 * C:, 2 * C:], 0.0)
        X = eye + Aab
        Pw = Aab
        n = 1
        while 2 * n < C:
            Pw = _dot(Pw, Pw)
            X = X + _dot(X, Pw)
            n *= 2
        S = s_ref[p]
        U = _dot(X, _dot_nt(At, S) + _dot(Aak, Vs))
        Y = _dot_nt(Rt, S) + _dot(Arb, U) + _dot(Ark, Vs)
        y_ref[:, sl] = Y[:C] + Y[C:]
        s_ref[p] = S * jnp.exp(GC) + _dot_tn(U, Bh) + _dot_tn(Vs, Kh)

    @pl.when(c == pl.num_programs(0) - 1)
    def _():
        sout_ref[...] = s_ref[...]


def _rwkv_chunked(r, lw, k2, v, na, b):
    t = r.shape[0]
    C = RWKV_CHUNK
    spec = _rows(C, B_WIDTH)
    pairs = B_HEADS // 2
    y, s = pl.pallas_call(
        _rwkv_chunk_kernel, grid=(t // C,),
        in_specs=[spec] * 6,
        out_specs=[spec, _full((pairs, LANES, LANES))],
        out_shape=[jax.ShapeDtypeStruct((t, B_WIDTH), F32),
                   jax.ShapeDtypeStruct((pairs, LANES, LANES), F32)],
        scratch_shapes=[pltpu.VMEM((pairs, LANES, LANES), F32)],
        compiler_params=_params("arbitrary"), name="rwkv_chunked",
    )(r, lw, k2, v, na, b)
    s = s.reshape(pairs, 2, B_HDIM, 2, B_HDIM)
    wkv = jnp.stack([s[:, 0, :, 0, :], s[:, 1, :, 1, :]], axis=1).reshape(B_HEADS, B_HDIM, B_HDIM)
    return y, wkv


def _rwkv_step_kernel(s_ref, r_ref, lw_ref, k2_ref, na_ref, b_ref, vcol_ref, sout_ref, y_ref):
    S = s_ref[...]
    sa = jnp.sum(S * na_ref[...], axis=-1, keepdims=True)
    S = S * jnp.exp(lw_ref[...]) + sa * b_ref[...] + vcol_ref[...] * k2_ref[...]
    sout_ref[...] = S
    y_ref[...] = jnp.sum(S * r_ref[...], axis=-1, keepdims=True)


def _rwkv_step(state, r, lw, k2, v, na, b):
    n = state.shape[0]
    nb = 8
    row4 = lambda x: x.reshape(n, B_HEADS, 1, B_HDIM)
    rspec = pl.BlockSpec((nb, B_HEADS, 1, B_HDIM), lambda i: (i, 0, 0, 0))
    cspec = pl.BlockSpec((nb, B_HEADS, B_HDIM, 1), lambda i: (i, 0, 0, 0))
    sspec = pl.BlockSpec((nb, B_HEADS, B_HDIM, B_HDIM), lambda i: (i, 0, 0, 0))
    s_new, y = pl.pallas_call(
        _rwkv_step_kernel, grid=(n // nb,),
        in_specs=[sspec, rspec, rspec, rspec, rspec, rspec, cspec],
        out_specs=[sspec, cspec],
        out_shape=[jax.ShapeDtypeStruct(state.shape, F32),
                   jax.ShapeDtypeStruct((n, B_HEADS, B_HDIM, 1), F32)],
        compiler_params=_params("parallel"), name="rwkv_step",
    )(state, row4(r), row4(lw), row4(k2), row4(na), row4(b), v.reshape(n, B_HEADS, B_HDIM, 1))
    return y.reshape(n, B_WIDTH), s_new


def _mla_prompt_kernel(q_ref, k_ref, v_ref, o_ref):
    qi = pl.program_id(1)
    T = ATTN_BLOCK
    q = (q_ref[0], q_ref[1])
    row = lax.broadcasted_iota(jnp.int32, (T, T), 0)
    col = lax.broadcasted_iota(jnp.int32, (T, T), 1)
    causal = col <= row

    def block(ks, carry, masked):
        start = pl.multiple_of(ks * T, T)
        vb = v_ref[pl.ds(start, T), :]
        out = []
        for hh in range(2):
            m, l, acc = carry[hh]
            s = _dot_nt(q[hh], k_ref[hh, pl.ds(start, T), :])
            if masked:
                s = jnp.where(causal, s, -jnp.inf)
            m_new = jnp.maximum(m, jnp.max(s, axis=-1, keepdims=True))
            alpha = jnp.exp(m - m_new)
            pr = jnp.exp(s - m_new)
            l = l * alpha + jnp.sum(pr, axis=-1, keepdims=True)
            acc = acc * alpha + jnp.dot(pr.astype(BF16), vb, preferred_element_type=F32)
            out.append((m_new, l, acc))
        return tuple(out)

    init = tuple((jnp.full((T, 1), -jnp.inf, F32), jnp.zeros((T, 1), F32), jnp.zeros((T, LANES), F32))
                 for _ in range(2))
    carry = lax.fori_loop(0, qi, lambda ks, cr: block(ks, cr, False), init)
    (_, l0, a0), (_, l1, a1) = block(qi, carry, True)
    head0 = lax.broadcasted_iota(jnp.int32, (T, LANES), 1) < A_VDIM
    o_ref[...] = jnp.where(head0, a0 / l0, a1 / l1).astype(o_ref.dtype)


def _mla_prompt(q, k, v):
    t = q.shape[1]
    T = ATTN_BLOCK
    return pl.pallas_call(
        _mla_prompt_kernel, grid=(A_HEADS // 2, t // T),
        in_specs=[pl.BlockSpec((2, T, QK_PAD), lambda p, i: (p, i, 0)),
                  pl.BlockSpec((2, t, QK_PAD), lambda p, i: (p, 0, 0)),
                  pl.BlockSpec((t, LANES), lambda p, i: (0, p))],
        out_specs=pl.BlockSpec((T, LANES), lambda p, i: (i, p)),
        out_shape=jax.ShapeDtypeStruct((t, A_HEADS * A_VDIM), BF16),
        compiler_params=_params("parallel", "arbitrary"), name="mla_prompt",
    )(q, k, v)


def _mla_decode_kernel(pt_ref, qabs_ref, q_ref, cnew_ref, knew_ref, wuv_ref, *rest):
    P = DECODE_PAGES
    ckv_refs, kpe_refs = rest[:P], rest[P:2 * P]
    o_ref, m_ref, l_ref, acc_ref = rest[2 * P:]
    j = pl.program_id(1)

    @pl.when(j == 0)
    def _():
        m_ref[...] = jnp.full_like(m_ref, -jnp.inf)
        l_ref[...] = jnp.zeros_like(l_ref)
        acc_ref[...] = jnp.zeros_like(acc_ref)

    qa = qabs_ref[...]
    qpe = q_ref[:, A_NOPE:A_NOPE + A_ROPE]
    pages = [ckv_refs[i][...].astype(BF16) for i in range(P)]
    s = jnp.concatenate(
        [_dot_nt(qa, pages[i]) + _dot_nt(qpe, kpe_refs[i][...]) for i in range(P)], axis=1)
    m = m_ref[...]
    m_new = jnp.maximum(m, jnp.max(s, axis=-1, keepdims=True))
    alpha = jnp.exp(m - m_new)
    pr = jnp.exp(s - m_new)
    l_ref[...] = l_ref[...] * alpha + jnp.sum(pr, axis=-1, keepdims=True)
    pb = pr.astype(BF16)
    pv = jnp.dot(pb[:, :LANES], pages[0], preferred_element_type=F32)
    for i in range(1, P):
        pv += jnp.dot(pb[:, i * LANES:(i + 1) * LANES], pages[i], preferred_element_type=F32)
    acc_ref[...] = acc_ref[...] * alpha + pv
    m_ref[...] = m_new

    @pl.when(j == pl.num_programs(1) - 1)
    def _():
        cnew = cnew_ref[...]
        s_self = (jnp.sum(qa * cnew, axis=-1, keepdims=True)
                  + jnp.sum(qpe.astype(F32) * knew_ref[...], axis=-1, keepdims=True))
        m_old = m_ref[...]
        m_fin = jnp.maximum(m_old, s_self)
        al = jnp.exp(m_old - m_fin)
        p_self = jnp.exp(s_self - m_fin)
        l_fin = l_ref[...] * al + p_self
        o_lat = (acc_ref[...] * al + p_self * cnew) / l_fin
        res = _dot(o_lat, wuv_ref[...])
        hrow = lax.broadcasted_iota(jnp.int32, res.shape, 0)
        hcol = lax.broadcasted_iota(jnp.int32, res.shape, 1) // A_VDIM
        o_ref[...] = jnp.sum(jnp.where(hrow == hcol, res, 0.0), axis=0, keepdims=True).astype(o_ref.dtype)


def _mla_decode(page_table, qabs, q, ckv_new, kpe_new, wuv, cache_ckv, cache_kpe, layer):
    n, n_pages = page_table.shape
    P = DECODE_PAGES
    page = cache_ckv.shape[2]

    def page_spec(width, i):
        return pl.BlockSpec((None, None, page, width),
                            lambda b, j, pt: (layer, pt[b, j * P + i], 0, 0))

    req = lambda shape: pl.BlockSpec((None,) + shape, lambda b, j, pt: (b, 0, 0))
    grid_spec = pltpu.PrefetchScalarGridSpec(
        num_scalar_prefetch=1, grid=(n, n_pages // P),
        in_specs=[req((A_HEADS, KV_RANK)), req((A_HEADS, QK_PAD)), req((1, KV_RANK)), req((1, A_ROPE)),
                  pl.BlockSpec(wuv.shape, lambda b, j, pt: (0, 0))]
                 + [page_spec(KV_RANK, i) for i in range(P)] + [page_spec(A_ROPE, i) for i in range(P)],
        out_specs=req((1, A_HEADS * A_VDIM)),
        scratch_shapes=[pltpu.VMEM((A_HEADS, 1), F32), pltpu.VMEM((A_HEADS, 1), F32),
                        pltpu.VMEM((A_HEADS, KV_RANK), F32)])
    out = pl.pallas_call(
        _mla_decode_kernel, grid_spec=grid_spec,
        out_shape=jax.ShapeDtypeStruct((n, 1, A_HEADS * A_VDIM), BF16),
        compiler_params=_params("parallel", "arbitrary"), name="mla_decode",
    )(page_table, qabs, q, ckv_new[:, None, :], kpe_new[:, None, :], wuv,
      *([cache_ckv] * P), *([cache_kpe] * P))
    return out[:, 0, :]


def _mix_out_kernel(x_ref, oa_ref, y_ref, r_ref, k2_ref, v_ref, g_ref, gpre_ref, wga_ref, wgb_ref,
                    rk_ref, lnw_ref, lnb_ref, e_ref, wpa_ref, wpb_ref, wout_ref, gpost_ref, gmem_ref,
                    wmq_ref, x1_ref, qm_ref):
    x = x_ref[...]
    h = _rms(x, gpre_ref[...]).astype(BF16)
    gate_a = _sigmoid(jnp.dot(h, wga_ref[...], preferred_element_type=F32))
    gate_b = _sigmoid(jnp.dot(h, wgb_ref[...], preferred_element_type=F32))
    e = e_ref[...]
    y = y_ref[...]
    inv = 1.0 / B_HDIM
    d = y - _seg_sum(y, e) * inv
    var = _seg_sum(d * d, e) * inv
    yn = d * lax.rsqrt(var + GN_EPS) * lnw_ref[...] + lnb_ref[...]
    v = v_ref[...]
    bonus = _seg_sum(r_ref[...] * k2_ref[...] * rk_ref[...], e) * v
    ob = (yn + bonus) * g_ref[...]
    merged = gate_a * _dot(oa_ref[...], wpa_ref[...]) + gate_b * _dot(ob, wpb_ref[...])
    x1 = x + _rms(_dot(merged, wout_ref[...]), gpost_ref[...])
    x1_ref[...] = x1
    qm = _dot(_rms(x1, gmem_ref[...]), wmq_ref[...]) * ((x.shape[-1] // X_HEADS) ** -0.5)
    qm_ref[...] = qm.astype(BF16)


def _mix_out(x, oa, y, r, k2, v, g, *weights):
    m, d = x.shape
    tm = min(ROW_TILE, m)
    wide = _rows(tm, d)
    half = _rows(tm, B_WIDTH)
    return pl.pallas_call(
        _mix_out_kernel, grid=(m // tm,),
        in_specs=[wide] + [half] * 6 + [_full(w.shape) for w in weights],
        out_specs=[wide, wide],
        out_shape=[jax.ShapeDtypeStruct((m, d), F32), jax.ShapeDtypeStruct((m, d), BF16)],
        compiler_params=_params("parallel"), name="mix_out",
    )(x, oa, y, r, k2, v, g, *weights)


def _mem_kv_kernel(mem_ref, g_ref, wk_ref, wv_ref, k_ref, v_ref):
    h = _rms(mem_ref[...], g_ref[...]).astype(BF16)
    k_ref[...] = jnp.dot(h, wk_ref[...], preferred_element_type=F32)
    v_ref[...] = jnp.dot(h, wv_ref[...], preferred_element_type=F32)


def _mem_kv(mem, g, wk, wv):
    m, d = mem.shape
    o = jax.ShapeDtypeStruct((m, d), F32)
    return pl.pallas_call(
        _mem_kv_kernel, grid=(1,),
        in_specs=[_full(mem.shape), _full(g.shape), _full(wk.shape), _full(wv.shape)],
        out_specs=[_full((m, d))] * 2, out_shape=[o, o],
        compiler_params=_params("arbitrary"), name="mem_kv",
    )(mem, g, wk, wv)


def _softmax_rows(s):
    pr = jnp.exp(s - jnp.max(s, axis=-1, keepdims=True))
    return pr / jnp.sum(pr, axis=-1, keepdims=True)


def _mem_attn_shared_kernel(q_ref, k_ref, v_ref, o_ref):
    hd = q_ref.shape[-1] // X_HEADS
    for hh in range(X_HEADS):
        sl = slice(hh * hd, (hh + 1) * hd)
        pr = _softmax_rows(_dot_nt(q_ref[:, sl], k_ref[:, sl]))
        o_ref[:, sl] = _dot(pr, v_ref[:, sl]).astype(o_ref.dtype)


def _mem_attn_shared(q, mk, mv):
    m, d = q.shape
    tm = min(ROW_TILE, m)
    return pl.pallas_call(
        _mem_attn_shared_kernel, grid=(m // tm,),
        in_specs=[_rows(tm, d), _full(mk.shape), _full(mv.shape)],
        out_specs=_rows(tm, d), out_shape=jax.ShapeDtypeStruct((m, d), BF16),
        compiler_params=_params("parallel"), name="mem_attn_shared",
    )(q, mk, mv)


def _mem_attn_rows_kernel(q_ref, k_ref, v_ref, o_ref):
    nb, _, d = q_ref.shape
    hd = d // X_HEADS
    hrow = lax.broadcasted_iota(jnp.int32, (8, d), 0)
    hcol = lax.broadcasted_iota(jnp.int32, (8, d), 1) // hd
    own = hrow == hcol
    for i in range(nb):
        qh = jnp.where(own, jnp.broadcast_to(q_ref[i].astype(F32), (8, d)), 0.0)
        pr = _softmax_rows(_dot_nt(qh, k_ref[i]))
        res = _dot(pr, v_ref[i])
        o_ref[i] = jnp.sum(jnp.where(own, res, 0.0), axis=0, keepdims=True).astype(o_ref.dtype)


def _mem_attn_rows(q, mk, mv):
    n, d = q.shape
    nb = 4
    mem = mk.shape[1]
    qspec = pl.BlockSpec((nb, 1, d), lambda i: (i, 0, 0))
    kspec = pl.BlockSpec((nb, mem, d), lambda i: (i, 0, 0))
    out = pl.pallas_call(
        _mem_attn_rows_kernel, grid=(n // nb,),
        in_specs=[qspec, kspec, kspec], out_specs=qspec,
        out_shape=jax.ShapeDtypeStruct((n, 1, d), BF16),
        compiler_params=_params("parallel"), name="mem_attn_rows",
    )(q[:, None, :], mk, mv)
    return out[:, 0, :]


def _tail_kernel(x1_ref, om_ref, wmo_ref, gpm_ref, gmlp_ref, wup_ref, wdn_ref, gpost_ref, y_ref):
    x2 = x1_ref[...] + _rms(jnp.dot(om_ref[...], wmo_ref[...], preferred_element_type=F32), gpm_ref[...])
    h = _rms(x2, gmlp_ref[...]).astype(BF16)
    u = jnp.maximum(jnp.dot(h, wup_ref[...], preferred_element_type=F32), 0.0)
    ff = jnp.dot((u * u).astype(BF16), wdn_ref[...], preferred_element_type=F32)
    y_ref[...] = x2 + _rms(ff, gpost_ref[...])


def _tail(x1, om, *weights):
    m, d = x1.shape
    tm = min(ROW_TILE, m)
    return pl.pallas_call(
        _tail_kernel, grid=(m // tm,),
        in_specs=[_rows(tm, d), _rows(tm, d)] + [_full(w.shape) for w in weights],
        out_specs=_rows(tm, d), out_shape=jax.ShapeDtypeStruct((m, d), F32),
        compiler_params=_params("parallel"), name="tail",
    )(x1, om, *weights)


def _rot_cols(w):
    half = w.shape[-1] // 2
    return jnp.concatenate([-w[..., half:], w[..., :half]], axis=-1)


def _rope_tables(pos):
    inv = ROPE_THETA ** (-jnp.arange(0, A_ROPE, 2, dtype=F32) / A_ROPE)
    ang = pos.astype(F32)[:, None] * inv[None, :]
    z_lo = jnp.zeros((pos.shape[0], A_NOPE), F32)
    z_hi = jnp.zeros((pos.shape[0], QK_PAD - A_NOPE - A_ROPE), F32)
    cos, sin = jnp.cos(ang), jnp.sin(ang)
    return (jnp.concatenate([z_lo, cos, cos, z_hi], axis=1),
            jnp.concatenate([z_lo, sin, sin, z_hi], axis=1))


def _prep_layer(l, w_in, w_uq, w_uk, w_uv, rw_decay_up, rw_a_up, rw_g_up):
    d = w_in.shape[1]
    wi = w_in[l]
    o_rw = Q_RANK + KV_RANK + A_ROPE
    o_ga = o_rw + RW_COLS
    w_kpe = wi[:, Q_RANK + KV_RANK:o_rw]
    z = lambda n: jnp.zeros((d, n), F32)
    pad_hi = QK_PAD - A_NOPE - A_ROPE
    wlat = jnp.concatenate([wi[:, :Q_RANK + KV_RANK], z(A_NOPE), w_kpe, z(pad_hi),
                            z(A_NOPE), _rot_cols(w_kpe), z(pad_hi)], axis=1).astype(BF16)
    wrw = jnp.pad(wi[:, o_rw:o_ga], ((0, 0), (0, RW_PAD - RW_COLS))).astype(BF16)
    wga = wi[:, o_ga:o_ga + d].astype(BF16)
    wgb = wi[:, o_ga + d:].astype(BF16)
    uq = w_uq[l].reshape(Q_RANK, A_HEADS, A_NOPE + A_ROPE)
    nope, pe = uq[..., :A_NOPE], uq[..., A_NOPE:]
    zq = lambda n: jnp.zeros((Q_RANK, A_HEADS, n), F32)
    wqa = jnp.concatenate([nope, pe, zq(pad_hi)], axis=-1).reshape(Q_RANK, -1).astype(BF16)
    wqb = jnp.concatenate([zq(A_NOPE), _rot_cols(pe), zq(pad_hi)], axis=-1).reshape(Q_RANK, -1).astype(BF16)
    wk_cols = jnp.pad(w_uk[l], ((0, 0), (0, 0), (0, QK_PAD - A_NOPE))).reshape(KV_RANK, -1).astype(BF16)
    wk_rows = jnp.pad(jnp.transpose(w_uk[l], (1, 2, 0)), ((0, 0), (0, QK_PAD - A_NOPE), (0, 0))).astype(BF16)
    wv = w_uv[l].reshape(KV_RANK, -1).astype(BF16)
    wl = jnp.zeros((LORA_IN, 3 * B_WIDTH), F32)
    wl = wl.at[:DECAY_LORA, :B_WIDTH].set(rw_decay_up[l])
    wl = wl.at[DECAY_LORA:DECAY_LORA + AAA_LORA, B_WIDTH:2 * B_WIDTH].set(rw_a_up[l])
    wl = wl.at[DECAY_LORA + AAA_LORA:, 2 * B_WIDTH:].set(rw_g_up[l])
    return wlat, wrw, wga, wgb, wqa, wqb, wk_cols, wk_rows, wv, wl.astype(BF16)


def kernel(x_prompt, x_sample, mem_prompt, cache_ckv, cache_kpe, state_wkv, state_shift, cache_mem_k, cache_mem_v, page_table, norm_pre_mix, w_in, q_norm, w_uq, kv_norm, w_uk, w_uv, rw_mu, rw_w0, rw_decay_up, rw_a0, rw_a_up, rw_g_up, rw_k_k, rw_k_a, rw_r_k, rw_ln_w, rw_ln_b, w_proj_a, w_proj_b, w_out, norm_post_mix, norm_pre_mem, mem_norm, w_mq, w_mk, w_mv, w_mo, norm_post_mem, norm_pre_mlp, w_ff_up, w_ff_down, norm_post_mlp):
    depth = w_in.shape[0]
    n_p, seq, d = x_prompt.shape
    n_s, dec_seq, _ = x_sample.shape
    assert n_p == 1 and dec_seq == 1, "one prompt sequence and one new token per decode request"
    past_len = page_table.shape[1] * cache_ckv.shape[2]
    ck_p, sk_p = _rope_tables(jnp.arange(seq))
    ck_s, sk_s = _rope_tables(jnp.full((n_s,), past_len))
    seg = jnp.arange(B_WIDTH) // B_HDIM
    e = (seg[:, None] == seg[None, :]).astype(BF16)
    row = lambda p, l: p[l].reshape(1, -1)
    bf = lambda p, l: p[l].astype(BF16)

    y_p = x_prompt.reshape(seq, d)
    y_s = x_sample.reshape(n_s, d)
    outs = [[] for _ in range(10)]
    for l in range(depth):
        wlat, wrw, wga, wgb, wqa, wqb, wk_cols, wk_rows, wv, wl = _prep_layer(
            l, w_in, w_uq, w_uk, w_uv, rw_decay_up, rw_a_up, rw_g_up)
        proj_w = (row(norm_pre_mix, l), wlat, wrw, row(q_norm, l), row(kv_norm, l), wqa, wqb)
        prep_w = (row(rw_mu, l), row(rw_w0, l), row(rw_a0, l), wl, row(rw_k_k, l), row(rw_k_a, l), e)
        mix_w = (row(norm_pre_mix, l), wga, wgb, row(rw_r_k, l), row(rw_ln_w, l), row(rw_ln_b, l), e,
                 bf(w_proj_a, l), bf(w_proj_b, l), bf(w_out, l), row(norm_post_mix, l),
                 row(norm_pre_mem, l), bf(w_mq, l))
        tail_w = (bf(w_mo, l), row(norm_post_mem, l), row(norm_pre_mlp, l), bf(w_ff_up, l),
                  bf(w_ff_down, l), row(norm_post_mlp, l))

        q, ckv_p, kpe_p, rw_p, k, v = _proj_in(y_p, *proj_w, ck_p, sk_p, wk_cols, wv, prompt=True)
        oa_p = _mla_prompt(q, k, v)
        shifted = jnp.concatenate([jnp.zeros((1, RW_COLS), F32), rw_p[:-1]], axis=0)
        r_, lw_, k2_, v_, na_, b_, g_ = _rwkv_prep(rw_p, shifted, *prep_w)
        yb_p, wkv_p = _rwkv_chunked(r_, lw_, k2_, v_, na_, b_)
        x1_p, qm_p = _mix_out(y_p, oa_p, yb_p, r_, k2_, v_, g_, *mix_w)
        mk_p, mv_p = _mem_kv(mem_prompt.reshape(-1, d), row(mem_norm, l), bf(w_mk, l), bf(w_mv, l))
        om_p = _mem_attn_shared(qm_p, mk_p, mv_p)
        y_p = _tail(x1_p, om_p, *tail_w)

        q, ckv_s, kpe_s, rw_s, qabs = _proj_in(y_s, *proj_w, ck_s, sk_s, wk_rows, wv, prompt=False)
        oa_s = _mla_decode(page_table, jnp.transpose(qabs, (1, 0, 2)), jnp.transpose(q, (1, 0, 2)),
                           ckv_s, kpe_s, wv, cache_ckv, cache_kpe, l)
        r_, lw_, k2_, v_, na_, b_, g_ = _rwkv_prep(rw_s, state_shift[l], *prep_w)
        yb_s, wkv_s = _rwkv_step(state_wkv[l], r_, lw_, k2_, v_, na_, b_)
        x1_s, qm_s = _mix_out(y_s, oa_s, yb_s, r_, k2_, v_, g_, *mix_w)
        mem_shape = (n_s, cache_mem_k.shape[2], d)
        om_s = _mem_attn_rows(qm_s, cache_mem_k[l].reshape(mem_shape), cache_mem_v[l].reshape(mem_shape))
        y_s = _tail(x1_s, om_s, *tail_w)

        mem_heads = (n_p, -1, X_HEADS, d // X_HEADS)
        layer_outs = (ckv_p.reshape(n_p, seq, KV_RANK), kpe_p.reshape(n_p, seq, A_ROPE),
                      wkv_p[None], rw_p[-1:], mk_p.reshape(mem_heads), mv_p.reshape(mem_heads),
                      ckv_s.reshape(n_s, 1, KV_RANK), kpe_s.reshape(n_s, 1, A_ROPE), wkv_s, rw_s)
        for lst, val in zip(outs, layer_outs):
            lst.append(val)
    return (y_p.reshape(n_p, seq, d), y_s.reshape(n_s, 1, d)) + tuple(jnp.stack(o) for o in outs)
```

```python
import functools
import math

import jax
import jax.numpy as jnp
from jax import lax
from jax.experimental import pallas as pl
from jax.experimental.pallas import tpu as pltpu

F32 = jnp.float32
BF16 = jnp.bfloat16

A_HEADS = 8
A_NOPE = 64
A_ROPE = 32
A_VDIM = 64
Q_RANK = 384
KV_RANK = 256
ROPE_THETA = 10000.0
B_HEADS = 8
B_HDIM = 64
B_WIDTH = B_HEADS * B_HDIM
DECAY_LORA = 64
AAA_LORA = 64
GATE_LORA = 160
LORA_IN = DECAY_LORA + AAA_LORA + GATE_LORA
GN_EPS = 64e-5
X_HEADS = 4
NORM_EPS = 1e-6
RW_COLS = 3 * B_WIDTH + LORA_IN

LANES = 128
SUBLANES = 8
QK_PAD = 128
RW_PAD = 15 * LANES
RWKV_CHUNK = 64
RWKV_BLOCK = 128
ATTN_Q_BLOCK = 256
ATTN_KV_BLOCK = 1024
DECODE_PAGES = 16
ROW_TILE = 256
VMEM_LIMIT = 48 * 1024 * 1024
QK_SCALE = (A_NOPE + A_ROPE) ** -0.5 * math.log2(math.e)


def _params(*sem):
    return pltpu.CompilerParams(dimension_semantics=sem, vmem_limit_bytes=VMEM_LIMIT)


def _full(shape):
    zeros = (0,) * len(shape)
    return pl.BlockSpec(shape, lambda *_: zeros)


def _rows(tm, width):
    return pl.BlockSpec((tm, width), lambda i: (i, 0))


def _rms(x, g):
    return x * lax.rsqrt(jnp.mean(x * x, axis=-1, keepdims=True) + NORM_EPS) * g


def _dot(a, b):
    return jnp.dot(a.astype(BF16), b.astype(BF16), preferred_element_type=F32)


def _dot_nt(a, b):
    return lax.dot_general(a.astype(BF16), b.astype(BF16), (((1,), (1,)), ((), ())),
                           preferred_element_type=F32)


def _dot_tn(a, b):
    return lax.dot_general(a.astype(BF16), b.astype(BF16), (((0,), (0,)), ((), ())),
                           preferred_element_type=F32)


def _sigmoid(x):
    return 1.0 / (1.0 + jnp.exp(-x))


def _seg_sum(x, e):
    hi = x.astype(BF16)
    lo = (x - hi.astype(F32)).astype(BF16)
    return (jnp.dot(hi, e, preferred_element_type=F32)
            + jnp.dot(lo, e, preferred_element_type=F32))


def _proj_in_kernel(prompt, x_ref, g_ref, wlat_ref, wrw_ref, qn_ref, kvn_ref, wqa_ref, wqb_ref,
                    ck_ref, sk_ref, wk_ref, wv_ref, *outs):
    if prompt:
        q_ref, ckv_ref, kpe_ref, rw_ref, k_ref, v_ref = outs
    else:
        q_ref, ckv_ref, kpe_ref, rw_ref, qabs_ref = outs
    h = _rms(x_ref[...], g_ref[...]).astype(BF16)
    lat = jnp.dot(h, wlat_ref[...], preferred_element_type=F32)
    cq = lat[:, :Q_RANK]
    ckv = _rms(lat[:, Q_RANK:Q_RANK + KV_RANK], kvn_ref[...])
    ck = ck_ref[...]
    sk = sk_ref[...]
    kpe = lat[:, 640:768] * ck + lat[:, 768:896] * sk
    ckv_ref[...] = ckv
    kpe_ref[...] = kpe[:, A_NOPE:A_NOPE + A_ROPE]
    rw = jnp.dot(h, wrw_ref[...], preferred_element_type=F32)
    rw_ref[...] = rw[:, :RW_COLS]
    qn = _rms(cq, qn_ref[...]).astype(BF16)
    qa = jnp.dot(qn, wqa_ref[...], preferred_element_type=F32)
    qb = jnp.dot(qn, wqb_ref[...], preferred_element_type=F32)
    lane = lax.broadcasted_iota(jnp.int32, ck.shape, 1)
    cq_t = QK_SCALE * (ck + jnp.where(lane < A_NOPE, 1.0, 0.0))
    sq_t = QK_SCALE * sk
    ckv_b = ckv.astype(BF16)
    if prompt:
        kn = jnp.dot(ckv_b, wk_ref[...], preferred_element_type=F32)
        vn = jnp.dot(ckv_b, wv_ref[...], preferred_element_type=F32)
        ones_col = jnp.where(lane == A_VDIM, 1.0, 0.0)
    for hh in range(A_HEADS):
        sl = slice(hh * QK_PAD, (hh + 1) * QK_PAD)
        qh = (qa[:, sl] * cq_t + qb[:, sl] * sq_t).astype(BF16)
        q_ref[hh] = qh
        if prompt:
            k_ref[hh] = (kn[:, sl] + kpe).astype(BF16)
            v_ref[hh] = (vn[:, sl] + ones_col).astype(BF16)
        else:
            qabs_ref[hh] = jnp.dot(qh, wk_ref[hh], preferred_element_type=F32)


def _proj_in(x, g, wlat, wrw, qn, kvn, wqa, wqb, ck, sk, wk, wv, *, prompt):
    m, d = x.shape
    tm = min(ROW_TILE, m)
    heads_out = lambda w, dt: jax.ShapeDtypeStruct((A_HEADS, m, w), dt)
    heads_spec = lambda w: pl.BlockSpec((A_HEADS, tm, w), lambda i: (0, i, 0))
    out_shape = [heads_out(QK_PAD, BF16), jax.ShapeDtypeStruct((m, KV_RANK), F32),
                 jax.ShapeDtypeStruct((m, A_ROPE), F32), jax.ShapeDtypeStruct((m, RW_COLS), F32)]
    out_specs = [heads_spec(QK_PAD), _rows(tm, KV_RANK), _rows(tm, A_ROPE), _rows(tm, RW_COLS)]
    if prompt:
        out_shape += [heads_out(QK_PAD, BF16), heads_out(QK_PAD, BF16)]
        out_specs += [heads_spec(QK_PAD), heads_spec(QK_PAD)]
    else:
        out_shape += [heads_out(KV_RANK, F32)]
        out_specs += [heads_spec(KV_RANK)]
    return pl.pallas_call(
        functools.partial(_proj_in_kernel, prompt),
        grid=(m // tm,),
        in_specs=[_rows(tm, d), _full(g.shape), _full(wlat.shape), _full(wrw.shape), _full(qn.shape),
                  _full(kvn.shape), _full(wqa.shape), _full(wqb.shape), _rows(tm, LANES),
                  _rows(tm, LANES), _full(wk.shape), _full(wv.shape)],
        out_specs=out_specs, out_shape=out_shape,
        compiler_params=_params("parallel"), name="proj_in",
    )(x, g, wlat, wrw, qn, kvn, wqa, wqb, ck, sk, wk, wv)


def _rwkv_prep_kernel(seq, rw_ref, prev_ref, first_ref, mu_ref, w0_ref, a0_ref, wl_ref, kk_ref, ka_ref,
                      e_ref, r_ref, lw_ref, k2_ref, v_ref, na_ref, b_ref, g_ref):
    rw = rw_ref[...]
    if seq:
        above = jnp.where(pl.program_id(0) == 0, first_ref[...], prev_ref[SUBLANES - 1:SUBLANES, :])
        ridx = lax.broadcasted_iota(jnp.int32, rw.shape, 0)
        shifted = jnp.where(ridx == 0, above, pltpu.roll(rw, 1, axis=0))
    else:
        shifted = first_ref[...]
    xs = rw + (shifted - rw) * mu_ref[...]
    r = xs[:, :B_WIDTH]
    k = xs[:, B_WIDTH:2 * B_WIDTH]
    v = xs[:, 2 * B_WIDTH:3 * B_WIDTH]
    lo = xs[:, 3 * B_WIDTH:]
    lane = lax.broadcasted_iota(jnp.int32, lo.shape, 1)
    t = jnp.where(lane < DECAY_LORA, jnp.tanh(lo),
                  jnp.where(lane < DECAY_LORA + AAA_LORA, lo, _sigmoid(lo)))
    l3 = jnp.dot(t.astype(BF16), wl_ref[...], preferred_element_type=F32)
    z = -(w0_ref[...] + l3[:, :B_WIDTH])
    softplus = jnp.maximum(z, 0.0) + jnp.log(1.0 + jnp.exp(-jnp.abs(z)))
    w = -softplus - 0.5
    a = _sigmoid(a0_ref[...] + l3[:, B_WIDTH:2 * B_WIDTH])
    kkr = k * kk_ref[...]
    norm = jnp.sqrt(_seg_sum(kkr * kkr, e_ref[...]))
    kk = kkr / jnp.maximum(norm, 1e-12)
    r_ref[...] = r
    lw_ref[...] = -jnp.exp(w)
    k2_ref[...] = k * (1.0 + (a - 1.0) * ka_ref[...])
    v_ref[...] = v
    na_ref[...] = -kk
    b_ref[...] = kk * a
    g_ref[...] = l3[:, 2 * B_WIDTH:]


def _rwkv_prep(rw, first, mu, w0, a0, wl, k_k, k_a, e, *, seq):
    m = rw.shape[0]
    tm = min(ROW_TILE, m)
    o = jax.ShapeDtypeStruct((m, B_WIDTH), F32)
    per_tile = tm // SUBLANES
    prev_spec = pl.BlockSpec((SUBLANES, RW_COLS), lambda i: (jnp.maximum(i * per_tile - 1, 0), 0))
    first_spec = _full(first.shape) if seq else _rows(tm, RW_COLS)
    return pl.pallas_call(
        functools.partial(_rwkv_prep_kernel, seq), grid=(m // tm,),
        in_specs=[_rows(tm, RW_COLS), prev_spec, first_spec, _full(mu.shape), _full(w0.shape),
                  _full(a0.shape), _full(wl.shape), _full(k_k.shape), _full(k_a.shape), _full(e.shape)],
        out_specs=[_rows(tm, B_WIDTH)] * 7, out_shape=[o] * 7,
        compiler_params=_params("parallel"), name="rwkv_prep",
    )(rw, rw, first, mu, w0, a0, wl, k_k, k_a, e)


def _rwkv_chunk_kernel(r_ref, lw_ref, k2_ref, v_ref, na_ref, b_ref, y_ref, sout_ref, s_ref):
    C = RWKV_CHUNK
    n_chunks = r_ref.shape[0] // C
    pairs = B_HEADS // 2

    @pl.when(pl.program_id(0) == 0)
    def _():
        s_ref[...] = jnp.zeros_like(s_ref)

    row = lax.broadcasted_iota(jnp.int32, (2 * C, 2 * C), 0)
    col = lax.broadcasted_iota(jnp.int32, (2 * C, 2 * C), 1)
    tok_r = jnp.bitwise_and(row, C - 1)
    tok_c = jnp.bitwise_and(col, C - 1)
    strict = tok_c < tok_r
    incl = tok_c <= tok_r
    eye = jnp.where(row == col, 1.0, 0.0)
    crow = lax.broadcasted_iota(jnp.int32, (C, C), 0)
    ccol = lax.broadcasted_iota(jnp.int32, (C, C), 1)
    cum = jnp.where(ccol <= crow, 1.0, 0.0).astype(BF16)
    head0 = lax.broadcasted_iota(jnp.int32, (C, LANES), 1) < B_HDIM

    def stack(x):
        return jnp.concatenate([jnp.where(head0, x, 0.0), jnp.where(head0, 0.0, x)], axis=0)

    units = [(ci, p) for ci in range(n_chunks) for p in range(pairs)]

    G_all = []
    for ci in range(n_chunks):
        lw = lw_ref[ci * C:(ci + 1) * C, :]
        l_hi = lw.astype(BF16)
        rem = lw - l_hi.astype(F32)
        l_mid = rem.astype(BF16)
        l_lo = (rem - l_mid.astype(F32)).astype(BF16)
        g3 = jnp.dot(cum, jnp.concatenate([l_hi, l_mid, l_lo], axis=1), preferred_element_type=F32)
        G_all.append(g3[:, :B_WIDTH] + g3[:, B_WIDTH:2 * B_WIDTH] + g3[:, 2 * B_WIDTH:])

    ops = {}
    for ci, p in units:
        rows = slice(ci * C, (ci + 1) * C)
        sl = slice(p * LANES, (p + 1) * LANES)
        lw = lw_ref[rows, sl]
        G = G_all[ci][:, sl]
        GC = G[C - 1:C, :]
        e_pos = jnp.exp(G)
        e_neg = jnp.exp(-G)
        e_prev = jnp.exp(G - lw)
        e_tail = jnp.exp(GC - G)
        r, k2, v = r_ref[rows, sl], k2_ref[rows, sl], v_ref[rows, sl]
        na, b = na_ref[rows, sl], b_ref[rows, sl]
        ops[ci, p] = dict(
            AR=jnp.concatenate([stack(na * e_prev), stack(r * e_pos)], axis=0).astype(BF16),
            BK=jnp.concatenate([stack(b * e_neg), stack(k2 * e_neg)], axis=0).astype(BF16),
            Bh=stack(b * e_tail).astype(BF16), Kh=stack(k2 * e_tail).astype(BF16),
            Vs=stack(v).astype(BF16), decay=jnp.exp(GC))

    AA = {u: _dot_nt(ops[u]["AR"], ops[u]["BK"]) for u in units}
    X, Pw, AakArk, Arb = {}, {}, {}, {}
    for u in units:
        aa = AA[u]
        a_ab = jnp.where(strict, aa[:2 * C, :2 * C], 0.0)
        AakArk[u] = jnp.concatenate([jnp.where(strict, aa[:2 * C, 2 * C:], 0.0),
                                     jnp.where(incl, aa[2 * C:, 2 * C:], 0.0)], axis=0).astype(BF16)
        Arb[u] = jnp.where(incl, aa[2 * C:, :2 * C], 0.0).astype(BF16)
        X[u] = eye + a_ab
        Pw[u] = a_ab
    sq = {u: _dot(Pw[u], Pw[u]) for u in units}
    n = 2
    while 2 * n < C:
        both = {u: _dot(jnp.concatenate([X[u], sq[u]], axis=0), sq[u]) for u in units}
        for u in units:
            X[u] = X[u] + both[u][:2 * C]
            sq[u] = both[u][2 * C:]
        n *= 2
    last = {u: _dot(X[u], sq[u]) for u in units}
    for u in units:
        X[u] = (X[u] + last[u]).astype(BF16)
    AV = {u: _dot(AakArk[u], ops[u]["Vs"]) for u in units}
    VK = {u: _dot_tn(ops[u]["Vs"], ops[u]["Kh"]) for u in units}

    S = [s_ref[p] for p in range(pairs)]
    for ci in range(n_chunks):
        SS = [_dot_nt(ops[ci, p]["AR"], S[p]) for p in range(pairs)]
        U = [_dot(X[ci, p], SS[p][:2 * C] + AV[ci, p][:2 * C]) for p in range(pairs)]
        YU = [_dot(Arb[ci, p], U[p]) for p in range(pairs)]
        UB = [_dot_tn(U[p], ops[ci, p]["Bh"]) for p in range(pairs)]
        for p in range(pairs):
            Y = SS[p][2 * C:] + YU[p] + AV[ci, p][2 * C:]
            y_ref[ci * C:(ci + 1) * C, p * LANES:(p + 1) * LANES] = Y[:C] + Y[C:]
            S[p] = S[p] * ops[ci, p]["decay"] + UB[p] + VK[ci, p]
    for p in range(pairs):
        s_ref[p] = S[p]

    @pl.when(pl.program_id(0) == pl.num_programs(0) - 1)
    def _():
        sout_ref[...] = s_ref[...]


def _rwkv_chunked(r, lw, k2, v, na, b):
    t = r.shape[0]
    spec = _rows(RWKV_BLOCK, B_WIDTH)
    pairs = B_HEADS // 2
    y, s = pl.pallas_call(
        _rwkv_chunk_kernel, grid=(t // RWKV_BLOCK,),
        in_specs=[spec] * 6,
        out_specs=[spec, _full((pairs, LANES, LANES))],
        out_shape=[jax.ShapeDtypeStruct((t, B_WIDTH), F32),
                   jax.ShapeDtypeStruct((pairs, LANES, LANES), F32)],
        scratch_shapes=[pltpu.VMEM((pairs, LANES, LANES), F32)],
        compiler_params=_params("arbitrary"), name="rwkv_chunked",
    )(r, lw, k2, v, na, b)
    s = s.reshape(pairs, 2, B_HDIM, 2, B_HDIM)
    wkv = jnp.stack([s[:, 0, :, 0, :], s[:, 1, :, 1, :]], axis=1).reshape(B_HEADS, B_HDIM, B_HDIM)
    return y, wkv


def _rwkv_step_kernel(s_ref, r_ref, lw_ref, k2_ref, na_ref, b_ref, v_ref, sout_ref, y_ref):
    S = s_ref[...]
    sa = jnp.sum(S * na_ref[...], axis=1, keepdims=True)
    S = S * jnp.exp(lw_ref[...]) + sa * b_ref[...] + v_ref[...] * k2_ref[...]
    sout_ref[...] = S
    y_ref[...] = jnp.sum(S * r_ref[...], axis=1, keepdims=True)


def _rwkv_step(state, r, lw, k2, v, na, b):
    n = state.shape[0]
    keyed = lambda x: x.T.reshape(B_HEADS, 1, B_HDIM, n)
    kspec = pl.BlockSpec((None, 1, B_HDIM, n), lambda h: (h, 0, 0, 0))
    vspec = pl.BlockSpec((None, B_HDIM, 1, n), lambda h: (h, 0, 0, 0))
    sspec = pl.BlockSpec((None, B_HDIM, B_HDIM, n), lambda h: (h, 0, 0, 0))
    s_new, y = pl.pallas_call(
        _rwkv_step_kernel, grid=(B_HEADS,),
        in_specs=[sspec, kspec, kspec, kspec, kspec, kspec, vspec],
        out_specs=[sspec, vspec],
        out_shape=[jax.ShapeDtypeStruct((B_HEADS, B_HDIM, B_HDIM, n), F32),
                   jax.ShapeDtypeStruct((B_HEADS, B_HDIM, 1, n), F32)],
        compiler_params=_params("parallel"), name="rwkv_step",
    )(jnp.transpose(state, (1, 2, 3, 0)), keyed(r), keyed(lw), keyed(k2), keyed(na), keyed(b),
      v.T.reshape(B_HEADS, B_HDIM, 1, n))
    return y.reshape(B_WIDTH, n).T, jnp.transpose(s_new, (3, 0, 1, 2))


def _mla_prompt_kernel(q_ref, k_ref, v_ref, o_ref):
    qi = pl.program_id(1)
    TQ, TK = ATTN_Q_BLOCK, ATTN_KV_BLOCK
    q = (q_ref[0], q_ref[1])
    row = lax.broadcasted_iota(jnp.int32, (TQ, TK), 0) + qi * TQ
    col = lax.broadcasted_iota(jnp.int32, (TQ, TK), 1)

    def block(ks, carry, masked):
        start = pl.multiple_of(ks * TK, TK)
        s = [_dot_nt(q[hh], k_ref[hh, pl.ds(start, TK), :]) for hh in range(2)]
        pr, alpha, m_out = [], [], []
        for hh in range(2):
            m = carry[hh][0]
            sh = jnp.where(col + start <= row, s[hh], -jnp.inf) if masked else s[hh]
            m_new = jnp.maximum(m, jnp.max(sh, axis=-1, keepdims=True))
            alpha.append(jnp.exp2(m - m_new))
            pr.append(jnp.exp2(sh - m_new).astype(BF16))
            m_out.append(m_new)
        pv = [jnp.dot(pr[hh], v_ref[hh, pl.ds(start, TK), :], preferred_element_type=F32)
              for hh in range(2)]
        return tuple((m_out[hh], carry[hh][1] * alpha[hh] + pv[hh]) for hh in range(2))

    init = tuple((jnp.full((TQ, 1), -jnp.inf, F32), jnp.zeros((TQ, LANES), F32)) for _ in range(2))
    n_full = (qi * TQ) // TK
    carry = lax.fori_loop(0, n_full, lambda ks, cr: block(ks, cr, False), init)
    (_, a0), (_, a1) = block(n_full, carry, True)
    o0 = a0 / a0[:, A_VDIM:A_VDIM + 1]
    o1 = a1 / a1[:, A_VDIM:A_VDIM + 1]
    head0 = lax.broadcasted_iota(jnp.int32, (TQ, LANES), 1) < A_VDIM
    o_ref[...] = jnp.where(head0, o0, pltpu.roll(o1, A_VDIM, axis=1)).astype(o_ref.dtype)


def _mla_prompt(q, k, v):
    t = q.shape[1]
    TQ = ATTN_Q_BLOCK
    resident = pl.BlockSpec((2, t, QK_PAD), lambda p, i: (p, 0, 0), pipeline_mode=pl.Buffered(1))
    return pl.pallas_call(
        _mla_prompt_kernel, grid=(A_HEADS // 2, t // TQ),
        in_specs=[pl.BlockSpec((2, TQ, QK_PAD), lambda p, i: (p, i, 0)), resident, resident],
        out_specs=pl.BlockSpec((TQ, LANES), lambda p, i: (i, p)),
        out_shape=jax.ShapeDtypeStruct((t, A_HEADS * A_VDIM), BF16),
        compiler_params=_params("parallel", "arbitrary"), name="mla_prompt",
    )(q, k, v)


def _mla_decode_kernel(pt_ref, qabs_ref, q_ref, cnew_ref, knew_ref, wuv_ref, *rest):
    P = DECODE_PAGES
    ckv_refs, kpe_refs = rest[:P], rest[P:2 * P]
    o_ref, m_ref, l_ref, acc_ref, kcat_ref, pcat_ref = rest[2 * P:]
    j = pl.program_id(1)
    page = ckv_refs[0].shape[0]

    @pl.when(j == 0)
    def _():
        m_ref[...] = jnp.full_like(m_ref, -jnp.inf)
        l_ref[...] = jnp.zeros_like(l_ref)
        acc_ref[...] = jnp.zeros_like(acc_ref)

    for i in range(P):
        kcat_ref[i * page:(i + 1) * page, :] = ckv_refs[i][...].astype(BF16)
        pcat_ref[:, i * page:(i + 1) * page] = kpe_refs[i][...].astype(BF16)
    qa = qabs_ref[...]
    qpe = q_ref[:, A_NOPE:A_NOPE + A_ROPE]
    kcat = kcat_ref[...]
    s = _dot_nt(qa, kcat) + jnp.dot(qpe, pcat_ref[...], preferred_element_type=F32)
    m = m_ref[...]
    m_new = jnp.maximum(m, jnp.max(s, axis=-1, keepdims=True))
    alpha = jnp.exp2(m - m_new)
    pr = jnp.exp2(s - m_new)
    l_ref[...] = l_ref[...] * alpha + jnp.sum(pr, axis=-1, keepdims=True)
    acc_ref[...] = acc_ref[...] * alpha + jnp.dot(pr.astype(BF16), kcat, preferred_element_type=F32)
    m_ref[...] = m_new

    @pl.when(j == pl.num_programs(1) - 1)
    def _():
        cnew = cnew_ref[...]
        s_self = (jnp.sum(qa * cnew, axis=-1, keepdims=True)
                  + jnp.sum(qpe.astype(F32) * knew_ref[...], axis=-1, keepdims=True))
        m_old = m_ref[...]
        m_fin = jnp.maximum(m_old, s_self)
        al = jnp.exp2(m_old - m_fin)
        p_self = jnp.exp2(s_self - m_fin)
        l_fin = l_ref[...] * al + p_self
        o_lat = (acc_ref[...] * al + p_self * cnew) / l_fin
        res = _dot(o_lat, wuv_ref[...])
        hrow = lax.broadcasted_iota(jnp.int32, res.shape, 0)
        hcol = lax.broadcasted_iota(jnp.int32, res.shape, 1) // A_VDIM
        o_ref[...] = jnp.sum(jnp.where(hrow == hcol, res, 0.0), axis=0, keepdims=True).astype(o_ref.dtype)


def _mla_decode(page_table, qabs, q, ckv_new, kpe_new, wuv, cache_ckv, cache_kpe_t, layer):
    n, n_pages = page_table.shape
    P = DECODE_PAGES
    page = cache_ckv.shape[2]

    def page_spec(shape, i):
        return pl.BlockSpec((None, None) + shape, lambda b, j, pt: (layer, pt[b, j * P + i], 0, 0))

    req = lambda shape: pl.BlockSpec((None,) + shape, lambda b, j, pt: (b, 0, 0))
    grid_spec = pltpu.PrefetchScalarGridSpec(
        num_scalar_prefetch=1, grid=(n, n_pages // P),
        in_specs=[req((A_HEADS, KV_RANK)), req((A_HEADS, QK_PAD)), req((1, KV_RANK)), req((1, A_ROPE)),
                  pl.BlockSpec(wuv.shape, lambda b, j, pt: (0, 0))]
                 + [page_spec((page, KV_RANK), i) for i in range(P)]
                 + [page_spec((A_ROPE, page), i) for i in range(P)],
        out_specs=req((1, A_HEADS * A_VDIM)),
        scratch_shapes=[pltpu.VMEM((A_HEADS, 1), F32), pltpu.VMEM((A_HEADS, 1), F32),
                        pltpu.VMEM((A_HEADS, KV_RANK), F32), pltpu.VMEM((P * page, KV_RANK), BF16),
                        pltpu.VMEM((A_ROPE, P * page), BF16)])
    out = pl.pallas_call(
        _mla_decode_kernel, grid_spec=grid_spec,
        out_shape=jax.ShapeDtypeStruct((n, 1, A_HEADS * A_VDIM), BF16),
        compiler_params=_params("parallel", "arbitrary"), name="mla_decode",
    )(page_table, qabs, q, ckv_new[:, None, :], kpe_new[:, None, :], wuv,
      *([cache_ckv] * P), *([cache_kpe_t] * P))
    return out[:, 0, :]


def _mix_out_kernel(x_ref, oa_ref, y_ref, r_ref, k2_ref, v_ref, g_ref, gpre_ref, wga_ref, wgb_ref,
                    rk_ref, lnw_ref, lnb_ref, e_ref, wpa_ref, wpb_ref, wout_ref, gpost_ref, gmem_ref,
                    wmq_ref, x1_ref, qm_ref):
    x = x_ref[...]
    h = _rms(x, gpre_ref[...]).astype(BF16)
    gate_a = _sigmoid(jnp.dot(h, wga_ref[...], preferred_element_type=F32))
    gate_b = _sigmoid(jnp.dot(h, wgb_ref[...], preferred_element_type=F32))
    e = e_ref[...]
    y = y_ref[...]
    inv = 1.0 / B_HDIM
    d = y - _seg_sum(y, e) * inv
    var = _seg_sum(d * d, e) * inv
    yn = d * lax.rsqrt(var + GN_EPS) * lnw_ref[...] + lnb_ref[...]
    v = v_ref[...]
    bonus = _seg_sum(r_ref[...] * k2_ref[...] * rk_ref[...], e) * v
    ob = (yn + bonus) * g_ref[...]
    merged = gate_a * _dot(oa_ref[...], wpa_ref[...]) + gate_b * _dot(ob, wpb_ref[...])
    x1 = x + _rms(_dot(merged, wout_ref[...]), gpost_ref[...])
    x1_ref[...] = x1
    qm = _dot(_rms(x1, gmem_ref[...]), wmq_ref[...]) * ((x.shape[-1] // X_HEADS) ** -0.5)
    qm_ref[...] = qm.astype(BF16)


def _mix_out(x, oa, y, r, k2, v, g, *weights):
    m, d = x.shape
    tm = min(ROW_TILE, m)
    wide = _rows(tm, d)
    half = _rows(tm, B_WIDTH)
    return pl.pallas_call(
        _mix_out_kernel, grid=(m // tm,),
        in_specs=[wide] + [half] * 6 + [_full(w.shape) for w in weights],
        out_specs=[wide, wide],
        out_shape=[jax.ShapeDtypeStruct((m, d), F32), jax.ShapeDtypeStruct((m, d), BF16)],
        compiler_params=_params("parallel"), name="mix_out",
    )(x, oa, y, r, k2, v, g, *weights)


def _mem_kv_kernel(mem_ref, g_ref, wk_ref, wv_ref, k_ref, v_ref):
    h = _rms(mem_ref[...], g_ref[...]).astype(BF16)
    k_ref[...] = jnp.dot(h, wk_ref[...], preferred_element_type=F32)
    v_ref[...] = jnp.dot(h, wv_ref[...], preferred_element_type=F32)


def _mem_kv(mem, g, wk, wv):
    m, d = mem.shape
    o = jax.ShapeDtypeStruct((m, d), F32)
    return pl.pallas_call(
        _mem_kv_kernel, grid=(1,),
        in_specs=[_full(mem.shape), _full(g.shape), _full(wk.shape), _full(wv.shape)],
        out_specs=[_full((m, d))] * 2, out_shape=[o, o],
        compiler_params=_params("arbitrary"), name="mem_kv",
    )(mem, g, wk, wv)


def _softmax_rows(s):
    pr = jnp.exp(s - jnp.max(s, axis=-1, keepdims=True))
    return pr / jnp.sum(pr, axis=-1, keepdims=True)


def _mem_attn_shared_kernel(q_ref, k_ref, v_ref, o_ref):
    hd = q_ref.shape[-1] // X_HEADS
    heads = [slice(hh * hd, (hh + 1) * hd) for hh in range(X_HEADS)]
    s = [_dot_nt(q_ref[:, sl], k_ref[:, sl]) for sl in heads]
    pr = [_softmax_rows(sh) for sh in s]
    o = [_dot(pr[hh], v_ref[:, heads[hh]]) for hh in range(X_HEADS)]
    for hh in range(X_HEADS):
        o_ref[:, heads[hh]] = o[hh].astype(o_ref.dtype)


def _mem_attn_shared(q, mk, mv):
    m, d = q.shape
    tm = min(ROW_TILE, m)
    return pl.pallas_call(
        _mem_attn_shared_kernel, grid=(m // tm,),
        in_specs=[_rows(tm, d), _full(mk.shape), _full(mv.shape)],
        out_specs=_rows(tm, d), out_shape=jax.ShapeDtypeStruct((m, d), BF16),
        compiler_params=_params("parallel"), name="mem_attn_shared",
    )(q, mk, mv)


def _mem_attn_rows_kernel(q_ref, k_ref, v_ref, o_ref):
    nb, _, d = q_ref.shape
    hd = d // X_HEADS
    hrow = lax.broadcasted_iota(jnp.int32, (SUBLANES, d), 0)
    hcol = lax.broadcasted_iota(jnp.int32, (SUBLANES, d), 1) // hd
    own = hrow == hcol
    qh = [jnp.where(own, jnp.broadcast_to(q_ref[i].astype(F32), (SUBLANES, d)), 0.0) for i in range(nb)]
    s = [_dot_nt(qh[i], k_ref[i]) for i in range(nb)]
    pr = [_softmax_rows(s[i]) for i in range(nb)]
    res = [_dot(pr[i], v_ref[i]) for i in range(nb)]
    for i in range(nb):
        o_ref[i] = jnp.sum(jnp.where(own, res[i], 0.0), axis=0, keepdims=True).astype(o_ref.dtype)


def _mem_attn_rows(q, mk, mv):
    n, d = q.shape
    nb = 4
    mem = mk.shape[1]
    qspec = pl.BlockSpec((nb, 1, d), lambda i: (i, 0, 0))
    kspec = pl.BlockSpec((nb, mem, d), lambda i: (i, 0, 0))
    out = pl.pallas_call(
        _mem_attn_rows_kernel, grid=(n // nb,),
        in_specs=[qspec, kspec, kspec], out_specs=qspec,
        out_shape=jax.ShapeDtypeStruct((n, 1, d), BF16),
        compiler_params=_params("parallel"), name="mem_attn_rows",
    )(q[:, None, :], mk, mv)
    return out[:, 0, :]


def _tail_kernel(x1_ref, om_ref, wmo_ref, gpm_ref, gmlp_ref, wup_ref, wdn_ref, gpost_ref, y_ref):
    x2 = x1_ref[...] + _rms(jnp.dot(om_ref[...], wmo_ref[...], preferred_element_type=F32), gpm_ref[...])
    h = _rms(x2, gmlp_ref[...]).astype(BF16)
    u = jnp.maximum(jnp.dot(h, wup_ref[...], preferred_element_type=F32), 0.0)
    ff = jnp.dot((u * u).astype(BF16), wdn_ref[...], preferred_element_type=F32)
    y_ref[...] = x2 + _rms(ff, gpost_ref[...])


def _tail(x1, om, *weights):
    m, d = x1.shape
    tm = min(ROW_TILE, m)
    return pl.pallas_call(
        _tail_kernel, grid=(m // tm,),
        in_specs=[_rows(tm, d), _rows(tm, d)] + [_full(w.shape) for w in weights],
        out_specs=_rows(tm, d), out_shape=jax.ShapeDtypeStruct((m, d), F32),
        compiler_params=_params("parallel"), name="tail",
    )(x1, om, *weights)


def _rot_cols(w):
    half = w.shape[-1] // 2
    return jnp.concatenate([-w[..., half:], w[..., :half]], axis=-1)


def _rope_tables(pos):
    inv = ROPE_THETA ** (-jnp.arange(0, A_ROPE, 2, dtype=F32) / A_ROPE)
    ang = pos.astype(F32)[:, None] * inv[None, :]
    z_lo = jnp.zeros((pos.shape[0], A_NOPE), F32)
    z_hi = jnp.zeros((pos.shape[0], QK_PAD - A_NOPE - A_ROPE), F32)
    cos, sin = jnp.cos(ang), jnp.sin(ang)
    return (jnp.concatenate([z_lo, cos, cos, z_hi], axis=1),
            jnp.concatenate([z_lo, sin, sin, z_hi], axis=1))


def _prep_layer(l, w_in, w_uq, w_uk, w_uv, rw_decay_up, rw_a_up, rw_g_up):
    d = w_in.shape[1]
    wi = w_in[l]
    o_rw = Q_RANK + KV_RANK + A_ROPE
    o_ga = o_rw + RW_COLS
    w_kpe = wi[:, Q_RANK + KV_RANK:o_rw]
    z = lambda n: jnp.zeros((d, n), F32)
    pad_hi = QK_PAD - A_NOPE - A_ROPE
    wlat = jnp.concatenate([wi[:, :Q_RANK + KV_RANK], z(A_NOPE), w_kpe, z(pad_hi),
                            z(A_NOPE), _rot_cols(w_kpe), z(pad_hi)], axis=1).astype(BF16)
    wrw = jnp.pad(wi[:, o_rw:o_ga], ((0, 0), (0, RW_PAD - RW_COLS))).astype(BF16)
    wga = wi[:, o_ga:o_ga + d].astype(BF16)
    wgb = wi[:, o_ga + d:].astype(BF16)
    uq = w_uq[l].reshape(Q_RANK, A_HEADS, A_NOPE + A_ROPE)
    nope, pe = uq[..., :A_NOPE], uq[..., A_NOPE:]
    zq = lambda n: jnp.zeros((Q_RANK, A_HEADS, n), F32)
    wqa = jnp.concatenate([nope, pe, zq(pad_hi)], axis=-1).reshape(Q_RANK, -1).astype(BF16)
    wqb = jnp.concatenate([zq(A_NOPE), _rot_cols(pe), zq(pad_hi)], axis=-1).reshape(Q_RANK, -1).astype(BF16)
    pad_cols = lambda w: jnp.pad(w, ((0, 0), (0, 0), (0, QK_PAD - w.shape[-1]))).reshape(KV_RANK, -1).astype(BF16)
    wk_cols = pad_cols(w_uk[l])
    wv_cols = pad_cols(w_uv[l])
    wk_rows = jnp.pad(jnp.transpose(w_uk[l], (1, 2, 0)), ((0, 0), (0, QK_PAD - A_NOPE), (0, 0))).astype(BF16)
    wv = w_uv[l].reshape(KV_RANK, -1).astype(BF16)
    wl = jnp.zeros((LORA_IN, 3 * B_WIDTH), F32)
    wl = wl.at[:DECAY_LORA, :B_WIDTH].set(rw_decay_up[l])
    wl = wl.at[DECAY_LORA:DECAY_LORA + AAA_LORA, B_WIDTH:2 * B_WIDTH].set(rw_a_up[l])
    wl = wl.at[DECAY_LORA + AAA_LORA:, 2 * B_WIDTH:].set(rw_g_up[l])
    return wlat, wrw, wga, wgb, wqa, wqb, wk_cols, wv_cols, wk_rows, wv, wl.astype(BF16)


def kernel(x_prompt, x_sample, mem_prompt, cache_ckv, cache_kpe, state_wkv, state_shift, cache_mem_k, cache_mem_v, page_table, norm_pre_mix, w_in, q_norm, w_uq, kv_norm, w_uk, w_uv, rw_mu, rw_w0, rw_decay_up, rw_a0, rw_a_up, rw_g_up, rw_k_k, rw_k_a, rw_r_k, rw_ln_w, rw_ln_b, w_proj_a, w_proj_b, w_out, norm_post_mix, norm_pre_mem, mem_norm, w_mq, w_mk, w_mv, w_mo, norm_post_mem, norm_pre_mlp, w_ff_up, w_ff_down, norm_post_mlp):
    depth = w_in.shape[0]
    n_p, seq, d = x_prompt.shape
    n_s, dec_seq, _ = x_sample.shape
    assert n_p == 1 and dec_seq == 1, "one prompt sequence and one new token per decode request"
    past_len = page_table.shape[1] * cache_ckv.shape[2]
    ck_p, sk_p = _rope_tables(jnp.arange(seq))
    ck_s, sk_s = _rope_tables(jnp.full((n_s,), past_len))
    seg = jnp.arange(B_WIDTH) // B_HDIM
    e = (seg[:, None] == seg[None, :]).astype(BF16)
    cache_kpe_t = jnp.swapaxes(cache_kpe, 2, 3)
    row = lambda p, l: p[l].reshape(1, -1)
    bf = lambda p, l: p[l].astype(BF16)

    y_p = x_prompt.reshape(seq, d)
    y_s = x_sample.reshape(n_s, d)
    outs = [[] for _ in range(10)]
    for l in range(depth):
        wlat, wrw, wga, wgb, wqa, wqb, wk_cols, wv_cols, wk_rows, wv, wl = _prep_layer(
            l, w_in, w_uq, w_uk, w_uv, rw_decay_up, rw_a_up, rw_g_up)
        proj_w = (row(norm_pre_mix, l), wlat, wrw, row(q_norm, l), row(kv_norm, l), wqa, wqb)
        prep_w = (row(rw_mu, l), row(rw_w0, l), row(rw_a0, l), wl, row(rw_k_k, l), row(rw_k_a, l), e)
        mix_w = (row(norm_pre_mix, l), wga, wgb, row(rw_r_k, l), row(rw_ln_w, l), row(rw_ln_b, l), e,
                 bf(w_proj_a, l), bf(w_proj_b, l), bf(w_out, l), row(norm_post_mix, l),
                 row(norm_pre_mem, l), bf(w_mq, l))
        tail_w = (bf(w_mo, l), row(norm_post_mem, l), row(norm_pre_mlp, l), bf(w_ff_up, l),
                  bf(w_ff_down, l), row(norm_post_mlp, l))

        q, ckv_p, kpe_p, rw_p, k, v = _proj_in(y_p, *proj_w, ck_p, sk_p, wk_cols, wv_cols, prompt=True)
        oa_p = _mla_prompt(q, k, v)
        r_, lw_, k2_, v_, na_, b_, g_ = _rwkv_prep(rw_p, jnp.zeros((1, RW_COLS), F32), *prep_w, seq=True)
        yb_p, wkv_p = _rwkv_chunked(r_, lw_, k2_, v_, na_, b_)
        x1_p, qm_p = _mix_out(y_p, oa_p, yb_p, r_, k2_, v_, g_, *mix_w)
        mk_p, mv_p = _mem_kv(mem_prompt.reshape(-1, d), row(mem_norm, l), bf(w_mk, l), bf(w_mv, l))
        om_p = _mem_attn_shared(qm_p, mk_p, mv_p)
        y_p = _tail(x1_p, om_p, *tail_w)

        q, ckv_s, kpe_s, rw_s, qabs = _proj_in(y_s, *proj_w, ck_s, sk_s, wk_rows, wv, prompt=False)
        oa_s = _mla_decode(page_table, jnp.transpose(qabs, (1, 0, 2)), jnp.transpose(q, (1, 0, 2)),
                           ckv_s, kpe_s, wv, cache_ckv, cache_kpe_t, l)
        r_, lw_, k2_, v_, na_, b_, g_ = _rwkv_prep(rw_s, state_shift[l], *prep_w, seq=False)
        yb_s, wkv_s = _rwkv_step(state_wkv[l], r_, lw_, k2_, v_, na_, b_)
        x1_s, qm_s = _mix_out(y_s, oa_s, yb_s, r_, k2_, v_, g_, *mix_w)
        mem_shape = (n_s, cache_mem_k.shape[2], d)
        om_s = _mem_attn_rows(qm_s, cache_mem_k[l].reshape(mem_shape), cache_mem_v[l].reshape(mem_shape))
        y_s = _tail(x1_s, om_s, *tail_w)

        mem_heads = (n_p, -1, X_HEADS, d // X_HEADS)
        layer_outs = (ckv_p.reshape(n_p, seq, KV_RANK), kpe_p.reshape(n_p, seq, A_ROPE),
                      wkv_p[None], rw_p[-1:], mk_p.reshape(mem_heads), mv_p.reshape(mem_heads),
                      ckv_s.reshape(n_s, 1, KV_RANK), kpe_s.reshape(n_s, 1, A_ROPE), wkv_s, rw_s)
        for lst, val in zip(outs, layer_outs):
            lst.append(val)
    return (y_p.reshape(n_p, seq, d), y_s.reshape(n_s, 1, d)) + tuple(jnp.stack(o) for o in outs)
```

```python
import functools
import math

import jax
import jax.numpy as jnp
from jax import lax
from jax.experimental import pallas as pl
from jax.experimental.pallas import tpu as pltpu

F32 = jnp.float32
BF16 = jnp.bfloat16

A_HEADS = 8
A_NOPE = 64
A_ROPE = 32
A_VDIM = 64
Q_RANK = 384
KV_RANK = 256
ROPE_THETA = 10000.0
B_HEADS = 8
B_HDIM = 64
B_WIDTH = B_HEADS * B_HDIM
DECAY_LORA = 64
AAA_LORA = 64
GATE_LORA = 160
LORA_IN = DECAY_LORA + AAA_LORA + GATE_LORA
GN_EPS = 64e-5
X_HEADS = 4
NORM_EPS = 1e-6
RW_COLS = 3 * B_WIDTH + LORA_IN

LANES = 128
SUBLANES = 8
QK_PAD = 128
RW_PAD = 15 * LANES
RWKV_CHUNK = 64
RWKV_BLOCK = 128
ATTN_Q_BLOCK = 512
ATTN_KV_BLOCK = 1024
DECODE_PAGES = 16
ROW_TILE = 256
VMEM_LIMIT = 48 * 1024 * 1024
QK_SCALE = (A_NOPE + A_ROPE) ** -0.5 * math.log2(math.e)


def _params(*sem):
    return pltpu.CompilerParams(dimension_semantics=sem, vmem_limit_bytes=VMEM_LIMIT)


def _full(shape):
    zeros = (0,) * len(shape)
    return pl.BlockSpec(shape, lambda *_: zeros)


def _rows(tm, width):
    return pl.BlockSpec((tm, width), lambda i: (i, 0))


def _rms(x, g):
    return x * lax.rsqrt(jnp.mean(x * x, axis=-1, keepdims=True) + NORM_EPS) * g


def _dot(a, b):
    return jnp.dot(a.astype(BF16), b.astype(BF16), preferred_element_type=F32)


def _dot_nt(a, b):
    return lax.dot_general(a.astype(BF16), b.astype(BF16), (((1,), (1,)), ((), ())),
                           preferred_element_type=F32)


def _dot_tn(a, b):
    return lax.dot_general(a.astype(BF16), b.astype(BF16), (((0,), (0,)), ((), ())),
                           preferred_element_type=F32)


def _sigmoid(x):
    return 1.0 / (1.0 + jnp.exp(-x))


def _seg_sum(x, e):
    hi = x.astype(BF16)
    lo = (x - hi.astype(F32)).astype(BF16)
    return (jnp.dot(hi, e, preferred_element_type=F32)
            + jnp.dot(lo, e, preferred_element_type=F32))


def _proj_in_kernel(prompt, x_ref, g_ref, wlat_ref, wrw_ref, qn_ref, kvn_ref, wqa_ref, wqb_ref,
                    ck_ref, sk_ref, wk_ref, wv_ref, *outs):
    if prompt:
        q_ref, ckv_ref, kpe_ref, rw_ref, k_ref, v_ref = outs
    else:
        q_ref, ckv_ref, kpe_ref, rw_ref, qabs_ref = outs
    h = _rms(x_ref[...], g_ref[...]).astype(BF16)
    lat = jnp.dot(h, wlat_ref[...], preferred_element_type=F32)
    cq = lat[:, :Q_RANK]
    ckv = _rms(lat[:, Q_RANK:Q_RANK + KV_RANK], kvn_ref[...])
    ck = ck_ref[...]
    sk = sk_ref[...]
    kpe = lat[:, 640:768] * ck + lat[:, 768:896] * sk
    ckv_ref[...] = ckv
    kpe_ref[...] = kpe[:, A_NOPE:A_NOPE + A_ROPE]
    rw = jnp.dot(h, wrw_ref[...], preferred_element_type=F32)
    rw_ref[...] = rw[:, :RW_COLS]
    qn = _rms(cq, qn_ref[...]).astype(BF16)
    qa = jnp.dot(qn, wqa_ref[...], preferred_element_type=F32)
    qb = jnp.dot(qn, wqb_ref[...], preferred_element_type=F32)
    lane = lax.broadcasted_iota(jnp.int32, ck.shape, 1)
    cq_t = QK_SCALE * (ck + jnp.where(lane < A_NOPE, 1.0, 0.0))
    sq_t = QK_SCALE * sk
    ckv_b = ckv.astype(BF16)
    if prompt:
        kn = jnp.dot(ckv_b, wk_ref[...], preferred_element_type=F32)
        vn = jnp.dot(ckv_b, wv_ref[...], preferred_element_type=F32)
        ones_col = jnp.where(lane == A_VDIM, 1.0, 0.0)
    for hh in range(A_HEADS):
        sl = slice(hh * QK_PAD, (hh + 1) * QK_PAD)
        qh = (qa[:, sl] * cq_t + qb[:, sl] * sq_t).astype(BF16)
        q_ref[hh] = qh
        if prompt:
            k_ref[hh] = (kn[:, sl] + kpe).astype(BF16)
            v_ref[hh] = (vn[:, sl] + ones_col).astype(BF16)
        else:
            qabs_ref[hh] = jnp.dot(qh, wk_ref[hh], preferred_element_type=F32)


def _proj_in(x, g, wlat, wrw, qn, kvn, wqa, wqb, ck, sk, wk, wv, *, prompt):
    m, d = x.shape
    tm = min(ROW_TILE, m)
    heads_out = lambda w, dt: jax.ShapeDtypeStruct((A_HEADS, m, w), dt)
    heads_spec = lambda w: pl.BlockSpec((A_HEADS, tm, w), lambda i: (0, i, 0))
    out_shape = [heads_out(QK_PAD, BF16), jax.ShapeDtypeStruct((m, KV_RANK), F32),
                 jax.ShapeDtypeStruct((m, A_ROPE), F32), jax.ShapeDtypeStruct((m, RW_COLS), F32)]
    out_specs = [heads_spec(QK_PAD), _rows(tm, KV_RANK), _rows(tm, A_ROPE), _rows(tm, RW_COLS)]
    if prompt:
        out_shape += [heads_out(QK_PAD, BF16), heads_out(QK_PAD, BF16)]
        out_specs += [heads_spec(QK_PAD), heads_spec(QK_PAD)]
    else:
        out_shape += [heads_out(KV_RANK, F32)]
        out_specs += [heads_spec(KV_RANK)]
    return pl.pallas_call(
        functools.partial(_proj_in_kernel, prompt),
        grid=(m // tm,),
        in_specs=[_rows(tm, d), _full(g.shape), _full(wlat.shape), _full(wrw.shape), _full(qn.shape),
                  _full(kvn.shape), _full(wqa.shape), _full(wqb.shape), _rows(tm, LANES),
                  _rows(tm, LANES), _full(wk.shape), _full(wv.shape)],
        out_specs=out_specs, out_shape=out_shape,
        compiler_params=_params("parallel"), name="proj_in",
    )(x, g, wlat, wrw, qn, kvn, wqa, wqb, ck, sk, wk, wv)


def _rwkv_prep_kernel(seq, rw_ref, prev_ref, first_ref, mu_ref, w0_ref, a0_ref, wl_ref, kk_ref, ka_ref,
                      e_ref, r_ref, lw_ref, k2_ref, v_ref, na_ref, b_ref, g_ref):
    rw = rw_ref[...]
    if seq:
        above = jnp.where(pl.program_id(0) == 0, first_ref[...], prev_ref[SUBLANES - 1:SUBLANES, :])
        ridx = lax.broadcasted_iota(jnp.int32, rw.shape, 0)
        shifted = jnp.where(ridx == 0, above, pltpu.roll(rw, 1, axis=0))
    else:
        shifted = first_ref[...]
    xs = rw + (shifted - rw) * mu_ref[...]
    r = xs[:, :B_WIDTH]
    k = xs[:, B_WIDTH:2 * B_WIDTH]
    v = xs[:, 2 * B_WIDTH:3 * B_WIDTH]
    lo = xs[:, 3 * B_WIDTH:]
    lane = lax.broadcasted_iota(jnp.int32, lo.shape, 1)
    t = jnp.where(lane < DECAY_LORA, jnp.tanh(lo),
                  jnp.where(lane < DECAY_LORA + AAA_LORA, lo, _sigmoid(lo)))
    l3 = jnp.dot(t.astype(BF16), wl_ref[...], preferred_element_type=F32)
    z = -(w0_ref[...] + l3[:, :B_WIDTH])
    softplus = jnp.maximum(z, 0.0) + jnp.log(1.0 + jnp.exp(-jnp.abs(z)))
    w = -softplus - 0.5
    a = _sigmoid(a0_ref[...] + l3[:, B_WIDTH:2 * B_WIDTH])
    kkr = k * kk_ref[...]
    norm = jnp.sqrt(_seg_sum(kkr * kkr, e_ref[...]))
    kk = kkr / jnp.maximum(norm, 1e-12)
    r_ref[...] = r
    lw_ref[...] = -jnp.exp(w)
    k2_ref[...] = k * (1.0 + (a - 1.0) * ka_ref[...])
    v_ref[...] = v
    na_ref[...] = -kk
    b_ref[...] = kk * a
    g_ref[...] = l3[:, 2 * B_WIDTH:]


def _rwkv_prep(rw, first, mu, w0, a0, wl, k_k, k_a, e, *, seq):
    m = rw.shape[0]
    tm = min(ROW_TILE, m)
    o = jax.ShapeDtypeStruct((m, B_WIDTH), F32)
    per_tile = tm // SUBLANES
    prev_spec = pl.BlockSpec((SUBLANES, RW_COLS), lambda i: (jnp.maximum(i * per_tile - 1, 0), 0))
    first_spec = _full(first.shape) if seq else _rows(tm, RW_COLS)
    return pl.pallas_call(
        functools.partial(_rwkv_prep_kernel, seq), grid=(m // tm,),
        in_specs=[_rows(tm, RW_COLS), prev_spec, first_spec, _full(mu.shape), _full(w0.shape),
                  _full(a0.shape), _full(wl.shape), _full(k_k.shape), _full(k_a.shape), _full(e.shape)],
        out_specs=[_rows(tm, B_WIDTH)] * 7, out_shape=[o] * 7,
        compiler_params=_params("parallel"), name="rwkv_prep",
    )(rw, rw, first, mu, w0, a0, wl, k_k, k_a, e)


def _rwkv_chunk_kernel(r_ref, lw_ref, k2_ref, v_ref, na_ref, b_ref, y_ref, sout_ref, s_ref):
    C = RWKV_CHUNK
    n_chunks = r_ref.shape[0] // C
    pairs = B_HEADS // 2

    @pl.when(pl.program_id(0) == 0)
    def _():
        s_ref[...] = jnp.zeros_like(s_ref)

    row = lax.broadcasted_iota(jnp.int32, (2 * C, 2 * C), 0)
    col = lax.broadcasted_iota(jnp.int32, (2 * C, 2 * C), 1)
    tok_r = jnp.bitwise_and(row, C - 1)
    tok_c = jnp.bitwise_and(col, C - 1)
    strict = tok_c < tok_r
    incl = tok_c <= tok_r
    eye = jnp.where(row == col, 1.0, 0.0)
    crow = lax.broadcasted_iota(jnp.int32, (C, C), 0)
    ccol = lax.broadcasted_iota(jnp.int32, (C, C), 1)
    cum = jnp.where(ccol <= crow, 1.0, 0.0).astype(BF16)
    head0 = lax.broadcasted_iota(jnp.int32, (C, LANES), 1) < B_HDIM

    def stack(x):
        return jnp.concatenate([jnp.where(head0, x, 0.0), jnp.where(head0, 0.0, x)], axis=0)

    units = [(ci, p) for ci in range(n_chunks) for p in range(pairs)]

    G_all = []
    for ci in range(n_chunks):
        lw = lw_ref[ci * C:(ci + 1) * C, :]
        l_hi = lw.astype(BF16)
        rem = lw - l_hi.astype(F32)
        l_mid = rem.astype(BF16)
        l_lo = (rem - l_mid.astype(F32)).astype(BF16)
        g3 = jnp.dot(cum, jnp.concatenate([l_hi, l_mid, l_lo], axis=1), preferred_element_type=F32)
        G_all.append(g3[:, :B_WIDTH] + g3[:, B_WIDTH:2 * B_WIDTH] + g3[:, 2 * B_WIDTH:])

    ops = {}
    for ci, p in units:
        rows = slice(ci * C, (ci + 1) * C)
        sl = slice(p * LANES, (p + 1) * LANES)
        lw = lw_ref[rows, sl]
        G = G_all[ci][:, sl]
        GC = G[C - 1:C, :]
        e_pos = jnp.exp(G)
        e_neg = jnp.exp(-G)
        e_prev = jnp.exp(G - lw)
        e_tail = jnp.exp(GC - G)
        r, k2, v = r_ref[rows, sl], k2_ref[rows, sl], v_ref[rows, sl]
        na, b = na_ref[rows, sl], b_ref[rows, sl]
        ops[ci, p] = dict(
            AR=jnp.concatenate([stack(na * e_prev), stack(r * e_pos)], axis=0).astype(BF16),
            BK=jnp.concatenate([stack(b * e_neg), stack(k2 * e_neg)], axis=0).astype(BF16),
            Bh=stack(b * e_tail).astype(BF16), Kh=stack(k2 * e_tail).astype(BF16),
            Vs=stack(v).astype(BF16), decay=jnp.exp(GC))

    AA = {u: _dot_nt(ops[u]["AR"], ops[u]["BK"]) for u in units}
    X, Pw, AakArk, Arb = {}, {}, {}, {}
    for u in units:
        aa = AA[u]
        a_ab = jnp.where(strict, aa[:2 * C, :2 * C], 0.0)
        AakArk[u] = jnp.concatenate([jnp.where(strict, aa[:2 * C, 2 * C:], 0.0),
                                     jnp.where(incl, aa[2 * C:, 2 * C:], 0.0)], axis=0).astype(BF16)
        Arb[u] = jnp.where(incl, aa[2 * C:, :2 * C], 0.0).astype(BF16)
        X[u] = eye + a_ab
        Pw[u] = a_ab
    sq = {u: _dot(Pw[u], Pw[u]) for u in units}
    n = 2
    while 2 * n < C:
        both = {u: _dot(jnp.concatenate([X[u], sq[u]], axis=0), sq[u]) for u in units}
        for u in units:
            X[u] = X[u] + both[u][:2 * C]
            sq[u] = both[u][2 * C:]
        n *= 2
    last = {u: _dot(X[u], sq[u]) for u in units}
    for u in units:
        X[u] = (X[u] + last[u]).astype(BF16)
    AV = {u: _dot(AakArk[u], ops[u]["Vs"]) for u in units}
    VK = {u: _dot_tn(ops[u]["Vs"], ops[u]["Kh"]) for u in units}

    S = [s_ref[p] for p in range(pairs)]
    for ci in range(n_chunks):
        SS = [_dot_nt(ops[ci, p]["AR"], S[p]) for p in range(pairs)]
        U = [_dot(X[ci, p], SS[p][:2 * C] + AV[ci, p][:2 * C]) for p in range(pairs)]
        YU = [_dot(Arb[ci, p], U[p]) for p in range(pairs)]
        UB = [_dot_tn(U[p], ops[ci, p]["Bh"]) for p in range(pairs)]
        for p in range(pairs):
            Y = SS[p][2 * C:] + YU[p] + AV[ci, p][2 * C:]
            y_ref[ci * C:(ci + 1) * C, p * LANES:(p + 1) * LANES] = Y[:C] + Y[C:]
            S[p] = S[p] * ops[ci, p]["decay"] + UB[p] + VK[ci, p]
    for p in range(pairs):
        s_ref[p] = S[p]

    @pl.when(pl.program_id(0) == pl.num_programs(0) - 1)
    def _():
        sout_ref[...] = s_ref[...]


def _rwkv_chunked(r, lw, k2, v, na, b):
    t = r.shape[0]
    spec = _rows(RWKV_BLOCK, B_WIDTH)
    pairs = B_HEADS // 2
    y, s = pl.pallas_call(
        _rwkv_chunk_kernel, grid=(t // RWKV_BLOCK,),
        in_specs=[spec] * 6,
        out_specs=[spec, _full((pairs, LANES, LANES))],
        out_shape=[jax.ShapeDtypeStruct((t, B_WIDTH), F32),
                   jax.ShapeDtypeStruct((pairs, LANES, LANES), F32)],
        scratch_shapes=[pltpu.VMEM((pairs, LANES, LANES), F32)],
        compiler_params=_params("arbitrary"), name="rwkv_chunked",
    )(r, lw, k2, v, na, b)
    s = s.reshape(pairs, 2, B_HDIM, 2, B_HDIM)
    wkv = jnp.stack([s[:, 0, :, 0, :], s[:, 1, :, 1, :]], axis=1).reshape(B_HEADS, B_HDIM, B_HDIM)
    return y, wkv


def _rwkv_step_kernel(s_ref, r_ref, lw_ref, k2_ref, na_ref, b_ref, v_ref, sout_ref, y_ref):
    S = s_ref[...]
    sa = jnp.sum(S * na_ref[...], axis=1, keepdims=True)
    S = S * jnp.exp(lw_ref[...]) + sa * b_ref[...] + v_ref[...] * k2_ref[...]
    sout_ref[...] = S
    y_ref[...] = jnp.sum(S * r_ref[...], axis=1, keepdims=True)


def _rwkv_step(state, r, lw, k2, v, na, b):
    n = state.shape[0]
    keyed = lambda x: x.T.reshape(B_HEADS, 1, B_HDIM, n)
    kspec = pl.BlockSpec((None, 1, B_HDIM, n), lambda h: (h, 0, 0, 0))
    vspec = pl.BlockSpec((None, B_HDIM, 1, n), lambda h: (h, 0, 0, 0))
    sspec = pl.BlockSpec((None, B_HDIM, B_HDIM, n), lambda h: (h, 0, 0, 0))
    s_new, y = pl.pallas_call(
        _rwkv_step_kernel, grid=(B_HEADS,),
        in_specs=[sspec, kspec, kspec, kspec, kspec, kspec, vspec],
        out_specs=[sspec, vspec],
        out_shape=[jax.ShapeDtypeStruct((B_HEADS, B_HDIM, B_HDIM, n), F32),
                   jax.ShapeDtypeStruct((B_HEADS, B_HDIM, 1, n), F32)],
        compiler_params=_params("parallel"), name="rwkv_step",
    )(jnp.transpose(state, (1, 2, 3, 0)), keyed(r), keyed(lw), keyed(k2), keyed(na), keyed(b),
      v.T.reshape(B_HEADS, B_HDIM, 1, n))
    return y.reshape(B_WIDTH, n).T, jnp.transpose(s_new, (3, 0, 1, 2))


def _mla_prompt_kernel(q_ref, k_ref, v_ref, o_ref):
    qi = pl.program_id(1)
    TQ, TK = ATTN_Q_BLOCK, ATTN_KV_BLOCK
    q = (q_ref[0], q_ref[1])
    row = lax.broadcasted_iota(jnp.int32, (TQ, TK), 0) + qi * TQ
    col = lax.broadcasted_iota(jnp.int32, (TQ, TK), 1)

    def block(ks, carry, masked):
        start = pl.multiple_of(ks * TK, TK)
        s = [_dot_nt(q[hh], k_ref[hh, pl.ds(start, TK), :]) for hh in range(2)]
        pr, alpha, m_out = [], [], []
        for hh in range(2):
            m = carry[hh][0]
            sh = jnp.where(col + start <= row, s[hh], -jnp.inf) if masked else s[hh]
            m_new = jnp.maximum(m, jnp.max(sh, axis=-1, keepdims=True))
            alpha.append(jnp.exp2(m - m_new))
            pr.append(jnp.exp2(sh - m_new).astype(BF16))
            m_out.append(m_new)
        pv = [jnp.dot(pr[hh], v_ref[hh, pl.ds(start, TK), :], preferred_element_type=F32)
              for hh in range(2)]
        return tuple((m_out[hh], carry[hh][1] * alpha[hh] + pv[hh]) for hh in range(2))

    init = tuple((jnp.full((TQ, 1), -jnp.inf, F32), jnp.zeros((TQ, LANES), F32)) for _ in range(2))
    n_full = (qi * TQ) // TK
    carry = lax.fori_loop(0, n_full, lambda ks, cr: block(ks, cr, False), init)
    (_, a0), (_, a1) = block(n_full, carry, True)
    o0 = a0 / a0[:, A_VDIM:A_VDIM + 1]
    o1 = a1 / a1[:, A_VDIM:A_VDIM + 1]
    head0 = lax.broadcasted_iota(jnp.int32, (TQ, LANES), 1) < A_VDIM
    o_ref[...] = jnp.where(head0, o0, pltpu.roll(o1, A_VDIM, axis=1)).astype(o_ref.dtype)


def _mla_prompt(q, k, v):
    t = q.shape[1]
    TQ = ATTN_Q_BLOCK
    resident = pl.BlockSpec((2, t, QK_PAD), lambda p, i: (p, 0, 0), pipeline_mode=pl.Buffered(1))
    return pl.pallas_call(
        _mla_prompt_kernel, grid=(A_HEADS // 2, t // TQ),
        in_specs=[pl.BlockSpec((2, TQ, QK_PAD), lambda p, i: (p, i, 0)), resident, resident],
        out_specs=pl.BlockSpec((TQ, LANES), lambda p, i: (i, p)),
        out_shape=jax.ShapeDtypeStruct((t, A_HEADS * A_VDIM), BF16),
        compiler_params=_params("parallel", "arbitrary"), name="mla_prompt",
    )(q, k, v)


def _mla_decode_kernel(layer, pt_ref, qabs_ref, q_ref, cnew_ref, knew_ref, wuv_ref, ckv_hbm, kpe_hbm,
                       o_ref, cbuf, kbuf, kcat, pcat, csem, ksem):
    P = DECODE_PAGES
    n = pl.program_id(0)
    n_steps = pt_ref.shape[1] // P
    page = cbuf.shape[2]

    def page_copies(pid, slot, i):
        return (pltpu.make_async_copy(ckv_hbm.at[layer, pid], cbuf.at[slot, i], csem.at[slot]),
                pltpu.make_async_copy(kpe_hbm.at[layer, pid], kbuf.at[slot, i], ksem.at[slot]))

    def start(req, step, slot):
        for i in range(P):
            for cp in page_copies(pt_ref[req, step * P + i], slot, i):
                cp.start()

    def wait(slot):
        for i in range(P):
            for cp in page_copies(0, slot, i):
                cp.wait()

    @pl.when(n == 0)
    def _():
        start(0, 0, 0)

    qa = qabs_ref[...]
    qpe = q_ref[:, A_NOPE:A_NOPE + A_ROPE]
    m = jnp.full((A_HEADS, 1), -jnp.inf, F32)
    l = jnp.zeros((A_HEADS, 1), F32)
    acc = jnp.zeros((A_HEADS, KV_RANK), F32)
    pending = None
    for step in range(n_steps):
        slot = step % 2
        if step + 1 < n_steps:
            start(n, step + 1, 1 - slot)
        else:
            @pl.when(n + 1 < pl.num_programs(0))
            def _():
                start(n + 1, 0, 1 - slot)
        wait(slot)
        for i in range(P):
            kcat[slot, i * page:(i + 1) * page, :] = cbuf[slot, i].astype(BF16)
            pcat[slot, :, i * page:(i + 1) * page] = kbuf[slot, i].astype(BF16)
        s = _dot_nt(qa, kcat[slot]) + jnp.dot(qpe, pcat[slot], preferred_element_type=F32)
        if pending is not None:
            pr_prev, alpha_prev, slot_prev = pending
            acc = acc * alpha_prev + jnp.dot(pr_prev, kcat[slot_prev], preferred_element_type=F32)
        m_new = jnp.maximum(m, jnp.max(s, axis=-1, keepdims=True))
        alpha = jnp.exp2(m - m_new)
        pr = jnp.exp2(s - m_new)
        l = l * alpha + jnp.sum(pr, axis=-1, keepdims=True)
        m = m_new
        pending = (pr.astype(BF16), alpha, slot)
    pr_prev, alpha_prev, slot_prev = pending
    acc = acc * alpha_prev + jnp.dot(pr_prev, kcat[slot_prev], preferred_element_type=F32)

    cnew = cnew_ref[...]
    s_self = (jnp.sum(qa * cnew, axis=-1, keepdims=True)
              + jnp.sum(qpe.astype(F32) * knew_ref[...], axis=-1, keepdims=True))
    m_fin = jnp.maximum(m, s_self)
    al = jnp.exp2(m - m_fin)
    p_self = jnp.exp2(s_self - m_fin)
    o_lat = (acc * al + p_self * cnew) / (l * al + p_self)
    res = _dot(o_lat, wuv_ref[...])
    hrow = lax.broadcasted_iota(jnp.int32, res.shape, 0)
    hcol = lax.broadcasted_iota(jnp.int32, res.shape, 1) // A_VDIM
    o_ref[...] = jnp.sum(jnp.where(hrow == hcol, res, 0.0), axis=0, keepdims=True).astype(o_ref.dtype)


def _mla_decode(page_table, qabs, q, ckv_new, kpe_new, wuv, cache_ckv, cache_kpe_t, layer):
    n, n_pages = page_table.shape
    P = DECODE_PAGES
    page = cache_ckv.shape[2]
    assert (n_pages // P) % 2 == 0, "buffer slots alternate per step and must line up across requests"
    req = lambda shape: pl.BlockSpec((None,) + shape, lambda b, pt: (b, 0, 0))
    hbm = pl.BlockSpec(memory_space=pl.ANY)
    grid_spec = pltpu.PrefetchScalarGridSpec(
        num_scalar_prefetch=1, grid=(n,),
        in_specs=[req((A_HEADS, KV_RANK)), req((A_HEADS, QK_PAD)), req((1, KV_RANK)), req((1, A_ROPE)),
                  pl.BlockSpec(wuv.shape, lambda b, pt: (0, 0)), hbm, hbm],
        out_specs=req((1, A_HEADS * A_VDIM)),
        scratch_shapes=[pltpu.VMEM((2, P, page, KV_RANK), F32), pltpu.VMEM((2, P, A_ROPE, page), F32),
                        pltpu.VMEM((2, P * page, KV_RANK), BF16), pltpu.VMEM((2, A_ROPE, P * page), BF16),
                        pltpu.SemaphoreType.DMA((2,)), pltpu.SemaphoreType.DMA((2,))])
    out = pl.pallas_call(
        functools.partial(_mla_decode_kernel, layer), grid_spec=grid_spec,
        out_shape=jax.ShapeDtypeStruct((n, 1, A_HEADS * A_VDIM), BF16),
        compiler_params=_params("arbitrary"), name="mla_decode",
    )(page_table, qabs, q, ckv_new[:, None, :], kpe_new[:, None, :], wuv, cache_ckv, cache_kpe_t)
    return out[:, 0, :]


def _mix_out_kernel(x_ref, oa_ref, y_ref, r_ref, k2_ref, v_ref, g_ref, gpre_ref, wga_ref, wgb_ref,
                    rk_ref, lnw_ref, lnb_ref, e_ref, wpa_ref, wpb_ref, wout_ref, gpost_ref, gmem_ref,
                    wmq_ref, x1_ref, qm_ref):
    x = x_ref[...]
    h = _rms(x, gpre_ref[...]).astype(BF16)
    gate_a = _sigmoid(jnp.dot(h, wga_ref[...], preferred_element_type=F32))
    gate_b = _sigmoid(jnp.dot(h, wgb_ref[...], preferred_element_type=F32))
    e = e_ref[...]
    y = y_ref[...]
    inv = 1.0 / B_HDIM
    d = y - _seg_sum(y, e) * inv
    var = _seg_sum(d * d, e) * inv
    yn = d * lax.rsqrt(var + GN_EPS) * lnw_ref[...] + lnb_ref[...]
    v = v_ref[...]
    bonus = _seg_sum(r_ref[...] * k2_ref[...] * rk_ref[...], e) * v
    ob = (yn + bonus) * g_ref[...]
    merged = gate_a * _dot(oa_ref[...], wpa_ref[...]) + gate_b * _dot(ob, wpb_ref[...])
    x1 = x + _rms(_dot(merged, wout_ref[...]), gpost_ref[...])
    x1_ref[...] = x1
    qm = _dot(_rms(x1, gmem_ref[...]), wmq_ref[...]) * ((x.shape[-1] // X_HEADS) ** -0.5)
    qm_ref[...] = qm.astype(BF16)


def _mix_out(x, oa, y, r, k2, v, g, *weights):
    m, d = x.shape
    tm = min(ROW_TILE, m)
    wide = _rows(tm, d)
    half = _rows(tm, B_WIDTH)
    return pl.pallas_call(
        _mix_out_kernel, grid=(m // tm,),
        in_specs=[wide] + [half] * 6 + [_full(w.shape) for w in weights],
        out_specs=[wide, wide],
        out_shape=[jax.ShapeDtypeStruct((m, d), F32), jax.ShapeDtypeStruct((m, d), BF16)],
        compiler_params=_params("parallel"), name="mix_out",
    )(x, oa, y, r, k2, v, g, *weights)


def _mem_kv_kernel(mem_ref, g_ref, wk_ref, wv_ref, k_ref, v_ref):
    h = _rms(mem_ref[...], g_ref[...]).astype(BF16)
    k_ref[...] = jnp.dot(h, wk_ref[...], preferred_element_type=F32)
    v_ref[...] = jnp.dot(h, wv_ref[...], preferred_element_type=F32)


def _mem_kv(mem, g, wk, wv):
    m, d = mem.shape
    o = jax.ShapeDtypeStruct((m, d), F32)
    return pl.pallas_call(
        _mem_kv_kernel, grid=(1,),
        in_specs=[_full(mem.shape), _full(g.shape), _full(wk.shape), _full(wv.shape)],
        out_specs=[_full((m, d))] * 2, out_shape=[o, o],
        compiler_params=_params("arbitrary"), name="mem_kv",
    )(mem, g, wk, wv)


def _softmax_rows(s):
    pr = jnp.exp(s - jnp.max(s, axis=-1, keepdims=True))
    return pr / jnp.sum(pr, axis=-1, keepdims=True)


def _mem_attn_shared_kernel(q_ref, k_ref, v_ref, o_ref):
    hd = q_ref.shape[-1] // X_HEADS
    heads = [slice(hh * hd, (hh + 1) * hd) for hh in range(X_HEADS)]
    s = [_dot_nt(q_ref[:, sl], k_ref[:, sl]) for sl in heads]
    pr = [_softmax_rows(sh) for sh in s]
    o = [_dot(pr[hh], v_ref[:, heads[hh]]) for hh in range(X_HEADS)]
    for hh in range(X_HEADS):
        o_ref[:, heads[hh]] = o[hh].astype(o_ref.dtype)


def _mem_attn_shared(q, mk, mv):
    m, d = q.shape
    tm = min(ROW_TILE, m)
    return pl.pallas_call(
        _mem_attn_shared_kernel, grid=(m // tm,),
        in_specs=[_rows(tm, d), _full(mk.shape), _full(mv.shape)],
        out_specs=_rows(tm, d), out_shape=jax.ShapeDtypeStruct((m, d), BF16),
        compiler_params=_params("parallel"), name="mem_attn_shared",
    )(q, mk, mv)


def _mem_attn_rows_kernel(q_ref, k_ref, v_ref, o_ref):
    for i in range(q_ref.shape[0]):
        prod = k_ref[i] * q_ref[i][None]
        s = jnp.sum(jnp.sum(prod, axis=1), axis=-1, keepdims=True)
        pr = jnp.exp(s - jnp.max(s, axis=0, keepdims=True))
        w = pr / jnp.sum(pr, axis=0, keepdims=True)
        o_ref[i] = jnp.sum(w[:, None] * v_ref[i], axis=0)


def _mem_attn_rows(q, mk, mv):
    n, mem, heads, hd = mk.shape
    nb = 4
    tiles = hd // LANES
    tiled = lambda x, lead: jnp.swapaxes(x.reshape(lead + (heads, tiles, LANES)), -2, -3)
    qspec = pl.BlockSpec((nb, tiles, heads, LANES), lambda i: (i, 0, 0, 0))
    kspec = pl.BlockSpec((nb, mem, tiles, heads, LANES), lambda i: (i, 0, 0, 0, 0))
    out = pl.pallas_call(
        _mem_attn_rows_kernel, grid=(n // nb,),
        in_specs=[qspec, kspec, kspec], out_specs=qspec,
        out_shape=jax.ShapeDtypeStruct((n, tiles, heads, LANES), F32),
        compiler_params=_params("parallel"), name="mem_attn_rows",
    )(tiled(q.astype(F32), (n,)), tiled(mk, (n, mem)), tiled(mv, (n, mem)))
    return jnp.swapaxes(out, 1, 2).reshape(n, heads * hd).astype(BF16)


def _tail_kernel(x1_ref, om_ref, wmo_ref, gpm_ref, gmlp_ref, wup_ref, wdn_ref, gpost_ref, y_ref):
    x2 = x1_ref[...] + _rms(jnp.dot(om_ref[...], wmo_ref[...], preferred_element_type=F32), gpm_ref[...])
    h = _rms(x2, gmlp_ref[...]).astype(BF16)
    u = jnp.maximum(jnp.dot(h, wup_ref[...], preferred_element_type=F32), 0.0)
    ff = jnp.dot((u * u).astype(BF16), wdn_ref[...], preferred_element_type=F32)
    y_ref[...] = x2 + _rms(ff, gpost_ref[...])


def _tail(x1, om, *weights):
    m, d = x1.shape
    tm = min(ROW_TILE, m)
    return pl.pallas_call(
        _tail_kernel, grid=(m // tm,),
        in_specs=[_rows(tm, d), _rows(tm, d)] + [_full(w.shape) for w in weights],
        out_specs=_rows(tm, d), out_shape=jax.ShapeDtypeStruct((m, d), F32),
        compiler_params=_params("parallel"), name="tail",
    )(x1, om, *weights)


def _rot_cols(w):
    half = w.shape[-1] // 2
    return jnp.concatenate([-w[..., half:], w[..., :half]], axis=-1)


def _rope_tables(pos):
    inv = ROPE_THETA ** (-jnp.arange(0, A_ROPE, 2, dtype=F32) / A_ROPE)
    ang = pos.astype(F32)[:, None] * inv[None, :]
    z_lo = jnp.zeros((pos.shape[0], A_NOPE), F32)
    z_hi = jnp.zeros((pos.shape[0], QK_PAD - A_NOPE - A_ROPE), F32)
    cos, sin = jnp.cos(ang), jnp.sin(ang)
    return (jnp.concatenate([z_lo, cos, cos, z_hi], axis=1),
            jnp.concatenate([z_lo, sin, sin, z_hi], axis=1))


def _prep_layer(l, w_in, w_uq, w_uk, w_uv, rw_decay_up, rw_a_up, rw_g_up):
    d = w_in.shape[1]
    wi = w_in[l]
    o_rw = Q_RANK + KV_RANK + A_ROPE
    o_ga = o_rw + RW_COLS
    w_kpe = wi[:, Q_RANK + KV_RANK:o_rw]
    z = lambda n: jnp.zeros((d, n), F32)
    pad_hi = QK_PAD - A_NOPE - A_ROPE
    wlat = jnp.concatenate([wi[:, :Q_RANK + KV_RANK], z(A_NOPE), w_kpe, z(pad_hi),
                            z(A_NOPE), _rot_cols(w_kpe), z(pad_hi)], axis=1).astype(BF16)
    wrw = jnp.pad(wi[:, o_rw:o_ga], ((0, 0), (0, RW_PAD - RW_COLS))).astype(BF16)
    wga = wi[:, o_ga:o_ga + d].astype(BF16)
    wgb = wi[:, o_ga + d:].astype(BF16)
    uq = w_uq[l].reshape(Q_RANK, A_HEADS, A_NOPE + A_ROPE)
    nope, pe = uq[..., :A_NOPE], uq[..., A_NOPE:]
    zq = lambda n: jnp.zeros((Q_RANK, A_HEADS, n), F32)
    wqa = jnp.concatenate([nope, pe, zq(pad_hi)], axis=-1).reshape(Q_RANK, -1).astype(BF16)
    wqb = jnp.concatenate([zq(A_NOPE), _rot_cols(pe), zq(pad_hi)], axis=-1).reshape(Q_RANK, -1).astype(BF16)
    pad_cols = lambda w: jnp.pad(w, ((0, 0), (0, 0), (0, QK_PAD - w.shape[-1]))).reshape(KV_RANK, -1).astype(BF16)
    wk_cols = pad_cols(w_uk[l])
    wv_cols = pad_cols(w_uv[l])
    wk_rows = jnp.pad(jnp.transpose(w_uk[l], (1, 2, 0)), ((0, 0), (0, QK_PAD - A_NOPE), (0, 0))).astype(BF16)
    wv = w_uv[l].reshape(KV_RANK, -1).astype(BF16)
    wl = jnp.zeros((LORA_IN, 3 * B_WIDTH), F32)
    wl = wl.at[:DECAY_LORA, :B_WIDTH].set(rw_decay_up[l])
    wl = wl.at[DECAY_LORA:DECAY_LORA + AAA_LORA, B_WIDTH:2 * B_WIDTH].set(rw_a_up[l])
    wl = wl.at[DECAY_LORA + AAA_LORA:, 2 * B_WIDTH:].set(rw_g_up[l])
    return wlat, wrw, wga, wgb, wqa, wqb, wk_cols, wv_cols, wk_rows, wv, wl.astype(BF16)


def kernel(x_prompt, x_sample, mem_prompt, cache_ckv, cache_kpe, state_wkv, state_shift, cache_mem_k, cache_mem_v, page_table, norm_pre_mix, w_in, q_norm, w_uq, kv_norm, w_uk, w_uv, rw_mu, rw_w0, rw_decay_up, rw_a0, rw_a_up, rw_g_up, rw_k_k, rw_k_a, rw_r_k, rw_ln_w, rw_ln_b, w_proj_a, w_proj_b, w_out, norm_post_mix, norm_pre_mem, mem_norm, w_mq, w_mk, w_mv, w_mo, norm_post_mem, norm_pre_mlp, w_ff_up, w_ff_down, norm_post_mlp):
    depth = w_in.shape[0]
    n_p, seq, d = x_prompt.shape
    n_s, dec_seq, _ = x_sample.shape
    assert n_p == 1 and dec_seq == 1, "one prompt sequence and one new token per decode request"
    past_len = page_table.shape[1] * cache_ckv.shape[2]
    ck_p, sk_p = _rope_tables(jnp.arange(seq))
    ck_s, sk_s = _rope_tables(jnp.full((n_s,), past_len))
    seg = jnp.arange(B_WIDTH) // B_HDIM
    e = (seg[:, None] == seg[None, :]).astype(BF16)
    cache_kpe_t = jnp.swapaxes(cache_kpe, 2, 3)
    row = lambda p, l: p[l].reshape(1, -1)
    bf = lambda p, l: p[l].astype(BF16)

    y_p = x_prompt.reshape(seq, d)
    y_s = x_sample.reshape(n_s, d)
    outs = [[] for _ in range(10)]
    for l in range(depth):
        wlat, wrw, wga, wgb, wqa, wqb, wk_cols, wv_cols, wk_rows, wv, wl = _prep_layer(
            l, w_in, w_uq, w_uk, w_uv, rw_decay_up, rw_a_up, rw_g_up)
        proj_w = (row(norm_pre_mix, l), wlat, wrw, row(q_norm, l), row(kv_norm, l), wqa, wqb)
        prep_w = (row(rw_mu, l), row(rw_w0, l), row(rw_a0, l), wl, row(rw_k_k, l), row(rw_k_a, l), e)
        mix_w = (row(norm_pre_mix, l), wga, wgb, row(rw_r_k, l), row(rw_ln_w, l), row(rw_ln_b, l), e,
                 bf(w_proj_a, l), bf(w_proj_b, l), bf(w_out, l), row(norm_post_mix, l),
                 row(norm_pre_mem, l), bf(w_mq, l))
        tail_w = (bf(w_mo, l), row(norm_post_mem, l), row(norm_pre_mlp, l), bf(w_ff_up, l),
                  bf(w_ff_down, l), row(norm_post_mlp, l))

        q, ckv_p, kpe_p, rw_p, k, v = _proj_in(y_p, *proj_w, ck_p, sk_p, wk_cols, wv_cols, prompt=True)
        oa_p = _mla_prompt(q, k, v)
        r_, lw_, k2_, v_, na_, b_, g_ = _rwkv_prep(rw_p, jnp.zeros((1, RW_COLS), F32), *prep_w, seq=True)
        yb_p, wkv_p = _rwkv_chunked(r_, lw_, k2_, v_, na_, b_)
        x1_p, qm_p = _mix_out(y_p, oa_p, yb_p, r_, k2_, v_, g_, *mix_w)
        mk_p, mv_p = _mem_kv(mem_prompt.reshape(-1, d), row(mem_norm, l), bf(w_mk, l), bf(w_mv, l))
        om_p = _mem_attn_shared(qm_p, mk_p, mv_p)
        y_p = _tail(x1_p, om_p, *tail_w)

        q, ckv_s, kpe_s, rw_s, qabs = _proj_in(y_s, *proj_w, ck_s, sk_s, wk_rows, wv, prompt=False)
        oa_s = _mla_decode(page_table, jnp.transpose(qabs, (1, 0, 2)), jnp.transpose(q, (1, 0, 2)),
                           ckv_s, kpe_s, wv, cache_ckv, cache_kpe_t, l)
        r_, lw_, k2_, v_, na_, b_, g_ = _rwkv_prep(rw_s, state_shift[l], *prep_w, seq=False)
        yb_s, wkv_s = _rwkv_step(state_wkv[l], r_, lw_, k2_, v_, na_, b_)
        x1_s, qm_s = _mix_out(y_s, oa_s, yb_s, r_, k2_, v_, g_, *mix_w)
        om_s = _mem_attn_rows(qm_s, cache_mem_k[l], cache_mem_v[l])
        y_s = _tail(x1_s, om_s, *tail_w)

        mem_heads = (n_p, -1, X_HEADS, d // X_HEADS)
        layer_outs = (ckv_p.reshape(n_p, seq, KV_RANK), kpe_p.reshape(n_p, seq, A_ROPE),
                      wkv_p[None], rw_p[-1:], mk_p.reshape(mem_heads), mv_p.reshape(mem_heads),
                      ckv_s.reshape(n_s, 1, KV_RANK), kpe_s.reshape(n_s, 1, A_ROPE), wkv_s, rw_s)
        for lst, val in zip(outs, layer_outs):
            lst.append(val)
    return (y_p.reshape(n_p, seq, d), y_s.reshape(n_s, 1, d)) + tuple(jnp.stack(o) for o in outs)
```

```python
import functools
import math

import jax
import jax.numpy as jnp
from jax import lax
from jax.experimental import pallas as pl
from jax.experimental.pallas import tpu as pltpu

F32 = jnp.float32
BF16 = jnp.bfloat16

A_HEADS = 8
A_NOPE = 64
A_ROPE = 32
A_VDIM = 64
Q_RANK = 384
KV_RANK = 256
ROPE_THETA = 10000.0
B_HEADS = 8
B_HDIM = 64
B_WIDTH = B_HEADS * B_HDIM
DECAY_LORA = 64
AAA_LORA = 64
GATE_LORA = 160
LORA_IN = DECAY_LORA + AAA_LORA + GATE_LORA
GN_EPS = 64e-5
X_HEADS = 4
NORM_EPS = 1e-6
RW_COLS = 3 * B_WIDTH + LORA_IN

LANES = 128
SUBLANES = 8
QK_PAD = 128
RW_PAD = 15 * LANES
RWKV_CHUNK = 64
RWKV_BLOCK = 128
ATTN_Q_BLOCK = 512
ATTN_KV_BLOCK = 1024
DECODE_PAGES = 16
DECODE_SLOTS = 4
ROW_TILE = 256
VMEM_LIMIT = 48 * 1024 * 1024
QK_SCALE = (A_NOPE + A_ROPE) ** -0.5 * math.log2(math.e)


def _params(*sem):
    return pltpu.CompilerParams(dimension_semantics=sem, vmem_limit_bytes=VMEM_LIMIT)


def _full(shape):
    zeros = (0,) * len(shape)
    return pl.BlockSpec(shape, lambda *_: zeros)


def _rows(tm, width):
    return pl.BlockSpec((tm, width), lambda i: (i, 0))


def _rms(x, g):
    return x * lax.rsqrt(jnp.mean(x * x, axis=-1, keepdims=True) + NORM_EPS) * g


def _dot(a, b):
    return jnp.dot(a.astype(BF16), b.astype(BF16), preferred_element_type=F32)


def _dot_nt(a, b):
    return lax.dot_general(a.astype(BF16), b.astype(BF16), (((1,), (1,)), ((), ())),
                           preferred_element_type=F32)


def _dot_tn(a, b):
    return lax.dot_general(a.astype(BF16), b.astype(BF16), (((0,), (0,)), ((), ())),
                           preferred_element_type=F32)


def _sigmoid(x):
    return 1.0 / (1.0 + jnp.exp(-x))


def _seg_sum(x, e):
    hi = x.astype(BF16)
    lo = (x - hi.astype(F32)).astype(BF16)
    return (jnp.dot(hi, e, preferred_element_type=F32)
            + jnp.dot(lo, e, preferred_element_type=F32))


def _proj_in_kernel(prompt, x_ref, g_ref, wlat_ref, wrw_ref, qn_ref, kvn_ref, wqa_ref, wqb_ref,
                    ck_ref, sk_ref, wk_ref, wv_ref, *outs):
    if prompt:
        q_ref, ckv_ref, kpe_ref, rw_ref, k_ref, v_ref = outs
    else:
        q_ref, ckv_ref, kpe_ref, rw_ref, qabs_ref = outs
    h = _rms(x_ref[...], g_ref[...]).astype(BF16)
    lat = jnp.dot(h, wlat_ref[...], preferred_element_type=F32)
    cq = lat[:, :Q_RANK]
    ckv = _rms(lat[:, Q_RANK:Q_RANK + KV_RANK], kvn_ref[...])
    ck = ck_ref[...]
    sk = sk_ref[...]
    kpe = lat[:, 640:768] * ck + lat[:, 768:896] * sk
    ckv_ref[...] = ckv
    kpe_ref[...] = kpe[:, A_NOPE:A_NOPE + A_ROPE]
    rw = jnp.dot(h, wrw_ref[...], preferred_element_type=F32)
    rw_ref[...] = rw[:, :RW_COLS]
    qn = _rms(cq, qn_ref[...]).astype(BF16)
    qa = jnp.dot(qn, wqa_ref[...], preferred_element_type=F32)
    qb = jnp.dot(qn, wqb_ref[...], preferred_element_type=F32)
    lane = lax.broadcasted_iota(jnp.int32, ck.shape, 1)
    cq_t = QK_SCALE * (ck + jnp.where(lane < A_NOPE, 1.0, 0.0))
    sq_t = QK_SCALE * sk
    ckv_b = ckv.astype(BF16)
    if prompt:
        kn = jnp.dot(ckv_b, wk_ref[...], preferred_element_type=F32)
        vn = jnp.dot(ckv_b, wv_ref[...], preferred_element_type=F32)
        ones_col = jnp.where(lane == A_VDIM, 1.0, 0.0)
    for hh in range(A_HEADS):
        sl = slice(hh * QK_PAD, (hh + 1) * QK_PAD)
        qh = (qa[:, sl] * cq_t + qb[:, sl] * sq_t).astype(BF16)
        q_ref[hh] = qh
        if prompt:
            k_ref[hh] = (kn[:, sl] + kpe).astype(BF16)
            v_ref[hh] = (vn[:, sl] + ones_col).astype(BF16)
        else:
            qabs_ref[hh] = jnp.dot(qh, wk_ref[hh], preferred_element_type=F32)


def _proj_in(x, g, wlat, wrw, qn, kvn, wqa, wqb, ck, sk, wk, wv, *, prompt):
    m, d = x.shape
    tm = min(ROW_TILE, m)
    heads_out = lambda w, dt: jax.ShapeDtypeStruct((A_HEADS, m, w), dt)
    heads_spec = lambda w: pl.BlockSpec((A_HEADS, tm, w), lambda i: (0, i, 0))
    out_shape = [heads_out(QK_PAD, BF16), jax.ShapeDtypeStruct((m, KV_RANK), F32),
                 jax.ShapeDtypeStruct((m, A_ROPE), F32), jax.ShapeDtypeStruct((m, RW_COLS), F32)]
    out_specs = [heads_spec(QK_PAD), _rows(tm, KV_RANK), _rows(tm, A_ROPE), _rows(tm, RW_COLS)]
    if prompt:
        out_shape += [heads_out(QK_PAD, BF16), heads_out(QK_PAD, BF16)]
        out_specs += [heads_spec(QK_PAD), heads_spec(QK_PAD)]
    else:
        out_shape += [heads_out(KV_RANK, F32)]
        out_specs += [heads_spec(KV_RANK)]
    return pl.pallas_call(
        functools.partial(_proj_in_kernel, prompt),
        grid=(m // tm,),
        in_specs=[_rows(tm, d), _full(g.shape), _full(wlat.shape), _full(wrw.shape), _full(qn.shape),
                  _full(kvn.shape), _full(wqa.shape), _full(wqb.shape), _rows(tm, LANES),
                  _rows(tm, LANES), _full(wk.shape), _full(wv.shape)],
        out_specs=out_specs, out_shape=out_shape,
        compiler_params=_params("parallel"), name="proj_in",
    )(x, g, wlat, wrw, qn, kvn, wqa, wqb, ck, sk, wk, wv)


def _rwkv_prep_kernel(seq, rw_ref, prev_ref, first_ref, mu_ref, w0_ref, a0_ref, wl_ref, kk_ref, ka_ref,
                      e_ref, r_ref, lw_ref, k2_ref, v_ref, na_ref, b_ref, g_ref):
    rw = rw_ref[...]
    if seq:
        above = jnp.where(pl.program_id(0) == 0, first_ref[...], prev_ref[SUBLANES - 1:SUBLANES, :])
        ridx = lax.broadcasted_iota(jnp.int32, rw.shape, 0)
        shifted = jnp.where(ridx == 0, above, pltpu.roll(rw, 1, axis=0))
    else:
        shifted = first_ref[...]
    xs = rw + (shifted - rw) * mu_ref[...]
    r = xs[:, :B_WIDTH]
    k = xs[:, B_WIDTH:2 * B_WIDTH]
    v = xs[:, 2 * B_WIDTH:3 * B_WIDTH]
    lo = xs[:, 3 * B_WIDTH:]
    lane = lax.broadcasted_iota(jnp.int32, lo.shape, 1)
    t = jnp.where(lane < DECAY_LORA, jnp.tanh(lo),
                  jnp.where(lane < DECAY_LORA + AAA_LORA, lo, _sigmoid(lo)))
    l3 = jnp.dot(t.astype(BF16), wl_ref[...], preferred_element_type=F32)
    z = -(w0_ref[...] + l3[:, :B_WIDTH])
    softplus = jnp.maximum(z, 0.0) + jnp.log(1.0 + jnp.exp(-jnp.abs(z)))
    w = -softplus - 0.5
    a = _sigmoid(a0_ref[...] + l3[:, B_WIDTH:2 * B_WIDTH])
    kkr = k * kk_ref[...]
    norm = jnp.sqrt(_seg_sum(kkr * kkr, e_ref[...]))
    kk = kkr / jnp.maximum(norm, 1e-12)
    r_ref[...] = r
    lw_ref[...] = -jnp.exp(w)
    k2_ref[...] = k * (1.0 + (a - 1.0) * ka_ref[...])
    v_ref[...] = v
    na_ref[...] = -kk
    b_ref[...] = kk * a
    g_ref[...] = l3[:, 2 * B_WIDTH:]


def _rwkv_prep(rw, first, mu, w0, a0, wl, k_k, k_a, e, *, seq):
    m = rw.shape[0]
    tm = min(ROW_TILE, m)
    o = jax.ShapeDtypeStruct((m, B_WIDTH), F32)
    per_tile = tm // SUBLANES
    prev_spec = pl.BlockSpec((SUBLANES, RW_COLS), lambda i: (jnp.maximum(i * per_tile - 1, 0), 0))
    first_spec = _full(first.shape) if seq else _rows(tm, RW_COLS)
    return pl.pallas_call(
        functools.partial(_rwkv_prep_kernel, seq), grid=(m // tm,),
        in_specs=[_rows(tm, RW_COLS), prev_spec, first_spec, _full(mu.shape), _full(w0.shape),
                  _full(a0.shape), _full(wl.shape), _full(k_k.shape), _full(k_a.shape), _full(e.shape)],
        out_specs=[_rows(tm, B_WIDTH)] * 7, out_shape=[o] * 7,
        compiler_params=_params("parallel"), name="rwkv_prep",
    )(rw, rw, first, mu, w0, a0, wl, k_k, k_a, e)


def _rwkv_chunk_kernel(r_ref, lw_ref, k2_ref, v_ref, na_ref, b_ref, y_ref, sout_ref, s_ref):
    C = RWKV_CHUNK
    n_chunks = r_ref.shape[0] // C
    pairs = B_HEADS // 2

    @pl.when(pl.program_id(0) == 0)
    def _():
        s_ref[...] = jnp.zeros_like(s_ref)

    row = lax.broadcasted_iota(jnp.int32, (2 * C, 2 * C), 0)
    col = lax.broadcasted_iota(jnp.int32, (2 * C, 2 * C), 1)
    tok_r = jnp.bitwise_and(row, C - 1)
    tok_c = jnp.bitwise_and(col, C - 1)
    strict = tok_c < tok_r
    incl = tok_c <= tok_r
    eye = jnp.where(row == col, 1.0, 0.0)
    crow = lax.broadcasted_iota(jnp.int32, (C, C), 0)
    ccol = lax.broadcasted_iota(jnp.int32, (C, C), 1)
    cum = jnp.where(ccol <= crow, 1.0, 0.0).astype(BF16)
    head0 = lax.broadcasted_iota(jnp.int32, (C, LANES), 1) < B_HDIM

    def stack(x):
        return jnp.concatenate([jnp.where(head0, x, 0.0), jnp.where(head0, 0.0, x)], axis=0)

    units = [(ci, p) for ci in range(n_chunks) for p in range(pairs)]

    G_all = []
    for ci in range(n_chunks):
        lw = lw_ref[ci * C:(ci + 1) * C, :]
        l_hi = lw.astype(BF16)
        rem = lw - l_hi.astype(F32)
        l_mid = rem.astype(BF16)
        l_lo = (rem - l_mid.astype(F32)).astype(BF16)
        g3 = jnp.dot(cum, jnp.concatenate([l_hi, l_mid, l_lo], axis=1), preferred_element_type=F32)
        G_all.append(g3[:, :B_WIDTH] + g3[:, B_WIDTH:2 * B_WIDTH] + g3[:, 2 * B_WIDTH:])

    ops = {}
    for ci, p in units:
        rows = slice(ci * C, (ci + 1) * C)
        sl = slice(p * LANES, (p + 1) * LANES)
        lw = lw_ref[rows, sl]
        G = G_all[ci][:, sl]
        GC = G[C - 1:C, :]
        e_pos = jnp.exp(G)
        e_neg = jnp.exp(-G)
        e_prev = jnp.exp(G - lw)
        e_tail = jnp.exp(GC - G)
        r, k2, v = r_ref[rows, sl], k2_ref[rows, sl], v_ref[rows, sl]
        na, b = na_ref[rows, sl], b_ref[rows, sl]
        ops[ci, p] = dict(
            AR=jnp.concatenate([stack(na * e_prev), stack(r * e_pos)], axis=0).astype(BF16),
            BK=jnp.concatenate([stack(b * e_neg), stack(k2 * e_neg)], axis=0).astype(BF16),
            Bh=stack(b * e_tail).astype(BF16), Kh=stack(k2 * e_tail).astype(BF16),
            Vs=stack(v).astype(BF16), decay=jnp.exp(GC))

    AA = {u: _dot_nt(ops[u]["AR"], ops[u]["BK"]) for u in units}
    X, Pw, AakArk, Arb = {}, {}, {}, {}
    for u in units:
        aa = AA[u]
        a_ab = jnp.where(strict, aa[:2 * C, :2 * C], 0.0)
        AakArk[u] = jnp.concatenate([jnp.where(strict, aa[:2 * C, 2 * C:], 0.0),
                                     jnp.where(incl, aa[2 * C:, 2 * C:], 0.0)], axis=0).astype(BF16)
        Arb[u] = jnp.where(incl, aa[2 * C:, :2 * C], 0.0).astype(BF16)
        X[u] = eye + a_ab
        Pw[u] = a_ab
    sq = {u: _dot(Pw[u], Pw[u]) for u in units}
    n = 2
    while 2 * n < C:
        both = {u: _dot(jnp.concatenate([X[u], sq[u]], axis=0), sq[u]) for u in units}
        for u in units:
            X[u] = X[u] + both[u][:2 * C]
            sq[u] = both[u][2 * C:]
        n *= 2
    last = {u: _dot(X[u], sq[u]) for u in units}
    for u in units:
        X[u] = (X[u] + last[u]).astype(BF16)
    AV = {u: _dot(AakArk[u], ops[u]["Vs"]) for u in units}
    VK = {u: _dot_tn(ops[u]["Vs"], ops[u]["Kh"]) for u in units}

    S = [s_ref[p] for p in range(pairs)]
    for ci in range(n_chunks):
        SS = [_dot_nt(ops[ci, p]["AR"], S[p]) for p in range(pairs)]
        U = [_dot(X[ci, p], SS[p][:2 * C] + AV[ci, p][:2 * C]) for p in range(pairs)]
        YU = [_dot(Arb[ci, p], U[p]) for p in range(pairs)]
        UB = [_dot_tn(U[p], ops[ci, p]["Bh"]) for p in range(pairs)]
        for p in range(pairs):
            Y = SS[p][2 * C:] + YU[p] + AV[ci, p][2 * C:]
            y_ref[ci * C:(ci + 1) * C, p * LANES:(p + 1) * LANES] = Y[:C] + Y[C:]
            S[p] = S[p] * ops[ci, p]["decay"] + UB[p] + VK[ci, p]
    for p in range(pairs):
        s_ref[p] = S[p]

    @pl.when(pl.program_id(0) == pl.num_programs(0) - 1)
    def _():
        sout_ref[...] = s_ref[...]


def _rwkv_chunked(r, lw, k2, v, na, b):
    t = r.shape[0]
    spec = _rows(RWKV_BLOCK, B_WIDTH)
    pairs = B_HEADS // 2
    y, s = pl.pallas_call(
        _rwkv_chunk_kernel, grid=(t // RWKV_BLOCK,),
        in_specs=[spec] * 6,
        out_specs=[spec, _full((pairs, LANES, LANES))],
        out_shape=[jax.ShapeDtypeStruct((t, B_WIDTH), F32),
                   jax.ShapeDtypeStruct((pairs, LANES, LANES), F32)],
        scratch_shapes=[pltpu.VMEM((pairs, LANES, LANES), F32)],
        compiler_params=_params("arbitrary"), name="rwkv_chunked",
    )(r, lw, k2, v, na, b)
    s = s.reshape(pairs, 2, B_HDIM, 2, B_HDIM)
    wkv = jnp.stack([s[:, 0, :, 0, :], s[:, 1, :, 1, :]], axis=1).reshape(B_HEADS, B_HDIM, B_HDIM)
    return y, wkv


def _rwkv_step_kernel(s_ref, r_ref, lw_ref, k2_ref, na_ref, b_ref, v_ref, sout_ref, y_ref):
    S = s_ref[...]
    sa = jnp.sum(S * na_ref[...], axis=1, keepdims=True)
    S = S * jnp.exp(lw_ref[...]) + sa * b_ref[...] + v_ref[...] * k2_ref[...]
    sout_ref[...] = S
    y_ref[...] = jnp.sum(S * r_ref[...], axis=1, keepdims=True)


def _rwkv_step(state, r, lw, k2, v, na, b):
    n = state.shape[0]
    keyed = lambda x: x.T.reshape(B_HEADS, 1, B_HDIM, n)
    kspec = pl.BlockSpec((None, 1, B_HDIM, n), lambda h: (h, 0, 0, 0))
    vspec = pl.BlockSpec((None, B_HDIM, 1, n), lambda h: (h, 0, 0, 0))
    sspec = pl.BlockSpec((None, B_HDIM, B_HDIM, n), lambda h: (h, 0, 0, 0))
    s_new, y = pl.pallas_call(
        _rwkv_step_kernel, grid=(B_HEADS,),
        in_specs=[sspec, kspec, kspec, kspec, kspec, kspec, vspec],
        out_specs=[sspec, vspec],
        out_shape=[jax.ShapeDtypeStruct((B_HEADS, B_HDIM, B_HDIM, n), F32),
                   jax.ShapeDtypeStruct((B_HEADS, B_HDIM, 1, n), F32)],
        compiler_params=_params("parallel"), name="rwkv_step",
    )(jnp.transpose(state, (1, 2, 3, 0)), keyed(r), keyed(lw), keyed(k2), keyed(na), keyed(b),
      v.T.reshape(B_HEADS, B_HDIM, 1, n))
    return y.reshape(B_WIDTH, n).T, jnp.transpose(s_new, (3, 0, 1, 2))


def _mla_prompt_kernel(q_ref, k_ref, v_ref, o_ref, sa_ref, sb_ref):
    qi = pl.program_id(1)
    TQ, TK = ATTN_Q_BLOCK, ATTN_KV_BLOCK
    q = (q_ref[0], q_ref[1])
    n_full = (qi * TQ) // TK

    def scores(ks, s_ref):
        start = pl.multiple_of(ks * TK, TK)
        for hh in range(2):
            s_ref[hh] = _dot_nt(q[hh], k_ref[hh, pl.ds(start, TK), :])

    def update(carry, s, ks, width):
        start = pl.multiple_of(ks * TK, TK)
        pr, alpha, m_out = [], [], []
        for hh in range(2):
            m_new = jnp.maximum(carry[hh][0], jnp.max(s[hh], axis=-1, keepdims=True))
            alpha.append(jnp.exp2(carry[hh][0] - m_new))
            pr.append(jnp.exp2(s[hh] - m_new).astype(BF16))
            m_out.append(m_new)
        pv = [jnp.dot(pr[hh], v_ref[hh, pl.ds(start, width), :], preferred_element_type=F32)
              for hh in range(2)]
        return tuple((m_out[hh], carry[hh][1] * alpha[hh] + pv[hh]) for hh in range(2))

    def full_block(carry, ks, s_ref, next_ref):
        scores(ks + 1, next_ref)
        return update(carry, [s_ref[hh] for hh in range(2)], ks, TK)

    def body(j, carry):
        carry = full_block(carry, 2 * j, sa_ref, sb_ref)
        return full_block(carry, 2 * j + 1, sb_ref, sa_ref)

    scores(0, sa_ref)
    init = tuple((jnp.full((TQ, 1), -jnp.inf, F32), jnp.zeros((TQ, LANES), F32)) for _ in range(2))
    carry = lax.fori_loop(0, n_full // 2, body, init)
    head0 = lax.broadcasted_iota(jnp.int32, (TQ, LANES), 1) < A_VDIM

    def finish(widths, odd):
        cr, s_ref = carry, sa_ref
        if odd:
            cr, s_ref = full_block(cr, n_full - 1, sa_ref, sb_ref), sb_ref
        width = widths * TQ
        row = lax.broadcasted_iota(jnp.int32, (TQ, width), 0)
        col = lax.broadcasted_iota(jnp.int32, (TQ, width), 1)
        keep = col <= row + (widths - 1) * TQ
        s = [jnp.where(keep, s_ref[hh, :, :width], -jnp.inf) for hh in range(2)]
        (_, a0), (_, a1) = update(cr, s, n_full, width)
        o0 = a0 / a0[:, A_VDIM:A_VDIM + 1]
        o1 = a1 / a1[:, A_VDIM:A_VDIM + 1]
        o_ref[...] = jnp.where(head0, o0, pltpu.roll(o1, A_VDIM, axis=1)).astype(o_ref.dtype)

    for widths in range(1, TK // TQ + 1):
        for odd in (False, True):
            here = jnp.logical_and(qi * TQ - n_full * TK == (widths - 1) * TQ, (n_full % 2 == 1) == odd)
            pl.when(here)(functools.partial(finish, widths, odd))


def _mla_prompt(q, k, v):
    t = q.shape[1]
    TQ, TK = ATTN_Q_BLOCK, ATTN_KV_BLOCK
    assert TK % TQ == 0 and t % TK == 0
    resident = pl.BlockSpec((2, t, QK_PAD), lambda p, i: (p, 0, 0), pipeline_mode=pl.Buffered(1))
    return pl.pallas_call(
        _mla_prompt_kernel, grid=(A_HEADS // 2, t // TQ),
        in_specs=[pl.BlockSpec((2, TQ, QK_PAD), lambda p, i: (p, i, 0)), resident, resident],
        out_specs=pl.BlockSpec((TQ, LANES), lambda p, i: (i, p)),
        out_shape=jax.ShapeDtypeStruct((t, A_HEADS * A_VDIM), BF16),
        scratch_shapes=[pltpu.VMEM((2, TQ, TK), F32), pltpu.VMEM((2, TQ, TK), F32)],
        compiler_params=_params("parallel", "arbitrary"), name="mla_prompt",
    )(q, k, v)


def _mla_decode_kernel(layer, pt_ref, qabs_ref, q_ref, cnew_ref, knew_ref, wuv_ref, ckv_hbm, kpe_hbm,
                       o_ref, cbuf, kbuf, kcat, pcat, csem, ksem):
    P = DECODE_PAGES
    n = pl.program_id(0)
    n_steps = pt_ref.shape[1] // P
    page = cbuf.shape[2]

    def page_copies(pid, slot, i):
        return (pltpu.make_async_copy(ckv_hbm.at[layer, pid], cbuf.at[slot, i], csem.at[slot]),
                pltpu.make_async_copy(kpe_hbm.at[layer, pid], kbuf.at[slot, i], ksem.at[slot]))

    def start(req, step, slot):
        for i in range(P):
            for cp in page_copies(pt_ref[req, step * P + i], slot, i):
                cp.start()

    def wait(slot):
        for i in range(P):
            for cp in page_copies(0, slot, i):
                cp.wait()

    ahead = DECODE_SLOTS - 1

    @pl.when(n == 0)
    def _():
        for step in range(ahead):
            start(0, step, step % DECODE_SLOTS)

    qa = qabs_ref[...]
    qpe = q_ref[:, A_NOPE:A_NOPE + A_ROPE]
    m = jnp.full((A_HEADS, 1), -jnp.inf, F32)
    l = jnp.zeros((A_HEADS, 1), F32)
    acc = jnp.zeros((A_HEADS, KV_RANK), F32)
    pending = None
    for step in range(n_steps):
        fetch = step + ahead
        if fetch < n_steps:
            start(n, fetch, fetch % DECODE_SLOTS)
        else:
            @pl.when(n + 1 < pl.num_programs(0))
            def _():
                start(n + 1, fetch - n_steps, fetch % DECODE_SLOTS)
        wait(step % DECODE_SLOTS)
        slot = step % 2
        for i in range(P):
            kcat[slot, i * page:(i + 1) * page, :] = cbuf[step % DECODE_SLOTS, i].astype(BF16)
            pcat[slot, :, i * page:(i + 1) * page] = kbuf[step % DECODE_SLOTS, i].astype(BF16)
        s = _dot_nt(qa, kcat[slot]) + jnp.dot(qpe, pcat[slot], preferred_element_type=F32)
        if pending is not None:
            pr_prev, alpha_prev, slot_prev = pending
            acc = acc * alpha_prev + jnp.dot(pr_prev, kcat[slot_prev], preferred_element_type=F32)
        m_new = jnp.maximum(m, jnp.max(s, axis=-1, keepdims=True))
        alpha = jnp.exp2(m - m_new)
        pr = jnp.exp2(s - m_new)
        l = l * alpha + jnp.sum(pr, axis=-1, keepdims=True)
        m = m_new
        pending = (pr.astype(BF16), alpha, slot)
    pr_prev, alpha_prev, slot_prev = pending
    acc = acc * alpha_prev + jnp.dot(pr_prev, kcat[slot_prev], preferred_element_type=F32)

    cnew = cnew_ref[...]
    s_self = (jnp.sum(qa * cnew, axis=-1, keepdims=True)
              + jnp.sum(qpe.astype(F32) * knew_ref[...], axis=-1, keepdims=True))
    m_fin = jnp.maximum(m, s_self)
    al = jnp.exp2(m - m_fin)
    p_self = jnp.exp2(s_self - m_fin)
    o_lat = (acc * al + p_self * cnew) / (l * al + p_self)
    res = _dot(o_lat, wuv_ref[...])
    hrow = lax.broadcasted_iota(jnp.int32, res.shape, 0)
    hcol = lax.broadcasted_iota(jnp.int32, res.shape, 1) // A_VDIM
    o_ref[...] = jnp.sum(jnp.where(hrow == hcol, res, 0.0), axis=0, keepdims=True).astype(o_ref.dtype)


def _mla_decode(page_table, qabs, q, ckv_new, kpe_new, wuv, cache_ckv, cache_kpe_t, layer):
    n, n_pages = page_table.shape
    P = DECODE_PAGES
    page = cache_ckv.shape[2]
    slots = DECODE_SLOTS
    assert (n_pages // P) % slots == 0, "buffer slots rotate per step and must line up across requests"
    req = lambda shape: pl.BlockSpec((None,) + shape, lambda b, pt: (b, 0, 0))
    hbm = pl.BlockSpec(memory_space=pl.ANY)
    grid_spec = pltpu.PrefetchScalarGridSpec(
        num_scalar_prefetch=1, grid=(n,),
        in_specs=[req((A_HEADS, KV_RANK)), req((A_HEADS, QK_PAD)), req((1, KV_RANK)), req((1, A_ROPE)),
                  pl.BlockSpec(wuv.shape, lambda b, pt: (0, 0)), hbm, hbm],
        out_specs=req((1, A_HEADS * A_VDIM)),
        scratch_shapes=[pltpu.VMEM((slots, P, page, KV_RANK), F32), pltpu.VMEM((slots, P, A_ROPE, page), F32),
                        pltpu.VMEM((2, P * page, KV_RANK), BF16), pltpu.VMEM((2, A_ROPE, P * page), BF16),
                        pltpu.SemaphoreType.DMA((slots,)), pltpu.SemaphoreType.DMA((slots,))])
    out = pl.pallas_call(
        functools.partial(_mla_decode_kernel, layer), grid_spec=grid_spec,
        out_shape=jax.ShapeDtypeStruct((n, 1, A_HEADS * A_VDIM), BF16),
        compiler_params=_params("arbitrary"), name="mla_decode",
    )(page_table, qabs, q, ckv_new[:, None, :], kpe_new[:, None, :], wuv, cache_ckv, cache_kpe_t)
    return out[:, 0, :]


def _mix_out_kernel(x_ref, oa_ref, y_ref, r_ref, k2_ref, v_ref, g_ref, gpre_ref, wga_ref, wgb_ref,
                    rk_ref, lnw_ref, lnb_ref, e_ref, wpa_ref, wpb_ref, wout_ref, gpost_ref, gmem_ref,
                    wmq_ref, x1_ref, qm_ref):
    x = x_ref[...]
    h = _rms(x, gpre_ref[...]).astype(BF16)
    gate_a = _sigmoid(jnp.dot(h, wga_ref[...], preferred_element_type=F32))
    gate_b = _sigmoid(jnp.dot(h, wgb_ref[...], preferred_element_type=F32))
    e = e_ref[...]
    y = y_ref[...]
    inv = 1.0 / B_HDIM
    d = y - _seg_sum(y, e) * inv
    var = _seg_sum(d * d, e) * inv
    yn = d * lax.rsqrt(var + GN_EPS) * lnw_ref[...] + lnb_ref[...]
    v = v_ref[...]
    bonus = _seg_sum(r_ref[...] * k2_ref[...] * rk_ref[...], e) * v
    ob = (yn + bonus) * g_ref[...]
    merged = gate_a * _dot(oa_ref[...], wpa_ref[...]) + gate_b * _dot(ob, wpb_ref[...])
    x1 = x + _rms(_dot(merged, wout_ref[...]), gpost_ref[...])
    x1_ref[...] = x1
    qm = _dot(_rms(x1, gmem_ref[...]), wmq_ref[...]) * ((x.shape[-1] // X_HEADS) ** -0.5)
    qm_ref[...] = qm.astype(BF16)


def _mix_out(x, oa, y, r, k2, v, g, *weights):
    m, d = x.shape
    tm = min(ROW_TILE, m)
    wide = _rows(tm, d)
    half = _rows(tm, B_WIDTH)
    return pl.pallas_call(
        _mix_out_kernel, grid=(m // tm,),
        in_specs=[wide] + [half] * 6 + [_full(w.shape) for w in weights],
        out_specs=[wide, wide],
        out_shape=[jax.ShapeDtypeStruct((m, d), F32), jax.ShapeDtypeStruct((m, d), BF16)],
        compiler_params=_params("parallel"), name="mix_out",
    )(x, oa, y, r, k2, v, g, *weights)


def _mem_kv_kernel(mem_ref, g_ref, wk_ref, wv_ref, k_ref, v_ref):
    h = _rms(mem_ref[...], g_ref[...]).astype(BF16)
    k_ref[...] = jnp.dot(h, wk_ref[...], preferred_element_type=F32)
    v_ref[...] = jnp.dot(h, wv_ref[...], preferred_element_type=F32)


def _mem_kv(mem, g, wk, wv):
    m, d = mem.shape
    o = jax.ShapeDtypeStruct((m, d), F32)
    return pl.pallas_call(
        _mem_kv_kernel, grid=(1,),
        in_specs=[_full(mem.shape), _full(g.shape), _full(wk.shape), _full(wv.shape)],
        out_specs=[_full((m, d))] * 2, out_shape=[o, o],
        compiler_params=_params("arbitrary"), name="mem_kv",
    )(mem, g, wk, wv)


def _softmax_rows(s):
    pr = jnp.exp(s - jnp.max(s, axis=-1, keepdims=True))
    return pr / jnp.sum(pr, axis=-1, keepdims=True)


def _mem_attn_shared_kernel(q_ref, k_ref, v_ref, o_ref):
    hd = q_ref.shape[-1] // X_HEADS
    heads = [slice(hh * hd, (hh + 1) * hd) for hh in range(X_HEADS)]
    s = [_dot_nt(q_ref[:, sl], k_ref[:, sl]) for sl in heads]
    pr = [_softmax_rows(sh) for sh in s]
    o = [_dot(pr[hh], v_ref[:, heads[hh]]) for hh in range(X_HEADS)]
    for hh in range(X_HEADS):
        o_ref[:, heads[hh]] = o[hh].astype(o_ref.dtype)


def _mem_attn_shared(q, mk, mv):
    m, d = q.shape
    tm = min(ROW_TILE, m)
    return pl.pallas_call(
        _mem_attn_shared_kernel, grid=(m // tm,),
        in_specs=[_rows(tm, d), _full(mk.shape), _full(mv.shape)],
        out_specs=_rows(tm, d), out_shape=jax.ShapeDtypeStruct((m, d), BF16),
        compiler_params=_params("parallel"), name="mem_attn_shared",
    )(q, mk, mv)


def _mem_attn_rows_kernel(q_ref, k_ref, v_ref, o_ref):
    for i in range(q_ref.shape[0]):
        prod = k_ref[i] * q_ref[i][None]
        s = jnp.sum(jnp.sum(prod, axis=1), axis=-1, keepdims=True)
        pr = jnp.exp(s - jnp.max(s, axis=0, keepdims=True))
        w = pr / jnp.sum(pr, axis=0, keepdims=True)
        o_ref[i] = jnp.sum(w[:, None] * v_ref[i], axis=0)


def _mem_attn_rows(q, mk, mv):
    n, mem, heads, hd = mk.shape
    nb = 4
    tiles = hd // LANES
    tiled = lambda x, lead: jnp.swapaxes(x.reshape(lead + (heads, tiles, LANES)), -2, -3)
    qspec = pl.BlockSpec((nb, tiles, heads, LANES), lambda i: (i, 0, 0, 0))
    kspec = pl.BlockSpec((nb, mem, tiles, heads, LANES), lambda i: (i, 0, 0, 0, 0))
    out = pl.pallas_call(
        _mem_attn_rows_kernel, grid=(n // nb,),
        in_specs=[qspec, kspec, kspec], out_specs=qspec,
        out_shape=jax.ShapeDtypeStruct((n, tiles, heads, LANES), F32),
        compiler_params=_params("parallel"), name="mem_attn_rows",
    )(tiled(q.astype(F32), (n,)), tiled(mk, (n, mem)), tiled(mv, (n, mem)))
    return jnp.swapaxes(out, 1, 2).reshape(n, heads * hd).astype(BF16)


def _tail_kernel(x1_ref, om_ref, wmo_ref, gpm_ref, gmlp_ref, wup_ref, wdn_ref, gpost_ref, y_ref):
    x2 = x1_ref[...] + _rms(jnp.dot(om_ref[...], wmo_ref[...], preferred_element_type=F32), gpm_ref[...])
    h = _rms(x2, gmlp_ref[...]).astype(BF16)
    u = jnp.maximum(jnp.dot(h, wup_ref[...], preferred_element_type=F32), 0.0)
    ff = jnp.dot((u * u).astype(BF16), wdn_ref[...], preferred_element_type=F32)
    y_ref[...] = x2 + _rms(ff, gpost_ref[...])


def _tail(x1, om, *weights):
    m, d = x1.shape
    tm = min(ROW_TILE, m)
    return pl.pallas_call(
        _tail_kernel, grid=(m // tm,),
        in_specs=[_rows(tm, d), _rows(tm, d)] + [_full(w.shape) for w in weights],
        out_specs=_rows(tm, d), out_shape=jax.ShapeDtypeStruct((m, d), F32),
        compiler_params=_params("parallel"), name="tail",
    )(x1, om, *weights)


def _rot_cols(w):
    half = w.shape[-1] // 2
    return jnp.concatenate([-w[..., half:], w[..., :half]], axis=-1)


def _rope_tables(pos):
    inv = ROPE_THETA ** (-jnp.arange(0, A_ROPE, 2, dtype=F32) / A_ROPE)
    ang = pos.astype(F32)[:, None] * inv[None, :]
    z_lo = jnp.zeros((pos.shape[0], A_NOPE), F32)
    z_hi = jnp.zeros((pos.shape[0], QK_PAD - A_NOPE - A_ROPE), F32)
    cos, sin = jnp.cos(ang), jnp.sin(ang)
    return (jnp.concatenate([z_lo, cos, cos, z_hi], axis=1),
            jnp.concatenate([z_lo, sin, sin, z_hi], axis=1))


def _prep_layer(l, w_in, w_uq, w_uk, w_uv, rw_decay_up, rw_a_up, rw_g_up):
    d = w_in.shape[1]
    wi = w_in[l]
    o_rw = Q_RANK + KV_RANK + A_ROPE
    o_ga = o_rw + RW_COLS
    w_kpe = wi[:, Q_RANK + KV_RANK:o_rw]
    z = lambda n: jnp.zeros((d, n), F32)
    pad_hi = QK_PAD - A_NOPE - A_ROPE
    wlat = jnp.concatenate([wi[:, :Q_RANK + KV_RANK], z(A_NOPE), w_kpe, z(pad_hi),
                            z(A_NOPE), _rot_cols(w_kpe), z(pad_hi)], axis=1).astype(BF16)
    wrw = jnp.pad(wi[:, o_rw:o_ga], ((0, 0), (0, RW_PAD - RW_COLS))).astype(BF16)
    wga = wi[:, o_ga:o_ga + d].astype(BF16)
    wgb = wi[:, o_ga + d:].astype(BF16)
    uq = w_uq[l].reshape(Q_RANK, A_HEADS, A_NOPE + A_ROPE)
    nope, pe = uq[..., :A_NOPE], uq[..., A_NOPE:]
    zq = lambda n: jnp.zeros((Q_RANK, A_HEADS, n), F32)
    wqa = jnp.concatenate([nope, pe, zq(pad_hi)], axis=-1).reshape(Q_RANK, -1).astype(BF16)
    wqb = jnp.concatenate([zq(A_NOPE), _rot_cols(pe), zq(pad_hi)], axis=-1).reshape(Q_RANK, -1).astype(BF16)
    pad_cols = lambda w: jnp.pad(w, ((0, 0), (0, 0), (0, QK_PAD - w.shape[-1]))).reshape(KV_RANK, -1).astype(BF16)
    wk_cols = pad_cols(w_uk[l])
    wv_cols = pad_cols(w_uv[l])
    wk_rows = jnp.pad(jnp.transpose(w_uk[l], (1, 2, 0)), ((0, 0), (0, QK_PAD - A_NOPE), (0, 0))).astype(BF16)
    wv = w_uv[l].reshape(KV_RANK, -1).astype(BF16)
    wl = jnp.zeros((LORA_IN, 3 * B_WIDTH), F32)
    wl = wl.at[:DECAY_LORA, :B_WIDTH].set(rw_decay_up[l])
    wl = wl.at[DECAY_LORA:DECAY_LORA + AAA_LORA, B_WIDTH:2 * B_WIDTH].set(rw_a_up[l])
    wl = wl.at[DECAY_LORA + AAA_LORA:, 2 * B_WIDTH:].set(rw_g_up[l])
    return wlat, wrw, wga, wgb, wqa, wqb, wk_cols, wv_cols, wk_rows, wv, wl.astype(BF16)


def kernel(x_prompt, x_sample, mem_prompt, cache_ckv, cache_kpe, state_wkv, state_shift, cache_mem_k, cache_mem_v, page_table, norm_pre_mix, w_in, q_norm, w_uq, kv_norm, w_uk, w_uv, rw_mu, rw_w0, rw_decay_up, rw_a0, rw_a_up, rw_g_up, rw_k_k, rw_k_a, rw_r_k, rw_ln_w, rw_ln_b, w_proj_a, w_proj_b, w_out, norm_post_mix, norm_pre_mem, mem_norm, w_mq, w_mk, w_mv, w_mo, norm_post_mem, norm_pre_mlp, w_ff_up, w_ff_down, norm_post_mlp):
    depth = w_in.shape[0]
    n_p, seq, d = x_prompt.shape
    n_s, dec_seq, _ = x_sample.shape
    assert n_p == 1 and dec_seq == 1, "one prompt sequence and one new token per decode request"
    past_len = page_table.shape[1] * cache_ckv.shape[2]
    ck_p, sk_p = _rope_tables(jnp.arange(seq))
    ck_s, sk_s = _rope_tables(jnp.full((n_s,), past_len))
    seg = jnp.arange(B_WIDTH) // B_HDIM
    e = (seg[:, None] == seg[None, :]).astype(BF16)
    cache_kpe_t = jnp.swapaxes(cache_kpe, 2, 3)
    row = lambda p, l: p[l].reshape(1, -1)
    bf = lambda p, l: p[l].astype(BF16)

    y_p = x_prompt.reshape(seq, d)
    y_s = x_sample.reshape(n_s, d)
    outs = [[] for _ in range(10)]
    for l in range(depth):
        wlat, wrw, wga, wgb, wqa, wqb, wk_cols, wv_cols, wk_rows, wv, wl = _prep_layer(
            l, w_in, w_uq, w_uk, w_uv, rw_decay_up, rw_a_up, rw_g_up)
        proj_w = (row(norm_pre_mix, l), wlat, wrw, row(q_norm, l), row(kv_norm, l), wqa, wqb)
        prep_w = (row(rw_mu, l), row(rw_w0, l), row(rw_a0, l), wl, row(rw_k_k, l), row(rw_k_a, l), e)
        mix_w = (row(norm_pre_mix, l), wga, wgb, row(rw_r_k, l), row(rw_ln_w, l), row(rw_ln_b, l), e,
                 bf(w_proj_a, l), bf(w_proj_b, l), bf(w_out, l), row(norm_post_mix, l),
                 row(norm_pre_mem, l), bf(w_mq, l))
        tail_w = (bf(w_mo, l), row(norm_post_mem, l), row(norm_pre_mlp, l), bf(w_ff_up, l),
                  bf(w_ff_down, l), row(norm_post_mlp, l))

        q, ckv_p, kpe_p, rw_p, k, v = _proj_in(y_p, *proj_w, ck_p, sk_p, wk_cols, wv_cols, prompt=True)
        oa_p = _mla_prompt(q, k, v)
        r_, lw_, k2_, v_, na_, b_, g_ = _rwkv_prep(rw_p, jnp.zeros((1, RW_COLS), F32), *prep_w, seq=True)
        yb_p, wkv_p = _rwkv_chunked(r_, lw_, k2_, v_, na_, b_)
        x1_p, qm_p = _mix_out(y_p, oa_p, yb_p, r_, k2_, v_, g_, *mix_w)
        mk_p, mv_p = _mem_kv(mem_prompt.reshape(-1, d), row(mem_norm, l), bf(w_mk, l), bf(w_mv, l))
        om_p = _mem_attn_shared(qm_p, mk_p, mv_p)
        y_p = _tail(x1_p, om_p, *tail_w)

        q, ckv_s, kpe_s, rw_s, qabs = _proj_in(y_s, *proj_w, ck_s, sk_s, wk_rows, wv, prompt=False)
        oa_s = _mla_decode(page_table, jnp.transpose(qabs, (1, 0, 2)), jnp.transpose(q, (1, 0, 2)),
                           ckv_s, kpe_s, wv, cache_ckv, cache_kpe_t, l)
        r_, lw_, k2_, v_, na_, b_, g_ = _rwkv_prep(rw_s, state_shift[l], *prep_w, seq=False)
        yb_s, wkv_s = _rwkv_step(state_wkv[l], r_, lw_, k2_, v_, na_, b_)
        x1_s, qm_s = _mix_out(y_s, oa_s, yb_s, r_, k2_, v_, g_, *mix_w)
        om_s = _mem_attn_rows(qm_s, cache_mem_k[l], cache_mem_v[l])
        y_s = _tail(x1_s, om_s, *tail_w)

        mem_heads = (n_p, -1, X_HEADS, d // X_HEADS)
        layer_outs = (ckv_p.reshape(n_p, seq, KV_RANK), kpe_p.reshape(n_p, seq, A_ROPE),
                      wkv_p[None], rw_p[-1:], mk_p.reshape(mem_heads), mv_p.reshape(mem_heads),
                      ckv_s.reshape(n_s, 1, KV_RANK), kpe_s.reshape(n_s, 1, A_ROPE), wkv_s, rw_s)
        for lst, val in zip(outs, layer_outs):
            lst.append(val)
    return (y_p.reshape(n_p, seq, d), y_s.reshape(n_s, 1, d)) + tuple(jnp.stack(o) for o in outs)
```

```python
import functools
import math

import jax
import jax.numpy as jnp
from jax import lax
from jax.experimental import pallas as pl
from jax.experimental.pallas import tpu as pltpu

F32 = jnp.float32
BF16 = jnp.bfloat16

A_HEADS = 8
A_NOPE = 64
A_ROPE = 32
A_VDIM = 64
Q_RANK = 384
KV_RANK = 256
ROPE_THETA = 10000.0
B_HEADS = 8
B_HDIM = 64
B_WIDTH = B_HEADS * B_HDIM
DECAY_LORA = 64
AAA_LORA = 64
GATE_LORA = 160
LORA_IN = DECAY_LORA + AAA_LORA + GATE_LORA
GN_EPS = 64e-5
X_HEADS = 4
NORM_EPS = 1e-6
RW_COLS = 3 * B_WIDTH + LORA_IN

LANES = 128
SUBLANES = 8
QK_PAD = 128
RW_PAD = 15 * LANES
RWKV_CHUNK = 64
RWKV_BLOCK = 128
ATTN_Q_BLOCK = 512
ATTN_KV_BLOCK = 1024
DECODE_PAGES = 16
DECODE_SLOTS = 4
ROW_TILE = 256
VMEM_LIMIT = 48 * 1024 * 1024
QK_SCALE = (A_NOPE + A_ROPE) ** -0.5 * math.log2(math.e)


def _params(*sem):
    return pltpu.CompilerParams(dimension_semantics=sem, vmem_limit_bytes=VMEM_LIMIT)


def _full(shape):
    zeros = (0,) * len(shape)
    return pl.BlockSpec(shape, lambda *_: zeros)


def _rows(tm, width):
    return pl.BlockSpec((tm, width), lambda i: (i, 0))


def _rms(x, g):
    return x * lax.rsqrt(jnp.mean(x * x, axis=-1, keepdims=True) + NORM_EPS) * g


def _dot(a, b):
    return jnp.dot(a.astype(BF16), b.astype(BF16), preferred_element_type=F32)


def _dot_nt(a, b):
    return lax.dot_general(a.astype(BF16), b.astype(BF16), (((1,), (1,)), ((), ())),
                           preferred_element_type=F32)


def _dot_tn(a, b):
    return lax.dot_general(a.astype(BF16), b.astype(BF16), (((0,), (0,)), ((), ())),
                           preferred_element_type=F32)


def _sigmoid(x):
    return 1.0 / (1.0 + jnp.exp(-x))


def _seg_sum(x, e):
    hi = x.astype(BF16)
    lo = (x - hi.astype(F32)).astype(BF16)
    return (jnp.dot(hi, e, preferred_element_type=F32)
            + jnp.dot(lo, e, preferred_element_type=F32))


def _proj_in_kernel(prompt, x_ref, g_ref, wlat_ref, wrw_ref, qn_ref, kvn_ref, wqa_ref, wqb_ref,
                    ck_ref, sk_ref, wk_ref, wv_ref, *outs):
    if prompt:
        q_ref, ckv_ref, kpe_ref, rw_ref, k_ref, v_ref = outs
    else:
        q_ref, ckv_ref, kpe_ref, rw_ref, qabs_ref = outs
    h = _rms(x_ref[...], g_ref[...]).astype(BF16)
    lat = jnp.dot(h, wlat_ref[...], preferred_element_type=F32)
    cq = lat[:, :Q_RANK]
    ckv = _rms(lat[:, Q_RANK:Q_RANK + KV_RANK], kvn_ref[...])
    ck = ck_ref[...]
    sk = sk_ref[...]
    kpe = lat[:, 640:768] * ck + lat[:, 768:896] * sk
    ckv_ref[...] = ckv
    kpe_ref[...] = kpe[:, A_NOPE:A_NOPE + A_ROPE]
    rw = jnp.dot(h, wrw_ref[...], preferred_element_type=F32)
    rw_ref[...] = rw[:, :RW_COLS]
    qn = _rms(cq, qn_ref[...]).astype(BF16)
    qa = jnp.dot(qn, wqa_ref[...], preferred_element_type=F32)
    qb = jnp.dot(qn, wqb_ref[...], preferred_element_type=F32)
    lane = lax.broadcasted_iota(jnp.int32, ck.shape, 1)
    cq_t = QK_SCALE * (ck + jnp.where(lane < A_NOPE, 1.0, 0.0))
    sq_t = QK_SCALE * sk
    ckv_b = ckv.astype(BF16)
    if prompt:
        kn = jnp.dot(ckv_b, wk_ref[...], preferred_element_type=F32)
        vn = jnp.dot(ckv_b, wv_ref[...], preferred_element_type=F32)
        ones_col = jnp.where(lane == A_VDIM, 1.0, 0.0)
    for hh in range(A_HEADS):
        sl = slice(hh * QK_PAD, (hh + 1) * QK_PAD)
        qh = (qa[:, sl] * cq_t + qb[:, sl] * sq_t).astype(BF16)
        q_ref[hh] = qh
        if prompt:
            k_ref[hh] = (kn[:, sl] + kpe).astype(BF16)
            v_ref[hh] = (vn[:, sl] + ones_col).T.astype(BF16)
        else:
            qabs_ref[hh] = jnp.dot(qh, wk_ref[hh], preferred_element_type=F32)


def _proj_in(x, g, wlat, wrw, qn, kvn, wqa, wqb, ck, sk, wk, wv, *, prompt):
    m, d = x.shape
    tm = min(ROW_TILE, m)
    heads_out = lambda w, dt: jax.ShapeDtypeStruct((A_HEADS, m, w), dt)
    heads_spec = lambda w: pl.BlockSpec((A_HEADS, tm, w), lambda i: (0, i, 0))
    out_shape = [heads_out(QK_PAD, BF16), jax.ShapeDtypeStruct((m, KV_RANK), F32),
                 jax.ShapeDtypeStruct((m, A_ROPE), F32), jax.ShapeDtypeStruct((m, RW_COLS), F32)]
    out_specs = [heads_spec(QK_PAD), _rows(tm, KV_RANK), _rows(tm, A_ROPE), _rows(tm, RW_COLS)]
    if prompt:
        out_shape += [heads_out(QK_PAD, BF16), jax.ShapeDtypeStruct((A_HEADS, QK_PAD, m), BF16)]
        out_specs += [heads_spec(QK_PAD), pl.BlockSpec((A_HEADS, QK_PAD, tm), lambda i: (0, 0, i))]
    else:
        out_shape += [heads_out(KV_RANK, F32)]
        out_specs += [heads_spec(KV_RANK)]
    return pl.pallas_call(
        functools.partial(_proj_in_kernel, prompt),
        grid=(m // tm,),
        in_specs=[_rows(tm, d), _full(g.shape), _full(wlat.shape), _full(wrw.shape), _full(qn.shape),
                  _full(kvn.shape), _full(wqa.shape), _full(wqb.shape), _rows(tm, LANES),
                  _rows(tm, LANES), _full(wk.shape), _full(wv.shape)],
        out_specs=out_specs, out_shape=out_shape,
        compiler_params=_params("parallel"), name="proj_in",
    )(x, g, wlat, wrw, qn, kvn, wqa, wqb, ck, sk, wk, wv)


def _rwkv_prep_kernel(seq, rw_ref, prev_ref, first_ref, mu_ref, w0_ref, a0_ref, wl_ref, kk_ref, ka_ref,
                      e_ref, r_ref, lw_ref, k2_ref, v_ref, na_ref, b_ref, g_ref):
    rw = rw_ref[...]
    if seq:
        above = jnp.where(pl.program_id(0) == 0, first_ref[...], prev_ref[SUBLANES - 1:SUBLANES, :])
        ridx = lax.broadcasted_iota(jnp.int32, rw.shape, 0)
        shifted = jnp.where(ridx == 0, above, pltpu.roll(rw, 1, axis=0))
    else:
        shifted = first_ref[...]
    xs = rw + (shifted - rw) * mu_ref[...]
    r = xs[:, :B_WIDTH]
    k = xs[:, B_WIDTH:2 * B_WIDTH]
    v = xs[:, 2 * B_WIDTH:3 * B_WIDTH]
    lo = xs[:, 3 * B_WIDTH:]
    lane = lax.broadcasted_iota(jnp.int32, lo.shape, 1)
    t = jnp.where(lane < DECAY_LORA, jnp.tanh(lo),
                  jnp.where(lane < DECAY_LORA + AAA_LORA, lo, _sigmoid(lo)))
    l3 = jnp.dot(t.astype(BF16), wl_ref[...], preferred_element_type=F32)
    z = -(w0_ref[...] + l3[:, :B_WIDTH])
    softplus = jnp.maximum(z, 0.0) + jnp.log(1.0 + jnp.exp(-jnp.abs(z)))
    w = -softplus - 0.5
    a = _sigmoid(a0_ref[...] + l3[:, B_WIDTH:2 * B_WIDTH])
    kkr = k * kk_ref[...]
    norm = jnp.sqrt(_seg_sum(kkr * kkr, e_ref[...]))
    kk = kkr / jnp.maximum(norm, 1e-12)
    r_ref[...] = r
    lw_ref[...] = -jnp.exp(w)
    k2_ref[...] = k * (1.0 + (a - 1.0) * ka_ref[...])
    v_ref[...] = v
    na_ref[...] = -kk
    b_ref[...] = kk * a
    g_ref[...] = l3[:, 2 * B_WIDTH:]


def _rwkv_prep(rw, first, mu, w0, a0, wl, k_k, k_a, e, *, seq):
    m = rw.shape[0]
    tm = min(ROW_TILE, m)
    o = jax.ShapeDtypeStruct((m, B_WIDTH), F32)
    per_tile = tm // SUBLANES
    prev_spec = pl.BlockSpec((SUBLANES, RW_COLS), lambda i: (jnp.maximum(i * per_tile - 1, 0), 0))
    first_spec = _full(first.shape) if seq else _rows(tm, RW_COLS)
    return pl.pallas_call(
        functools.partial(_rwkv_prep_kernel, seq), grid=(m // tm,),
        in_specs=[_rows(tm, RW_COLS), prev_spec, first_spec, _full(mu.shape), _full(w0.shape),
                  _full(a0.shape), _full(wl.shape), _full(k_k.shape), _full(k_a.shape), _full(e.shape)],
        out_specs=[_rows(tm, B_WIDTH)] * 7, out_shape=[o] * 7,
        compiler_params=_params("parallel"), name="rwkv_prep",
    )(rw, rw, first, mu, w0, a0, wl, k_k, k_a, e)


def _rwkv_chunk_kernel(r_ref, lw_ref, k2_ref, v_ref, na_ref, b_ref, y_ref, sout_ref, s_ref):
    C = RWKV_CHUNK
    n_chunks = r_ref.shape[0] // C
    pairs = B_HEADS // 2

    @pl.when(pl.program_id(0) == 0)
    def _():
        s_ref[...] = jnp.zeros_like(s_ref)

    row = lax.broadcasted_iota(jnp.int32, (2 * C, 2 * C), 0)
    col = lax.broadcasted_iota(jnp.int32, (2 * C, 2 * C), 1)
    tok_r = jnp.bitwise_and(row, C - 1)
    tok_c = jnp.bitwise_and(col, C - 1)
    strict = tok_c < tok_r
    incl = tok_c <= tok_r
    eye = jnp.where(row == col, 1.0, 0.0)
    crow = lax.broadcasted_iota(jnp.int32, (C, C), 0)
    ccol = lax.broadcasted_iota(jnp.int32, (C, C), 1)
    cum = jnp.where(ccol <= crow, 1.0, 0.0).astype(BF16)
    head0 = lax.broadcasted_iota(jnp.int32, (C, LANES), 1) < B_HDIM

    def stack(x):
        return jnp.concatenate([jnp.where(head0, x, 0.0), jnp.where(head0, 0.0, x)], axis=0)

    units = [(ci, p) for ci in range(n_chunks) for p in range(pairs)]

    G_all = []
    for ci in range(n_chunks):
        lw = lw_ref[ci * C:(ci + 1) * C, :]
        l_hi = lw.astype(BF16)
        rem = lw - l_hi.astype(F32)
        l_mid = rem.astype(BF16)
        l_lo = (rem - l_mid.astype(F32)).astype(BF16)
        g3 = jnp.dot(cum, jnp.concatenate([l_hi, l_mid, l_lo], axis=1), preferred_element_type=F32)
        G_all.append(g3[:, :B_WIDTH] + g3[:, B_WIDTH:2 * B_WIDTH] + g3[:, 2 * B_WIDTH:])

    ops = {}
    for ci, p in units:
        rows = slice(ci * C, (ci + 1) * C)
        sl = slice(p * LANES, (p + 1) * LANES)
        lw = lw_ref[rows, sl]
        G = G_all[ci][:, sl]
        GC = G[C - 1:C, :]
        e_pos = jnp.exp(G)
        e_neg = jnp.exp(-G)
        e_prev = jnp.exp(G - lw)
        e_tail = jnp.exp(GC - G)
        r, k2, v = r_ref[rows, sl], k2_ref[rows, sl], v_ref[rows, sl]
        na, b = na_ref[rows, sl], b_ref[rows, sl]
        ops[ci, p] = dict(
            AR=jnp.concatenate([stack(na * e_prev), stack(r * e_pos)], axis=0).astype(BF16),
            BK=jnp.concatenate([stack(b * e_neg), stack(k2 * e_neg)], axis=0).astype(BF16),
            Bh=stack(b * e_tail).astype(BF16), Kh=stack(k2 * e_tail).astype(BF16),
            Vs=stack(v).astype(BF16), decay=jnp.exp(GC))

    AA = {u: _dot_nt(ops[u]["AR"], ops[u]["BK"]) for u in units}
    X, Pw, AakArk, Arb = {}, {}, {}, {}
    for u in units:
        aa = AA[u]
        a_ab = jnp.where(strict, aa[:2 * C, :2 * C], 0.0)
        AakArk[u] = jnp.concatenate([jnp.where(strict, aa[:2 * C, 2 * C:], 0.0),
                                     jnp.where(incl, aa[2 * C:, 2 * C:], 0.0)], axis=0).astype(BF16)
        Arb[u] = jnp.where(incl, aa[2 * C:, :2 * C], 0.0).astype(BF16)
        X[u] = eye + a_ab
        Pw[u] = a_ab
    sq = {u: _dot(Pw[u], Pw[u]) for u in units}
    n = 2
    while 2 * n < C:
        both = {u: _dot(jnp.concatenate([X[u], sq[u]], axis=0), sq[u]) for u in units}
        for u in units:
            X[u] = X[u] + both[u][:2 * C]
            sq[u] = both[u][2 * C:]
        n *= 2
    last = {u: _dot(X[u], sq[u]) for u in units}
    for u in units:
        X[u] = (X[u] + last[u]).astype(BF16)
    AV = {u: _dot(AakArk[u], ops[u]["Vs"]) for u in units}
    VK = {u: _dot_tn(ops[u]["Vs"], ops[u]["Kh"]) for u in units}

    S = [s_ref[p] for p in range(pairs)]
    for ci in range(n_chunks):
        SS = [_dot_nt(ops[ci, p]["AR"], S[p]) for p in range(pairs)]
        U = [_dot(X[ci, p], SS[p][:2 * C] + AV[ci, p][:2 * C]) for p in range(pairs)]
        YU = [_dot(Arb[ci, p], U[p]) for p in range(pairs)]
        UB = [_dot_tn(U[p], ops[ci, p]["Bh"]) for p in range(pairs)]
        for p in range(pairs):
            Y = SS[p][2 * C:] + YU[p] + AV[ci, p][2 * C:]
            y_ref[ci * C:(ci + 1) * C, p * LANES:(p + 1) * LANES] = Y[:C] + Y[C:]
            S[p] = S[p] * ops[ci, p]["decay"] + UB[p] + VK[ci, p]
    for p in range(pairs):
        s_ref[p] = S[p]

    @pl.when(pl.program_id(0) == pl.num_programs(0) - 1)
    def _():
        sout_ref[...] = s_ref[...]


def _rwkv_chunked(r, lw, k2, v, na, b):
    t = r.shape[0]
    spec = _rows(RWKV_BLOCK, B_WIDTH)
    pairs = B_HEADS // 2
    y, s = pl.pallas_call(
        _rwkv_chunk_kernel, grid=(t // RWKV_BLOCK,),
        in_specs=[spec] * 6,
        out_specs=[spec, _full((pairs, LANES, LANES))],
        out_shape=[jax.ShapeDtypeStruct((t, B_WIDTH), F32),
                   jax.ShapeDtypeStruct((pairs, LANES, LANES), F32)],
        scratch_shapes=[pltpu.VMEM((pairs, LANES, LANES), F32)],
        compiler_params=_params("arbitrary"), name="rwkv_chunked",
    )(r, lw, k2, v, na, b)
    s = s.reshape(pairs, 2, B_HDIM, 2, B_HDIM)
    wkv = jnp.stack([s[:, 0, :, 0, :], s[:, 1, :, 1, :]], axis=1).reshape(B_HEADS, B_HDIM, B_HDIM)
    return y, wkv


def _rwkv_step_kernel(s_ref, r_ref, lw_ref, k2_ref, na_ref, b_ref, v_ref, sout_ref, y_ref):
    S = s_ref[...]
    sa = jnp.sum(S * na_ref[...], axis=1, keepdims=True)
    S = S * jnp.exp(lw_ref[...]) + sa * b_ref[...] + v_ref[...] * k2_ref[...]
    sout_ref[...] = S
    y_ref[...] = jnp.sum(S * r_ref[...], axis=1, keepdims=True)


def _rwkv_step(state, r, lw, k2, v, na, b):
    n = state.shape[0]
    keyed = lambda x: x.T.reshape(B_HEADS, 1, B_HDIM, n)
    kspec = pl.BlockSpec((None, 1, B_HDIM, n), lambda h: (h, 0, 0, 0))
    vspec = pl.BlockSpec((None, B_HDIM, 1, n), lambda h: (h, 0, 0, 0))
    sspec = pl.BlockSpec((None, B_HDIM, B_HDIM, n), lambda h: (h, 0, 0, 0))
    s_new, y = pl.pallas_call(
        _rwkv_step_kernel, grid=(B_HEADS,),
        in_specs=[sspec, kspec, kspec, kspec, kspec, kspec, vspec],
        out_specs=[sspec, vspec],
        out_shape=[jax.ShapeDtypeStruct((B_HEADS, B_HDIM, B_HDIM, n), F32),
                   jax.ShapeDtypeStruct((B_HEADS, B_HDIM, 1, n), F32)],
        compiler_params=_params("parallel"), name="rwkv_step",
    )(jnp.transpose(state, (1, 2, 3, 0)), keyed(r), keyed(lw), keyed(k2), keyed(na), keyed(b),
      v.T.reshape(B_HEADS, B_HDIM, 1, n))
    return y.reshape(B_WIDTH, n).T, jnp.transpose(s_new, (3, 0, 1, 2))


def _mla_prompt_kernel(q_ref, k_ref, v_ref, o_ref, sa_ref, sb_ref):
    qi = pl.program_id(1)
    TQ, TK = ATTN_Q_BLOCK, ATTN_KV_BLOCK
    q = (q_ref[0], q_ref[1])
    n_full = (qi * TQ) // TK

    def scores(ks, s_ref):
        start = pl.multiple_of(ks * TK, TK)
        for hh in range(2):
            s_ref[hh] = _dot_nt(k_ref[hh, pl.ds(start, TK), :], q[hh])

    def update(carry, s, ks, width):
        start = pl.multiple_of(ks * TK, TK)
        pr, alpha, m_out = [], [], []
        for hh in range(2):
            m_new = jnp.maximum(carry[hh][0], jnp.max(s[hh], axis=0, keepdims=True))
            alpha.append(jnp.exp2(carry[hh][0] - m_new))
            pr.append(jnp.exp2(s[hh] - m_new).astype(BF16))
            m_out.append(m_new)
        pv = [jnp.dot(v_ref[hh, :, pl.ds(start, width)], pr[hh], preferred_element_type=F32)
              for hh in range(2)]
        return tuple((m_out[hh], carry[hh][1] * alpha[hh] + pv[hh]) for hh in range(2))

    def full_block(carry, ks, s_ref, next_ref):
        scores(ks + 1, next_ref)
        return update(carry, [s_ref[hh] for hh in range(2)], ks, TK)

    def body(j, carry):
        carry = full_block(carry, 2 * j, sa_ref, sb_ref)
        return full_block(carry, 2 * j + 1, sb_ref, sa_ref)

    scores(0, sa_ref)
    init = tuple((jnp.full((1, TQ), -jnp.inf, F32), jnp.zeros((QK_PAD, TQ), F32)) for _ in range(2))
    carry = lax.fori_loop(0, n_full // 2, body, init)

    def finish(widths, odd):
        cr, s_ref = carry, sa_ref
        if odd:
            cr, s_ref = full_block(cr, n_full - 1, sa_ref, sb_ref), sb_ref
        width = widths * TQ
        key = lax.broadcasted_iota(jnp.int32, (width, TQ), 0)
        qry = lax.broadcasted_iota(jnp.int32, (width, TQ), 1)
        keep = key <= qry + (widths - 1) * TQ
        s = [jnp.where(keep, s_ref[hh, :width, :], -jnp.inf) for hh in range(2)]
        (_, a0), (_, a1) = update(cr, s, n_full, width)
        o0 = a0[:A_VDIM] / a0[A_VDIM:A_VDIM + 1]
        o1 = a1[:A_VDIM] / a1[A_VDIM:A_VDIM + 1]
        o_ref[...] = jnp.concatenate([o0, o1], axis=0).T.astype(o_ref.dtype)

    for widths in range(1, TK // TQ + 1):
        for odd in (False, True):
            here = jnp.logical_and(qi * TQ - n_full * TK == (widths - 1) * TQ, (n_full % 2 == 1) == odd)
            pl.when(here)(functools.partial(finish, widths, odd))


def _mla_prompt(q, k, v):
    t = q.shape[1]
    TQ, TK = ATTN_Q_BLOCK, ATTN_KV_BLOCK
    assert TK % TQ == 0 and t % TK == 0
    k_spec = pl.BlockSpec((2, t, QK_PAD), lambda p, i: (p, 0, 0), pipeline_mode=pl.Buffered(1))
    v_spec = pl.BlockSpec((2, QK_PAD, t), lambda p, i: (p, 0, 0), pipeline_mode=pl.Buffered(1))
    return pl.pallas_call(
        _mla_prompt_kernel, grid=(A_HEADS // 2, t // TQ),
        in_specs=[pl.BlockSpec((2, TQ, QK_PAD), lambda p, i: (p, i, 0)), k_spec, v_spec],
        out_specs=pl.BlockSpec((TQ, LANES), lambda p, i: (i, p)),
        out_shape=jax.ShapeDtypeStruct((t, A_HEADS * A_VDIM), BF16),
        scratch_shapes=[pltpu.VMEM((2, TK, TQ), F32), pltpu.VMEM((2, TK, TQ), F32)],
        compiler_params=_params("parallel", "arbitrary"), name="mla_prompt",
    )(q, k, v)


def _mla_decode_kernel(layer, pt_ref, qabs_ref, q_ref, cnew_ref, knew_ref, wuv_ref, ckv_hbm, kpe_hbm,
                       o_ref, cbuf, kbuf, kcat, pcat, csem, ksem):
    P = DECODE_PAGES
    n = pl.program_id(0)
    n_steps = pt_ref.shape[1] // P
    page = cbuf.shape[2]

    def page_copies(pid, slot, i):
        return (pltpu.make_async_copy(ckv_hbm.at[layer, pid], cbuf.at[slot, i], csem.at[slot]),
                pltpu.make_async_copy(kpe_hbm.at[layer, pid], kbuf.at[slot, i], ksem.at[slot]))

    def start(req, step, slot):
        for i in range(P):
            for cp in page_copies(pt_ref[req, step * P + i], slot, i):
                cp.start()

    def wait(slot):
        for i in range(P):
            for cp in page_copies(0, slot, i):
                cp.wait()

    ahead = DECODE_SLOTS - 1

    @pl.when(n == 0)
    def _():
        for step in range(ahead):
            start(0, step, step % DECODE_SLOTS)

    qa = qabs_ref[...]
    qpe = q_ref[:, A_NOPE:A_NOPE + A_ROPE]
    m = jnp.full((A_HEADS, 1), -jnp.inf, F32)
    l = jnp.zeros((A_HEADS, 1), F32)
    acc = jnp.zeros((A_HEADS, KV_RANK), F32)
    pending = None
    for step in range(n_steps):
        fetch = step + ahead
        if fetch < n_steps:
            start(n, fetch, fetch % DECODE_SLOTS)
        else:
            @pl.when(n + 1 < pl.num_programs(0))
            def _():
                start(n + 1, fetch - n_steps, fetch % DECODE_SLOTS)
        wait(step % DECODE_SLOTS)
        slot = step % 2
        for i in range(P):
            kcat[slot, i * page:(i + 1) * page, :] = cbuf[step % DECODE_SLOTS, i].astype(BF16)
            pcat[slot, :, i * page:(i + 1) * page] = kbuf[step % DECODE_SLOTS, i].astype(BF16)
        s = _dot_nt(qa, kcat[slot]) + jnp.dot(qpe, pcat[slot], preferred_element_type=F32)
        if pending is not None:
            pr_prev, alpha_prev, slot_prev = pending
            acc = acc * alpha_prev + jnp.dot(pr_prev, kcat[slot_prev], preferred_element_type=F32)
        m_new = jnp.maximum(m, jnp.max(s, axis=-1, keepdims=True))
        alpha = jnp.exp2(m - m_new)
        pr = jnp.exp2(s - m_new)
        l = l * alpha + jnp.sum(pr, axis=-1, keepdims=True)
        m = m_new
        pending = (pr.astype(BF16), alpha, slot)
    pr_prev, alpha_prev, slot_prev = pending
    acc = acc * alpha_prev + jnp.dot(pr_prev, kcat[slot_prev], preferred_element_type=F32)

    cnew = cnew_ref[...]
    s_self = (jnp.sum(qa * cnew, axis=-1, keepdims=True)
              + jnp.sum(qpe.astype(F32) * knew_ref[...], axis=-1, keepdims=True))
    m_fin = jnp.maximum(m, s_self)
    al = jnp.exp2(m - m_fin)
    p_self = jnp.exp2(s_self - m_fin)
    o_lat = (acc * al + p_self * cnew) / (l * al + p_self)
    res = _dot(o_lat, wuv_ref[...])
    hrow = lax.broadcasted_iota(jnp.int32, res.shape, 0)
    hcol = lax.broadcasted_iota(jnp.int32, res.shape, 1) // A_VDIM
    o_ref[...] = jnp.sum(jnp.where(hrow == hcol, res, 0.0), axis=0, keepdims=True).astype(o_ref.dtype)


def _mla_decode(page_table, qabs, q, ckv_new, kpe_new, wuv, cache_ckv, cache_kpe_t, layer):
    n, n_pages = page_table.shape
    P = DECODE_PAGES
    page = cache_ckv.shape[2]
    slots = DECODE_SLOTS
    assert (n_pages // P) % slots == 0, "buffer slots rotate per step and must line up across requests"
    req = lambda shape: pl.BlockSpec((None,) + shape, lambda b, pt: (b, 0, 0))
    hbm = pl.BlockSpec(memory_space=pl.ANY)
    grid_spec = pltpu.PrefetchScalarGridSpec(
        num_scalar_prefetch=1, grid=(n,),
        in_specs=[req((A_HEADS, KV_RANK)), req((A_HEADS, QK_PAD)), req((1, KV_RANK)), req((1, A_ROPE)),
                  pl.BlockSpec(wuv.shape, lambda b, pt: (0, 0)), hbm, hbm],
        out_specs=req((1, A_HEADS * A_VDIM)),
        scratch_shapes=[pltpu.VMEM((slots, P, page, KV_RANK), F32), pltpu.VMEM((slots, P, A_ROPE, page), F32),
                        pltpu.VMEM((2, P * page, KV_RANK), BF16), pltpu.VMEM((2, A_ROPE, P * page), BF16),
                        pltpu.SemaphoreType.DMA((slots,)), pltpu.SemaphoreType.DMA((slots,))])
    out = pl.pallas_call(
        functools.partial(_mla_decode_kernel, layer), grid_spec=grid_spec,
        out_shape=jax.ShapeDtypeStruct((n, 1, A_HEADS * A_VDIM), BF16),
        compiler_params=_params("arbitrary"), name="mla_decode",
    )(page_table, qabs, q, ckv_new[:, None, :], kpe_new[:, None, :], wuv, cache_ckv, cache_kpe_t)
    return out[:, 0, :]


def _mix_out_kernel(x_ref, oa_ref, y_ref, r_ref, k2_ref, v_ref, g_ref, gpre_ref, wga_ref, wgb_ref,
                    rk_ref, lnw_ref, lnb_ref, e_ref, wpa_ref, wpb_ref, wout_ref, gpost_ref, gmem_ref,
                    wmq_ref, x1_ref, qm_ref):
    x = x_ref[...]
    h = _rms(x, gpre_ref[...]).astype(BF16)
    gate_a = _sigmoid(jnp.dot(h, wga_ref[...], preferred_element_type=F32))
    gate_b = _sigmoid(jnp.dot(h, wgb_ref[...], preferred_element_type=F32))
    e = e_ref[...]
    y = y_ref[...]
    inv = 1.0 / B_HDIM
    d = y - _seg_sum(y, e) * inv
    var = _seg_sum(d * d, e) * inv
    yn = d * lax.rsqrt(var + GN_EPS) * lnw_ref[...] + lnb_ref[...]
    v = v_ref[...]
    bonus = _seg_sum(r_ref[...] * k2_ref[...] * rk_ref[...], e) * v
    ob = (yn + bonus) * g_ref[...]
    merged = gate_a * _dot(oa_ref[...], wpa_ref[...]) + gate_b * _dot(ob, wpb_ref[...])
    x1 = x + _rms(_dot(merged, wout_ref[...]), gpost_ref[...])
    x1_ref[...] = x1
    qm = _dot(_rms(x1, gmem_ref[...]), wmq_ref[...]) * ((x.shape[-1] // X_HEADS) ** -0.5)
    qm_ref[...] = qm.astype(BF16)


def _mix_out(x, oa, y, r, k2, v, g, *weights):
    m, d = x.shape
    tm = min(ROW_TILE, m)
    wide = _rows(tm, d)
    half = _rows(tm, B_WIDTH)
    return pl.pallas_call(
        _mix_out_kernel, grid=(m // tm,),
        in_specs=[wide] + [half] * 6 + [_full(w.shape) for w in weights],
        out_specs=[wide, wide],
        out_shape=[jax.ShapeDtypeStruct((m, d), F32), jax.ShapeDtypeStruct((m, d), BF16)],
        compiler_params=_params("parallel"), name="mix_out",
    )(x, oa, y, r, k2, v, g, *weights)


def _mem_kv_kernel(mem_ref, g_ref, wk_ref, wv_ref, k_ref, v_ref):
    h = _rms(mem_ref[...], g_ref[...]).astype(BF16)
    k_ref[...] = jnp.dot(h, wk_ref[...], preferred_element_type=F32)
    v_ref[...] = jnp.dot(h, wv_ref[...], preferred_element_type=F32)


def _mem_kv(mem, g, wk, wv):
    m, d = mem.shape
    o = jax.ShapeDtypeStruct((m, d), F32)
    return pl.pallas_call(
        _mem_kv_kernel, grid=(1,),
        in_specs=[_full(mem.shape), _full(g.shape), _full(wk.shape), _full(wv.shape)],
        out_specs=[_full((m, d))] * 2, out_shape=[o, o],
        compiler_params=_params("arbitrary"), name="mem_kv",
    )(mem, g, wk, wv)


def _softmax_rows(s):
    pr = jnp.exp(s - jnp.max(s, axis=-1, keepdims=True))
    return pr / jnp.sum(pr, axis=-1, keepdims=True)


def _mem_attn_shared_kernel(q_ref, k_ref, v_ref, o_ref):
    hd = q_ref.shape[-1] // X_HEADS
    heads = [slice(hh * hd, (hh + 1) * hd) for hh in range(X_HEADS)]
    s = [_dot_nt(q_ref[:, sl], k_ref[:, sl]) for sl in heads]
    pr = [_softmax_rows(sh) for sh in s]
    o = [_dot(pr[hh], v_ref[:, heads[hh]]) for hh in range(X_HEADS)]
    for hh in range(X_HEADS):
        o_ref[:, heads[hh]] = o[hh].astype(o_ref.dtype)


def _mem_attn_shared(q, mk, mv):
    m, d = q.shape
    tm = min(ROW_TILE, m)
    return pl.pallas_call(
        _mem_attn_shared_kernel, grid=(m // tm,),
        in_specs=[_rows(tm, d), _full(mk.shape), _full(mv.shape)],
        out_specs=_rows(tm, d), out_shape=jax.ShapeDtypeStruct((m, d), BF16),
        compiler_params=_params("parallel"), name="mem_attn_shared",
    )(q, mk, mv)


def _mem_attn_rows_kernel(q_ref, k_ref, v_ref, o_ref):
    for i in range(q_ref.shape[0]):
        prod = k_ref[i] * q_ref[i][None]
        s = jnp.sum(jnp.sum(prod, axis=1), axis=-1, keepdims=True)
        pr = jnp.exp(s - jnp.max(s, axis=0, keepdims=True))
        w = pr / jnp.sum(pr, axis=0, keepdims=True)
        o_ref[i] = jnp.sum(w[:, None] * v_ref[i], axis=0)


def _mem_attn_rows(q, mk, mv):
    n, mem, heads, hd = mk.shape
    nb = 4
    tiles = hd // LANES
    tiled = lambda x, lead: jnp.swapaxes(x.reshape(lead + (heads, tiles, LANES)), -2, -3)
    qspec = pl.BlockSpec((nb, tiles, heads, LANES), lambda i: (i, 0, 0, 0))
    kspec = pl.BlockSpec((nb, mem, tiles, heads, LANES), lambda i: (i, 0, 0, 0, 0))
    out = pl.pallas_call(
        _mem_attn_rows_kernel, grid=(n // nb,),
        in_specs=[qspec, kspec, kspec], out_specs=qspec,
        out_shape=jax.ShapeDtypeStruct((n, tiles, heads, LANES), F32),
        compiler_params=_params("parallel"), name="mem_attn_rows",
    )(tiled(q.astype(F32), (n,)), tiled(mk, (n, mem)), tiled(mv, (n, mem)))
    return jnp.swapaxes(out, 1, 2).reshape(n, heads * hd).astype(BF16)


def _tail_kernel(x1_ref, om_ref, wmo_ref, gpm_ref, gmlp_ref, wup_ref, wdn_ref, gpost_ref, y_ref):
    x2 = x1_ref[...] + _rms(jnp.dot(om_ref[...], wmo_ref[...], preferred_element_type=F32), gpm_ref[...])
    h = _rms(x2, gmlp_ref[...]).astype(BF16)
    u = jnp.maximum(jnp.dot(h, wup_ref[...], preferred_element_type=F32), 0.0)
    ff = jnp.dot((u * u).astype(BF16), wdn_ref[...], preferred_element_type=F32)
    y_ref[...] = x2 + _rms(ff, gpost_ref[...])


def _tail(x1, om, *weights):
    m, d = x1.shape
    tm = min(ROW_TILE, m)
    return pl.pallas_call(
        _tail_kernel, grid=(m // tm,),
        in_specs=[_rows(tm, d), _rows(tm, d)] + [_full(w.shape) for w in weights],
        out_specs=_rows(tm, d), out_shape=jax.ShapeDtypeStruct((m, d), F32),
        compiler_params=_params("parallel"), name="tail",
    )(x1, om, *weights)


def _rot_cols(w):
    half = w.shape[-1] // 2
    return jnp.concatenate([-w[..., half:], w[..., :half]], axis=-1)


def _rope_tables(pos):
    inv = ROPE_THETA ** (-jnp.arange(0, A_ROPE, 2, dtype=F32) / A_ROPE)
    ang = pos.astype(F32)[:, None] * inv[None, :]
    z_lo = jnp.zeros((pos.shape[0], A_NOPE), F32)
    z_hi = jnp.zeros((pos.shape[0], QK_PAD - A_NOPE - A_ROPE), F32)
    cos, sin = jnp.cos(ang), jnp.sin(ang)
    return (jnp.concatenate([z_lo, cos, cos, z_hi], axis=1),
            jnp.concatenate([z_lo, sin, sin, z_hi], axis=1))


def _prep_layer(l, w_in, w_uq, w_uk, w_uv, rw_decay_up, rw_a_up, rw_g_up):
    d = w_in.shape[1]
    wi = w_in[l]
    o_rw = Q_RANK + KV_RANK + A_ROPE
    o_ga = o_rw + RW_COLS
    w_kpe = wi[:, Q_RANK + KV_RANK:o_rw]
    z = lambda n: jnp.zeros((d, n), F32)
    pad_hi = QK_PAD - A_NOPE - A_ROPE
    wlat = jnp.concatenate([wi[:, :Q_RANK + KV_RANK], z(A_NOPE), w_kpe, z(pad_hi),
                            z(A_NOPE), _rot_cols(w_kpe), z(pad_hi)], axis=1).astype(BF16)
    wrw = jnp.pad(wi[:, o_rw:o_ga], ((0, 0), (0, RW_PAD - RW_COLS))).astype(BF16)
    wga = wi[:, o_ga:o_ga + d].astype(BF16)
    wgb = wi[:, o_ga + d:].astype(BF16)
    uq = w_uq[l].reshape(Q_RANK, A_HEADS, A_NOPE + A_ROPE)
    nope, pe = uq[..., :A_NOPE], uq[..., A_NOPE:]
    zq = lambda n: jnp.zeros((Q_RANK, A_HEADS, n), F32)
    wqa = jnp.concatenate([nope, pe, zq(pad_hi)], axis=-1).reshape(Q_RANK, -1).astype(BF16)
    wqb = jnp.concatenate([zq(A_NOPE), _rot_cols(pe), zq(pad_hi)], axis=-1).reshape(Q_RANK, -1).astype(BF16)
    pad_cols = lambda w: jnp.pad(w, ((0, 0), (0, 0), (0, QK_PAD - w.shape[-1]))).reshape(KV_RANK, -1).astype(BF16)
    wk_cols = pad_cols(w_uk[l])
    wv_cols = pad_cols(w_uv[l])
    wk_rows = jnp.pad(jnp.transpose(w_uk[l], (1, 2, 0)), ((0, 0), (0, QK_PAD - A_NOPE), (0, 0))).astype(BF16)
    wv = w_uv[l].reshape(KV_RANK, -1).astype(BF16)
    wl = jnp.zeros((LORA_IN, 3 * B_WIDTH), F32)
    wl = wl.at[:DECAY_LORA, :B_WIDTH].set(rw_decay_up[l])
    wl = wl.at[DECAY_LORA:DECAY_LORA + AAA_LORA, B_WIDTH:2 * B_WIDTH].set(rw_a_up[l])
    wl = wl.at[DECAY_LORA + AAA_LORA:, 2 * B_WIDTH:].set(rw_g_up[l])
    return wlat, wrw, wga, wgb, wqa, wqb, wk_cols, wv_cols, wk_rows, wv, wl.astype(BF16)


def kernel(x_prompt, x_sample, mem_prompt, cache_ckv, cache_kpe, state_wkv, state_shift, cache_mem_k, cache_mem_v, page_table, norm_pre_mix, w_in, q_norm, w_uq, kv_norm, w_uk, w_uv, rw_mu, rw_w0, rw_decay_up, rw_a0, rw_a_up, rw_g_up, rw_k_k, rw_k_a, rw_r_k, rw_ln_w, rw_ln_b, w_proj_a, w_proj_b, w_out, norm_post_mix, norm_pre_mem, mem_norm, w_mq, w_mk, w_mv, w_mo, norm_post_mem, norm_pre_mlp, w_ff_up, w_ff_down, norm_post_mlp):
    depth = w_in.shape[0]
    n_p, seq, d = x_prompt.shape
    n_s, dec_seq, _ = x_sample.shape
    assert n_p == 1 and dec_seq == 1, "one prompt sequence and one new token per decode request"
    past_len = page_table.shape[1] * cache_ckv.shape[2]
    ck_p, sk_p = _rope_tables(jnp.arange(seq))
    ck_s, sk_s = _rope_tables(jnp.full((n_s,), past_len))
    seg = jnp.arange(B_WIDTH) // B_HDIM
    e = (seg[:, None] == seg[None, :]).astype(BF16)
    cache_kpe_t = jnp.swapaxes(cache_kpe, 2, 3)
    row = lambda p, l: p[l].reshape(1, -1)
    bf = lambda p, l: p[l].astype(BF16)

    y_p = x_prompt.reshape(seq, d)
    y_s = x_sample.reshape(n_s, d)
    outs = [[] for _ in range(10)]
    for l in range(depth):
        wlat, wrw, wga, wgb, wqa, wqb, wk_cols, wv_cols, wk_rows, wv, wl = _prep_layer(
            l, w_in, w_uq, w_uk, w_uv, rw_decay_up, rw_a_up, rw_g_up)
        proj_w = (row(norm_pre_mix, l), wlat, wrw, row(q_norm, l), row(kv_norm, l), wqa, wqb)
        prep_w = (row(rw_mu, l), row(rw_w0, l), row(rw_a0, l), wl, row(rw_k_k, l), row(rw_k_a, l), e)
        mix_w = (row(norm_pre_mix, l), wga, wgb, row(rw_r_k, l), row(rw_ln_w, l), row(rw_ln_b, l), e,
                 bf(w_proj_a, l), bf(w_proj_b, l), bf(w_out, l), row(norm_post_mix, l),
                 row(norm_pre_mem, l), bf(w_mq, l))
        tail_w = (bf(w_mo, l), row(norm_post_mem, l), row(norm_pre_mlp, l), bf(w_ff_up, l),
                  bf(w_ff_down, l), row(norm_post_mlp, l))

        q, ckv_p, kpe_p, rw_p, k, v = _proj_in(y_p, *proj_w, ck_p, sk_p, wk_cols, wv_cols, prompt=True)
        oa_p = _mla_prompt(q, k, v)
        r_, lw_, k2_, v_, na_, b_, g_ = _rwkv_prep(rw_p, jnp.zeros((1, RW_COLS), F32), *prep_w, seq=True)
        yb_p, wkv_p = _rwkv_chunked(r_, lw_, k2_, v_, na_, b_)
        x1_p, qm_p = _mix_out(y_p, oa_p, yb_p, r_, k2_, v_, g_, *mix_w)
        mk_p, mv_p = _mem_kv(mem_prompt.reshape(-1, d), row(mem_norm, l), bf(w_mk, l), bf(w_mv, l))
        om_p = _mem_attn_shared(qm_p, mk_p, mv_p)
        y_p = _tail(x1_p, om_p, *tail_w)

        q, ckv_s, kpe_s, rw_s, qabs = _proj_in(y_s, *proj_w, ck_s, sk_s, wk_rows, wv, prompt=False)
        oa_s = _mla_decode(page_table, jnp.transpose(qabs, (1, 0, 2)), jnp.transpose(q, (1, 0, 2)),
                           ckv_s, kpe_s, wv, cache_ckv, cache_kpe_t, l)
        r_, lw_, k2_, v_, na_, b_, g_ = _rwkv_prep(rw_s, state_shift[l], *prep_w, seq=False)
        yb_s, wkv_s = _rwkv_step(state_wkv[l], r_, lw_, k2_, v_, na_, b_)
        x1_s, qm_s = _mix_out(y_s, oa_s, yb_s, r_, k2_, v_, g_, *mix_w)
        om_s = _mem_attn_rows(qm_s, cache_mem_k[l], cache_mem_v[l])
        y_s = _tail(x1_s, om_s, *tail_w)

        mem_heads = (n_p, -1, X_HEADS, d // X_HEADS)
        layer_outs = (ckv_p.reshape(n_p, seq, KV_RANK), kpe_p.reshape(n_p, seq, A_ROPE),
                      wkv_p[None], rw_p[-1:], mk_p.reshape(mem_heads), mv_p.reshape(mem_heads),
                      ckv_s.reshape(n_s, 1, KV_RANK), kpe_s.reshape(n_s, 1, A_ROPE), wkv_s, rw_s)
        for lst, val in zip(outs, layer_outs):
            lst.append(val)
    return (y_p.reshape(n_p, seq, d), y_s.reshape(n_s, 1, d)) + tuple(jnp.stack(o) for o in outs)
```

```python
import functools
import math

import jax
import jax.numpy as jnp
from jax import lax
from jax.experimental import pallas as pl
from jax.experimental.pallas import tpu as pltpu

F32 = jnp.float32
BF16 = jnp.bfloat16

A_HEADS = 8
A_NOPE = 64
A_ROPE = 32
A_VDIM = 64
Q_RANK = 384
KV_RANK = 256
ROPE_THETA = 10000.0
B_HEADS = 8
B_HDIM = 64
B_WIDTH = B_HEADS * B_HDIM
DECAY_LORA = 64
AAA_LORA = 64
GATE_LORA = 160
LORA_IN = DECAY_LORA + AAA_LORA + GATE_LORA
GN_EPS = 64e-5
X_HEADS = 4
NORM_EPS = 1e-6
RW_COLS = 3 * B_WIDTH + LORA_IN

LANES = 128
SUBLANES = 8
QK_PAD = 128
RW_PAD = 15 * LANES
RWKV_CHUNK = 64
RWKV_BLOCK = 256
ATTN_Q_BLOCK = 1024
ATTN_KV_BLOCK = 1024
DECODE_PAGES = 16
DECODE_SLOTS = 4
ROW_TILE = 256
VMEM_LIMIT = 48 * 1024 * 1024
QK_SCALE = (A_NOPE + A_ROPE) ** -0.5 * math.log2(math.e)


def _params(*sem):
    return pltpu.CompilerParams(dimension_semantics=sem, vmem_limit_bytes=VMEM_LIMIT)


def _full(shape):
    zeros = (0,) * len(shape)
    return pl.BlockSpec(shape, lambda *_: zeros)


def _rows(tm, width):
    return pl.BlockSpec((tm, width), lambda i: (i, 0))


def _rms(x, g):
    return x * lax.rsqrt(jnp.mean(x * x, axis=-1, keepdims=True) + NORM_EPS) * g


def _dot(a, b):
    return jnp.dot(a.astype(BF16), b.astype(BF16), preferred_element_type=F32)


def _dot_nt(a, b):
    return lax.dot_general(a.astype(BF16), b.astype(BF16), (((1,), (1,)), ((), ())),
                           preferred_element_type=F32)


def _dot_tn(a, b):
    return lax.dot_general(a.astype(BF16), b.astype(BF16), (((0,), (0,)), ((), ())),
                           preferred_element_type=F32)


def _sigmoid(x):
    return 1.0 / (1.0 + jnp.exp(-x))


def _seg_sum(x, e):
    hi = x.astype(BF16)
    lo = (x - hi.astype(F32)).astype(BF16)
    w = e.shape[0]
    parts = [jnp.dot(hi[:, c:c + w], e, preferred_element_type=F32)
             + jnp.dot(lo[:, c:c + w], e, preferred_element_type=F32) for c in range(0, x.shape[1], w)]
    return jnp.concatenate(parts, axis=1)


def _rwkv_inputs(rw, shifted, mu, w0, a0, wl, k_k, k_a, e):
    xs = rw + (shifted - rw) * mu
    r = xs[:, :B_WIDTH]
    k = xs[:, B_WIDTH:2 * B_WIDTH]
    v = xs[:, 2 * B_WIDTH:3 * B_WIDTH]
    lo = xs[:, 3 * B_WIDTH:]
    lane = lax.broadcasted_iota(jnp.int32, lo.shape, 1)
    t = jnp.where(lane < DECAY_LORA, jnp.tanh(lo),
                  jnp.where(lane < DECAY_LORA + AAA_LORA, lo, _sigmoid(lo)))
    l3 = jnp.dot(t.astype(BF16), wl, preferred_element_type=F32)
    z = -(w0 + l3[:, :B_WIDTH])
    softplus = jnp.maximum(z, 0.0) + jnp.log(1.0 + jnp.exp(-jnp.abs(z)))
    w = -softplus - 0.5
    a = _sigmoid(a0 + l3[:, B_WIDTH:2 * B_WIDTH])
    kkr = k * k_k
    kk = kkr / jnp.maximum(jnp.sqrt(_seg_sum(kkr * kkr, e)), 1e-12)
    lw = -jnp.exp(w)
    return r, lw, k * (1.0 + (a - 1.0) * k_a), v, -kk, kk * a, l3[:, 2 * B_WIDTH:]


def _proj_in_kernel(prompt, x_ref, g_ref, wlat_ref, wrw_ref, qn_ref, kvn_ref, wqa_ref, wqb_ref,
                    ck_ref, sk_ref, wk_ref, wv_ref, first_ref, mu_ref, w0_ref, a0_ref, wl_ref, kk_ref,
                    ka_ref, e_ref, *rest):
    if prompt:
        q_ref, ckv_ref, kpe_ref, rw_ref, k_ref, v_ref = rest[:6]
        prev_ref = rest[-1]
    else:
        q_ref, ckv_ref, kpe_ref, rw_ref, qabs_ref = rest[:5]
    rwkv_refs = rest[6:13] if prompt else rest[5:12]
    if prompt:
        @pl.when(pl.program_id(0) == 0)
        def _():
            prev_ref[...] = jnp.broadcast_to(first_ref[...], prev_ref.shape)

    h = _rms(x_ref[...], g_ref[...]).astype(BF16)
    lat = jnp.dot(h, wlat_ref[...], preferred_element_type=F32)
    cq = lat[:, :Q_RANK]
    ckv = _rms(lat[:, Q_RANK:Q_RANK + KV_RANK], kvn_ref[...])
    ck = ck_ref[...]
    sk = sk_ref[...]
    kpe = lat[:, 640:768] * ck + lat[:, 768:896] * sk
    ckv_ref[...] = ckv
    kpe_ref[...] = kpe[:, A_NOPE:A_NOPE + A_ROPE]
    rw = jnp.dot(h, wrw_ref[...], preferred_element_type=F32)[:, :RW_COLS]
    if prompt:
        ridx = lax.broadcasted_iota(jnp.int32, rw.shape, 0)
        shifted = jnp.where(ridx == 0, prev_ref[SUBLANES - 1:SUBLANES, :], pltpu.roll(rw, 1, axis=0))
        last_rows = rw[rw.shape[0] - SUBLANES:, :]
        prev_ref[...] = last_rows
        rw_ref[...] = last_rows
    else:
        shifted = first_ref[...]
        rw_ref[...] = rw
    vals = _rwkv_inputs(rw, shifted, mu_ref[...], w0_ref[...], a0_ref[...], wl_ref[...], kk_ref[...],
                        ka_ref[...], e_ref[...])
    for ref, val in zip(rwkv_refs, vals):
        ref[...] = val
    qn = _rms(cq, qn_ref[...]).astype(BF16)
    qa = jnp.dot(qn, wqa_ref[...], preferred_element_type=F32)
    qb = jnp.dot(qn, wqb_ref[...], preferred_element_type=F32)
    lane = lax.broadcasted_iota(jnp.int32, ck.shape, 1)
    cq_t = QK_SCALE * (ck + jnp.where(lane < A_NOPE, 1.0, 0.0))
    sq_t = QK_SCALE * sk
    ckv_b = ckv.astype(BF16)
    if prompt:
        kn = jnp.dot(ckv_b, wk_ref[...], preferred_element_type=F32)
        vn = jnp.dot(ckv_b, wv_ref[...], preferred_element_type=F32)
        ones_col = jnp.where(lane == A_VDIM, 1.0, 0.0)
    for hh in range(A_HEADS):
        sl = slice(hh * QK_PAD, (hh + 1) * QK_PAD)
        qh = (qa[:, sl] * cq_t + qb[:, sl] * sq_t).astype(BF16)
        q_ref[hh] = qh
        if prompt:
            k_ref[hh] = (kn[:, sl] + kpe).astype(BF16)
            v_ref[hh] = (vn[:, sl] + ones_col).T.astype(BF16)
        else:
            qabs_ref[hh] = jnp.dot(qh, wk_ref[hh], preferred_element_type=F32)


def _proj_in(x, g, wlat, wrw, qn, kvn, wqa, wqb, ck, sk, wk, wv, first, *rwkv_w, prompt):
    m, d = x.shape
    tm = min(ROW_TILE, m)
    heads_out = lambda w, dt: jax.ShapeDtypeStruct((A_HEADS, m, w), dt)
    heads_spec = lambda w: pl.BlockSpec((A_HEADS, tm, w), lambda i: (0, i, 0))
    rw_rows = SUBLANES if prompt else m
    out_shape = [heads_out(QK_PAD, BF16), jax.ShapeDtypeStruct((m, KV_RANK), F32),
                 jax.ShapeDtypeStruct((m, A_ROPE), F32), jax.ShapeDtypeStruct((rw_rows, RW_COLS), F32)]
    out_specs = [heads_spec(QK_PAD), _rows(tm, KV_RANK), _rows(tm, A_ROPE),
                 _full((SUBLANES, RW_COLS)) if prompt else _rows(tm, RW_COLS)]
    if prompt:
        out_shape += [heads_out(QK_PAD, BF16), jax.ShapeDtypeStruct((A_HEADS, QK_PAD, m), BF16)]
        out_specs += [heads_spec(QK_PAD), pl.BlockSpec((A_HEADS, QK_PAD, tm), lambda i: (0, 0, i))]
    else:
        out_shape += [heads_out(KV_RANK, F32)]
        out_specs += [heads_spec(KV_RANK)]
    out_shape += [jax.ShapeDtypeStruct((m, B_WIDTH), F32)] * 7
    out_specs += [_rows(tm, B_WIDTH)] * 7
    first_spec = _full(first.shape) if prompt else _rows(tm, RW_COLS)
    return pl.pallas_call(
        functools.partial(_proj_in_kernel, prompt),
        grid=(m // tm,),
        in_specs=[_rows(tm, d), _full(g.shape), _full(wlat.shape), _full(wrw.shape), _full(qn.shape),
                  _full(kvn.shape), _full(wqa.shape), _full(wqb.shape), _rows(tm, LANES),
                  _rows(tm, LANES), _full(wk.shape), _full(wv.shape), first_spec]
                 + [_full(w.shape) for w in rwkv_w],
        out_specs=out_specs, out_shape=out_shape,
        scratch_shapes=[pltpu.VMEM((SUBLANES, RW_COLS), F32)] if prompt else [],
        compiler_params=_params("arbitrary"), name="proj_in",
    )(x, g, wlat, wrw, qn, kvn, wqa, wqb, ck, sk, wk, wv, first, *rwkv_w)


def _rwkv_chunk_kernel(r_ref, lw_ref, k2_ref, v_ref, na_ref, b_ref, y_ref, sout_ref, s_ref):
    C = RWKV_CHUNK
    n_chunks = r_ref.shape[0] // C
    pairs = B_HEADS // 2

    @pl.when(pl.program_id(0) == 0)
    def _():
        s_ref[...] = jnp.zeros_like(s_ref)

    row = lax.broadcasted_iota(jnp.int32, (2 * C, 2 * C), 0)
    col = lax.broadcasted_iota(jnp.int32, (2 * C, 2 * C), 1)
    tok_r = jnp.bitwise_and(row, C - 1)
    tok_c = jnp.bitwise_and(col, C - 1)
    strict = tok_c < tok_r
    incl = tok_c <= tok_r
    eye = jnp.where(row == col, 1.0, 0.0)
    crow = lax.broadcasted_iota(jnp.int32, (C, C), 0)
    ccol = lax.broadcasted_iota(jnp.int32, (C, C), 1)
    cum = jnp.where(ccol <= crow, 1.0, 0.0).astype(BF16)
    head0 = lax.broadcasted_iota(jnp.int32, (C, LANES), 1) < B_HDIM

    def stack(x):
        return jnp.concatenate([jnp.where(head0, x, 0.0), jnp.where(head0, 0.0, x)], axis=0)

    units = [(ci, p) for ci in range(n_chunks) for p in range(pairs)]

    G_all = []
    for ci in range(n_chunks):
        lw = lw_ref[ci * C:(ci + 1) * C, :]
        l_hi = lw.astype(BF16)
        rem = lw - l_hi.astype(F32)
        l_mid = rem.astype(BF16)
        l_lo = (rem - l_mid.astype(F32)).astype(BF16)
        g3 = jnp.dot(cum, jnp.concatenate([l_hi, l_mid, l_lo], axis=1), preferred_element_type=F32)
        G_all.append(g3[:, :B_WIDTH] + g3[:, B_WIDTH:2 * B_WIDTH] + g3[:, 2 * B_WIDTH:])

    ops = {}
    for ci, p in units:
        rows = slice(ci * C, (ci + 1) * C)
        sl = slice(p * LANES, (p + 1) * LANES)
        lw = lw_ref[rows, sl]
        G = G_all[ci][:, sl]
        GC = G[C - 1:C, :]
        e_pos = jnp.exp(G)
        e_neg = jnp.exp(-G)
        e_prev = jnp.exp(G - lw)
        e_tail = jnp.exp(GC - G)
        r, k2, v = r_ref[rows, sl], k2_ref[rows, sl], v_ref[rows, sl]
        na, b = na_ref[rows, sl], b_ref[rows, sl]
        ops[ci, p] = dict(
            AR=jnp.concatenate([stack(na * e_prev), stack(r * e_pos)], axis=0).astype(BF16),
            BK=jnp.concatenate([stack(b * e_neg), stack(k2 * e_neg)], axis=0).astype(BF16),
            Bh=stack(b * e_tail).astype(BF16), Kh=stack(k2 * e_tail).astype(BF16),
            Vs=stack(v).astype(BF16), decay=jnp.exp(GC))

    AA = {u: _dot_nt(ops[u]["AR"], ops[u]["BK"]) for u in units}
    X, Pw, AakArk, Arb = {}, {}, {}, {}
    for u in units:
        aa = AA[u]
        a_ab = jnp.where(strict, aa[:2 * C, :2 * C], 0.0)
        AakArk[u] = jnp.concatenate([jnp.where(strict, aa[:2 * C, 2 * C:], 0.0),
                                     jnp.where(incl, aa[2 * C:, 2 * C:], 0.0)], axis=0).astype(BF16)
        Arb[u] = jnp.where(incl, aa[2 * C:, :2 * C], 0.0).astype(BF16)
        X[u] = eye + a_ab
        Pw[u] = a_ab
    sq = {u: _dot(Pw[u], Pw[u]) for u in units}
    n = 2
    while 2 * n < C:
        both = {u: _dot(jnp.concatenate([X[u], sq[u]], axis=0), sq[u]) for u in units}
        for u in units:
            X[u] = X[u] + both[u][:2 * C]
            sq[u] = both[u][2 * C:]
        n *= 2
    last = {u: _dot(X[u], sq[u]) for u in units}
    for u in units:
        X[u] = (X[u] + last[u]).astype(BF16)
    AV = {u: _dot(AakArk[u], ops[u]["Vs"]) for u in units}
    VK = {u: _dot_tn(ops[u]["Vs"], ops[u]["Kh"]) for u in units}

    S = [s_ref[p] for p in range(pairs)]
    for ci in range(n_chunks):
        SS = [_dot_nt(ops[ci, p]["AR"], S[p]) for p in range(pairs)]
        U = [_dot(X[ci, p], SS[p][:2 * C] + AV[ci, p][:2 * C]) for p in range(pairs)]
        YU = [_dot(Arb[ci, p], U[p]) for p in range(pairs)]
        UB = [_dot_tn(U[p], ops[ci, p]["Bh"]) for p in range(pairs)]
        for p in range(pairs):
            Y = SS[p][2 * C:] + YU[p] + AV[ci, p][2 * C:]
            y_ref[ci * C:(ci + 1) * C, p * LANES:(p + 1) * LANES] = Y[:C] + Y[C:]
            S[p] = S[p] * ops[ci, p]["decay"] + UB[p] + VK[ci, p]
    for p in range(pairs):
        s_ref[p] = S[p]

    @pl.when(pl.program_id(0) == pl.num_programs(0) - 1)
    def _():
        sout_ref[...] = s_ref[...]


def _rwkv_chunked(r, lw, k2, v, na, b):
    t = r.shape[0]
    spec = _rows(RWKV_BLOCK, B_WIDTH)
    pairs = B_HEADS // 2
    y, s = pl.pallas_call(
        _rwkv_chunk_kernel, grid=(t // RWKV_BLOCK,),
        in_specs=[spec] * 6,
        out_specs=[spec, _full((pairs, LANES, LANES))],
        out_shape=[jax.ShapeDtypeStruct((t, B_WIDTH), F32),
                   jax.ShapeDtypeStruct((pairs, LANES, LANES), F32)],
        scratch_shapes=[pltpu.VMEM((pairs, LANES, LANES), F32)],
        compiler_params=_params("arbitrary"), name="rwkv_chunked",
    )(r, lw, k2, v, na, b)
    s = s.reshape(pairs, 2, B_HDIM, 2, B_HDIM)
    wkv = jnp.stack([s[:, 0, :, 0, :], s[:, 1, :, 1, :]], axis=1).reshape(B_HEADS, B_HDIM, B_HDIM)
    return y, wkv


def _rwkv_step_kernel(s_ref, r_ref, lw_ref, k2_ref, na_ref, b_ref, v_ref, sout_ref, y_ref):
    S = s_ref[...]
    sa = jnp.sum(S * na_ref[...], axis=1, keepdims=True)
    S = S * jnp.exp(lw_ref[...]) + sa * b_ref[...] + v_ref[...] * k2_ref[...]
    sout_ref[...] = S
    y_ref[...] = jnp.sum(S * r_ref[...], axis=1, keepdims=True)


def _rwkv_step(state, r, lw, k2, v, na, b):
    n = state.shape[0]
    keyed = lambda x: x.T.reshape(B_HEADS, 1, B_HDIM, n)
    kspec = pl.BlockSpec((None, 1, B_HDIM, n), lambda h: (h, 0, 0, 0))
    vspec = pl.BlockSpec((None, B_HDIM, 1, n), lambda h: (h, 0, 0, 0))
    sspec = pl.BlockSpec((None, B_HDIM, B_HDIM, n), lambda h: (h, 0, 0, 0))
    s_new, y = pl.pallas_call(
        _rwkv_step_kernel, grid=(B_HEADS,),
        in_specs=[sspec, kspec, kspec, kspec, kspec, kspec, vspec],
        out_specs=[sspec, vspec],
        out_shape=[jax.ShapeDtypeStruct((B_HEADS, B_HDIM, B_HDIM, n), F32),
                   jax.ShapeDtypeStruct((B_HEADS, B_HDIM, 1, n), F32)],
        compiler_params=_params("parallel"), name="rwkv_step",
    )(jnp.transpose(state, (1, 2, 3, 0)), keyed(r), keyed(lw), keyed(k2), keyed(na), keyed(b),
      v.T.reshape(B_HEADS, B_HDIM, 1, n))
    return y.reshape(B_WIDTH, n).T, jnp.transpose(s_new, (3, 0, 1, 2))


def _mla_prompt_kernel(q_ref, k_ref, v_ref, o_ref, sa_ref, sb_ref):
    qi = pl.program_id(1)
    TQ, TK = ATTN_Q_BLOCK, ATTN_KV_BLOCK
    q = (q_ref[0], q_ref[1])
    n_full = (qi * TQ) // TK

    def scores(ks, s_ref):
        start = pl.multiple_of(ks * TK, TK)
        for hh in range(2):
            s_ref[hh] = _dot_nt(k_ref[hh, pl.ds(start, TK), :], q[hh])

    def update(carry, s, ks, width):
        start = pl.multiple_of(ks * TK, TK)
        pr, alpha, m_out = [], [], []
        for hh in range(2):
            m_new = jnp.maximum(carry[hh][0], jnp.max(s[hh], axis=0, keepdims=True))
            alpha.append(jnp.exp2(carry[hh][0] - m_new))
            pr.append(jnp.exp2(s[hh] - m_new).astype(BF16))
            m_out.append(m_new)
        pv = [jnp.dot(v_ref[hh, :, pl.ds(start, width)], pr[hh], preferred_element_type=F32)
              for hh in range(2)]
        return tuple((m_out[hh], carry[hh][1] * alpha[hh] + pv[hh]) for hh in range(2))

    def full_block(carry, ks, s_ref, next_ref):
        scores(ks + 1, next_ref)
        return update(carry, [s_ref[hh] for hh in range(2)], ks, TK)

    def body(j, carry):
        carry = full_block(carry, 2 * j, sa_ref, sb_ref)
        return full_block(carry, 2 * j + 1, sb_ref, sa_ref)

    scores(0, sa_ref)
    init = tuple((jnp.full((1, TQ), -jnp.inf, F32), jnp.zeros((QK_PAD, TQ), F32)) for _ in range(2))
    carry = lax.fori_loop(0, n_full // 2, body, init)

    def finish(widths, odd):
        cr, s_ref = carry, sa_ref
        if odd:
            cr, s_ref = full_block(cr, n_full - 1, sa_ref, sb_ref), sb_ref
        width = widths * TQ
        key = lax.broadcasted_iota(jnp.int32, (width, TQ), 0)
        qry = lax.broadcasted_iota(jnp.int32, (width, TQ), 1)
        keep = key <= qry + (widths - 1) * TQ
        s = [jnp.where(keep, s_ref[hh, :width, :], -jnp.inf) for hh in range(2)]
        (_, a0), (_, a1) = update(cr, s, n_full, width)
        o0 = a0[:A_VDIM] / a0[A_VDIM:A_VDIM + 1]
        o1 = a1[:A_VDIM] / a1[A_VDIM:A_VDIM + 1]
        o_ref[...] = jnp.concatenate([o0, o1], axis=0).T.astype(o_ref.dtype)

    for widths in range(1, TK // TQ + 1):
        for odd in (False, True):
            here = jnp.logical_and(qi * TQ - n_full * TK == (widths - 1) * TQ, (n_full % 2 == 1) == odd)
            pl.when(here)(functools.partial(finish, widths, odd))


def _mla_prompt(q, k, v):
    t = q.shape[1]
    TQ, TK = ATTN_Q_BLOCK, ATTN_KV_BLOCK
    assert TK % TQ == 0 and t % TK == 0
    k_spec = pl.BlockSpec((2, t, QK_PAD), lambda p, i: (p, 0, 0), pipeline_mode=pl.Buffered(1))
    v_spec = pl.BlockSpec((2, QK_PAD, t), lambda p, i: (p, 0, 0), pipeline_mode=pl.Buffered(1))
    return pl.pallas_call(
        _mla_prompt_kernel, grid=(A_HEADS // 2, t // TQ),
        in_specs=[pl.BlockSpec((2, TQ, QK_PAD), lambda p, i: (p, i, 0)), k_spec, v_spec],
        out_specs=pl.BlockSpec((TQ, LANES), lambda p, i: (i, p)),
        out_shape=jax.ShapeDtypeStruct((t, A_HEADS * A_VDIM), BF16),
        scratch_shapes=[pltpu.VMEM((2, TK, TQ), F32), pltpu.VMEM((2, TK, TQ), F32)],
        compiler_params=_params("parallel", "arbitrary"), name="mla_prompt",
    )(q, k, v)


def _mla_decode_kernel(layer, pt_ref, qabs_ref, q_ref, cnew_ref, knew_ref, wuv_ref, ckv_hbm, kpe_hbm,
                       o_ref, cbuf, kbuf, kcat, pcat, csem, ksem):
    P = DECODE_PAGES
    n = pl.program_id(0)
    n_steps = pt_ref.shape[1] // P
    page = cbuf.shape[2]

    def page_copies(pid, slot, i):
        return (pltpu.make_async_copy(ckv_hbm.at[layer, pid], cbuf.at[slot, i], csem.at[slot]),
                pltpu.make_async_copy(kpe_hbm.at[layer, pid], kbuf.at[slot, i], ksem.at[slot]))

    def start(req, step, slot):
        for i in range(P):
            for cp in page_copies(pt_ref[req, step * P + i], slot, i):
                cp.start()

    def wait(slot):
        for i in range(P):
            for cp in page_copies(0, slot, i):
                cp.wait()

    ahead = DECODE_SLOTS - 1

    @pl.when(n == 0)
    def _():
        for step in range(ahead):
            start(0, step, step % DECODE_SLOTS)

    qa = qabs_ref[...]
    qpe = q_ref[:, A_NOPE:A_NOPE + A_ROPE]
    m = jnp.full((A_HEADS, 1), -jnp.inf, F32)
    l = jnp.zeros((A_HEADS, 1), F32)
    acc = jnp.zeros((A_HEADS, KV_RANK), F32)
    pending = None
    for step in range(n_steps):
        fetch = step + ahead
        if fetch < n_steps:
            start(n, fetch, fetch % DECODE_SLOTS)
        else:
            @pl.when(n + 1 < pl.num_programs(0))
            def _():
                start(n + 1, fetch - n_steps, fetch % DECODE_SLOTS)
        wait(step % DECODE_SLOTS)
        slot = step % 2
        for i in range(P):
            kcat[slot, i * page:(i + 1) * page, :] = cbuf[step % DECODE_SLOTS, i].astype(BF16)
            pcat[slot, :, i * page:(i + 1) * page] = kbuf[step % DECODE_SLOTS, i].astype(BF16)
        s = _dot_nt(qa, kcat[slot]) + jnp.dot(qpe, pcat[slot], preferred_element_type=F32)
        if pending is not None:
            pr_prev, alpha_prev, slot_prev = pending
            acc = acc * alpha_prev + jnp.dot(pr_prev, kcat[slot_prev], preferred_element_type=F32)
        m_new = jnp.maximum(m, jnp.max(s, axis=-1, keepdims=True))
        alpha = jnp.exp2(m - m_new)
        pr = jnp.exp2(s - m_new)
        l = l * alpha + jnp.sum(pr, axis=-1, keepdims=True)
        m = m_new
        pending = (pr.astype(BF16), alpha, slot)
    pr_prev, alpha_prev, slot_prev = pending
    acc = acc * alpha_prev + jnp.dot(pr_prev, kcat[slot_prev], preferred_element_type=F32)

    cnew = cnew_ref[...]
    s_self = (jnp.sum(qa * cnew, axis=-1, keepdims=True)
              + jnp.sum(qpe.astype(F32) * knew_ref[...], axis=-1, keepdims=True))
    m_fin = jnp.maximum(m, s_self)
    al = jnp.exp2(m - m_fin)
    p_self = jnp.exp2(s_self - m_fin)
    o_lat = (acc * al + p_self * cnew) / (l * al + p_self)
    res = _dot(o_lat, wuv_ref[...])
    hrow = lax.broadcasted_iota(jnp.int32, res.shape, 0)
    hcol = lax.broadcasted_iota(jnp.int32, res.shape, 1) // A_VDIM
    o_ref[...] = jnp.sum(jnp.where(hrow == hcol, res, 0.0), axis=0, keepdims=True).astype(o_ref.dtype)


def _mla_decode(page_table, qabs, q, ckv_new, kpe_new, wuv, cache_ckv, cache_kpe_t, layer):
    n, n_pages = page_table.shape
    P = DECODE_PAGES
    page = cache_ckv.shape[2]
    slots = DECODE_SLOTS
    assert (n_pages // P) % slots == 0, "buffer slots rotate per step and must line up across requests"
    req = lambda shape: pl.BlockSpec((None,) + shape, lambda b, pt: (b, 0, 0))
    hbm = pl.BlockSpec(memory_space=pl.ANY)
    grid_spec = pltpu.PrefetchScalarGridSpec(
        num_scalar_prefetch=1, grid=(n,),
        in_specs=[req((A_HEADS, KV_RANK)), req((A_HEADS, QK_PAD)), req((1, KV_RANK)), req((1, A_ROPE)),
                  pl.BlockSpec(wuv.shape, lambda b, pt: (0, 0)), hbm, hbm],
        out_specs=req((1, A_HEADS * A_VDIM)),
        scratch_shapes=[pltpu.VMEM((slots, P, page, KV_RANK), F32), pltpu.VMEM((slots, P, A_ROPE, page), F32),
                        pltpu.VMEM((2, P * page, KV_RANK), BF16), pltpu.VMEM((2, A_ROPE, P * page), BF16),
                        pltpu.SemaphoreType.DMA((slots,)), pltpu.SemaphoreType.DMA((slots,))])
    out = pl.pallas_call(
        functools.partial(_mla_decode_kernel, layer), grid_spec=grid_spec,
        out_shape=jax.ShapeDtypeStruct((n, 1, A_HEADS * A_VDIM), BF16),
        compiler_params=_params("arbitrary"), name="mla_decode",
    )(page_table, qabs, q, ckv_new[:, None, :], kpe_new[:, None, :], wuv, cache_ckv, cache_kpe_t)
    return out[:, 0, :]


def _mix_out_kernel(x_ref, oa_ref, y_ref, r_ref, k2_ref, v_ref, g_ref, gpre_ref, wga_ref, wgb_ref,
                    rk_ref, lnw_ref, lnb_ref, e_ref, wpa_ref, wpb_ref, wout_ref, gpost_ref, gmem_ref,
                    wmq_ref, x1_ref, qm_ref):
    x = x_ref[...]
    h = _rms(x, gpre_ref[...]).astype(BF16)
    gate_a = _sigmoid(jnp.dot(h, wga_ref[...], preferred_element_type=F32))
    gate_b = _sigmoid(jnp.dot(h, wgb_ref[...], preferred_element_type=F32))
    e = e_ref[...]
    y = y_ref[...]
    inv = 1.0 / B_HDIM
    d = y - _seg_sum(y, e) * inv
    var = _seg_sum(d * d, e) * inv
    yn = d * lax.rsqrt(var + GN_EPS) * lnw_ref[...] + lnb_ref[...]
    v = v_ref[...]
    bonus = _seg_sum(r_ref[...] * k2_ref[...] * rk_ref[...], e) * v
    ob = (yn + bonus) * g_ref[...]
    merged = gate_a * _dot(oa_ref[...], wpa_ref[...]) + gate_b * _dot(ob, wpb_ref[...])
    x1 = x + _rms(_dot(merged, wout_ref[...]), gpost_ref[...])
    x1_ref[...] = x1
    qm = _dot(_rms(x1, gmem_ref[...]), wmq_ref[...]) * ((x.shape[-1] // X_HEADS) ** -0.5)
    qm_ref[...] = qm.astype(BF16)


def _mix_out(x, oa, y, r, k2, v, g, *weights):
    m, d = x.shape
    tm = min(ROW_TILE, m)
    wide = _rows(tm, d)
    half = _rows(tm, B_WIDTH)
    return pl.pallas_call(
        _mix_out_kernel, grid=(m // tm,),
        in_specs=[wide] + [half] * 6 + [_full(w.shape) for w in weights],
        out_specs=[wide, wide],
        out_shape=[jax.ShapeDtypeStruct((m, d), F32), jax.ShapeDtypeStruct((m, d), BF16)],
        compiler_params=_params("parallel"), name="mix_out",
    )(x, oa, y, r, k2, v, g, *weights)


def _mem_kv_kernel(mem_ref, g_ref, wk_ref, wv_ref, k_ref, v_ref):
    h = _rms(mem_ref[...], g_ref[...]).astype(BF16)
    k_ref[...] = jnp.dot(h, wk_ref[...], preferred_element_type=F32)
    v_ref[...] = jnp.dot(h, wv_ref[...], preferred_element_type=F32)


def _mem_kv(mem, g, wk, wv):
    m, d = mem.shape
    o = jax.ShapeDtypeStruct((m, d), F32)
    return pl.pallas_call(
        _mem_kv_kernel, grid=(1,),
        in_specs=[_full(mem.shape), _full(g.shape), _full(wk.shape), _full(wv.shape)],
        out_specs=[_full((m, d))] * 2, out_shape=[o, o],
        compiler_params=_params("arbitrary"), name="mem_kv",
    )(mem, g, wk, wv)


def _softmax_rows(s):
    pr = jnp.exp(s - jnp.max(s, axis=-1, keepdims=True))
    return pr / jnp.sum(pr, axis=-1, keepdims=True)


def _mem_attn_shared_kernel(q_ref, k_ref, v_ref, o_ref):
    hd = q_ref.shape[-1] // X_HEADS
    heads = [slice(hh * hd, (hh + 1) * hd) for hh in range(X_HEADS)]
    s = [_dot_nt(q_ref[:, sl], k_ref[:, sl]) for sl in heads]
    pr = [_softmax_rows(sh) for sh in s]
    o = [_dot(pr[hh], v_ref[:, heads[hh]]) for hh in range(X_HEADS)]
    for hh in range(X_HEADS):
        o_ref[:, heads[hh]] = o[hh].astype(o_ref.dtype)


def _mem_attn_shared(q, mk, mv):
    m, d = q.shape
    tm = min(ROW_TILE, m)
    return pl.pallas_call(
        _mem_attn_shared_kernel, grid=(m // tm,),
        in_specs=[_rows(tm, d), _full(mk.shape), _full(mv.shape)],
        out_specs=_rows(tm, d), out_shape=jax.ShapeDtypeStruct((m, d), BF16),
        compiler_params=_params("parallel"), name="mem_attn_shared",
    )(q, mk, mv)


def _mem_attn_rows_kernel(q_ref, k_ref, v_ref, o_ref):
    for i in range(q_ref.shape[0]):
        prod = k_ref[i] * q_ref[i][None]
        s = jnp.sum(jnp.sum(prod, axis=1), axis=-1, keepdims=True)
        pr = jnp.exp(s - jnp.max(s, axis=0, keepdims=True))
        w = pr / jnp.sum(pr, axis=0, keepdims=True)
        o_ref[i] = jnp.sum(w[:, None] * v_ref[i], axis=0)


def _mem_attn_rows(q, mk, mv):
    n, mem, heads, hd = mk.shape
    nb = 4
    tiles = hd // LANES
    tiled = lambda x, lead: jnp.swapaxes(x.reshape(lead + (heads, tiles, LANES)), -2, -3)
    qspec = pl.BlockSpec((nb, tiles, heads, LANES), lambda i: (i, 0, 0, 0))
    kspec = pl.BlockSpec((nb, mem, tiles, heads, LANES), lambda i: (i, 0, 0, 0, 0))
    out = pl.pallas_call(
        _mem_attn_rows_kernel, grid=(n // nb,),
        in_specs=[qspec, kspec, kspec], out_specs=qspec,
        out_shape=jax.ShapeDtypeStruct((n, tiles, heads, LANES), F32),
        compiler_params=_params("parallel"), name="mem_attn_rows",
    )(tiled(q.astype(F32), (n,)), tiled(mk, (n, mem)), tiled(mv, (n, mem)))
    return jnp.swapaxes(out, 1, 2).reshape(n, heads * hd).astype(BF16)


def _tail_kernel(x1_ref, om_ref, wmo_ref, gpm_ref, gmlp_ref, wup_ref, wdn_ref, gpost_ref, y_ref):
    x2 = x1_ref[...] + _rms(jnp.dot(om_ref[...], wmo_ref[...], preferred_element_type=F32), gpm_ref[...])
    h = _rms(x2, gmlp_ref[...]).astype(BF16)
    u = jnp.maximum(jnp.dot(h, wup_ref[...], preferred_element_type=F32), 0.0)
    ff = jnp.dot((u * u).astype(BF16), wdn_ref[...], preferred_element_type=F32)
    y_ref[...] = x2 + _rms(ff, gpost_ref[...])


def _tail(x1, om, *weights):
    m, d = x1.shape
    tm = min(ROW_TILE, m)
    return pl.pallas_call(
        _tail_kernel, grid=(m // tm,),
        in_specs=[_rows(tm, d), _rows(tm, d)] + [_full(w.shape) for w in weights],
        out_specs=_rows(tm, d), out_shape=jax.ShapeDtypeStruct((m, d), F32),
        compiler_params=_params("parallel"), name="tail",
    )(x1, om, *weights)


def _rot_cols(w):
    half = w.shape[-1] // 2
    return jnp.concatenate([-w[..., half:], w[..., :half]], axis=-1)


def _rope_tables(pos):
    inv = ROPE_THETA ** (-jnp.arange(0, A_ROPE, 2, dtype=F32) / A_ROPE)
    ang = pos.astype(F32)[:, None] * inv[None, :]
    z_lo = jnp.zeros((pos.shape[0], A_NOPE), F32)
    z_hi = jnp.zeros((pos.shape[0], QK_PAD - A_NOPE - A_ROPE), F32)
    cos, sin = jnp.cos(ang), jnp.sin(ang)
    return (jnp.concatenate([z_lo, cos, cos, z_hi], axis=1),
            jnp.concatenate([z_lo, sin, sin, z_hi], axis=1))


def _prep_layer(l, w_in, w_uq, w_uk, w_uv, rw_decay_up, rw_a_up, rw_g_up):
    d = w_in.shape[1]
    wi = w_in[l]
    o_rw = Q_RANK + KV_RANK + A_ROPE
    o_ga = o_rw + RW_COLS
    w_kpe = wi[:, Q_RANK + KV_RANK:o_rw]
    z = lambda n: jnp.zeros((d, n), F32)
    pad_hi = QK_PAD - A_NOPE - A_ROPE
    wlat = jnp.concatenate([wi[:, :Q_RANK + KV_RANK], z(A_NOPE), w_kpe, z(pad_hi),
                            z(A_NOPE), _rot_cols(w_kpe), z(pad_hi)], axis=1).astype(BF16)
    wrw = jnp.pad(wi[:, o_rw:o_ga], ((0, 0), (0, RW_PAD - RW_COLS))).astype(BF16)
    wga = wi[:, o_ga:o_ga + d].astype(BF16)
    wgb = wi[:, o_ga + d:].astype(BF16)
    uq = w_uq[l].reshape(Q_RANK, A_HEADS, A_NOPE + A_ROPE)
    nope, pe = uq[..., :A_NOPE], uq[..., A_NOPE:]
    zq = lambda n: jnp.zeros((Q_RANK, A_HEADS, n), F32)
    wqa = jnp.concatenate([nope, pe, zq(pad_hi)], axis=-1).reshape(Q_RANK, -1).astype(BF16)
    wqb = jnp.concatenate([zq(A_NOPE), _rot_cols(pe), zq(pad_hi)], axis=-1).reshape(Q_RANK, -1).astype(BF16)
    pad_cols = lambda w: jnp.pad(w, ((0, 0), (0, 0), (0, QK_PAD - w.shape[-1]))).reshape(KV_RANK, -1).astype(BF16)
    wk_cols = pad_cols(w_uk[l])
    wv_cols = pad_cols(w_uv[l])
    wk_rows = jnp.pad(jnp.transpose(w_uk[l], (1, 2, 0)), ((0, 0), (0, QK_PAD - A_NOPE), (0, 0))).astype(BF16)
    wv = w_uv[l].reshape(KV_RANK, -1).astype(BF16)
    wl = jnp.zeros((LORA_IN, 3 * B_WIDTH), F32)
    wl = wl.at[:DECAY_LORA, :B_WIDTH].set(rw_decay_up[l])
    wl = wl.at[DECAY_LORA:DECAY_LORA + AAA_LORA, B_WIDTH:2 * B_WIDTH].set(rw_a_up[l])
    wl = wl.at[DECAY_LORA + AAA_LORA:, 2 * B_WIDTH:].set(rw_g_up[l])
    return wlat, wrw, wga, wgb, wqa, wqb, wk_cols, wv_cols, wk_rows, wv, wl.astype(BF16)


def kernel(x_prompt, x_sample, mem_prompt, cache_ckv, cache_kpe, state_wkv, state_shift, cache_mem_k, cache_mem_v, page_table, norm_pre_mix, w_in, q_norm, w_uq, kv_norm, w_uk, w_uv, rw_mu, rw_w0, rw_decay_up, rw_a0, rw_a_up, rw_g_up, rw_k_k, rw_k_a, rw_r_k, rw_ln_w, rw_ln_b, w_proj_a, w_proj_b, w_out, norm_post_mix, norm_pre_mem, mem_norm, w_mq, w_mk, w_mv, w_mo, norm_post_mem, norm_pre_mlp, w_ff_up, w_ff_down, norm_post_mlp):
    depth = w_in.shape[0]
    n_p, seq, d = x_prompt.shape
    n_s, dec_seq, _ = x_sample.shape
    assert n_p == 1 and dec_seq == 1, "one prompt sequence and one new token per decode request"
    past_len = page_table.shape[1] * cache_ckv.shape[2]
    ck_p, sk_p = _rope_tables(jnp.arange(seq))
    ck_s, sk_s = _rope_tables(jnp.full((n_s,), past_len))
    seg = jnp.arange(2 * LANES) // B_HDIM
    e = (seg[:, None] == seg[None, :]).astype(BF16)
    cache_kpe_t = jnp.swapaxes(cache_kpe, 2, 3)
    row = lambda p, l: p[l].reshape(1, -1)
    bf = lambda p, l: p[l].astype(BF16)

    y_p = x_prompt.reshape(seq, d)
    y_s = x_sample.reshape(n_s, d)
    outs = [[] for _ in range(10)]
    for l in range(depth):
        wlat, wrw, wga, wgb, wqa, wqb, wk_cols, wv_cols, wk_rows, wv, wl = _prep_layer(
            l, w_in, w_uq, w_uk, w_uv, rw_decay_up, rw_a_up, rw_g_up)
        proj_w = (row(norm_pre_mix, l), wlat, wrw, row(q_norm, l), row(kv_norm, l), wqa, wqb)
        prep_w = (row(rw_mu, l), row(rw_w0, l), row(rw_a0, l), wl, row(rw_k_k, l), row(rw_k_a, l), e)
        mix_w = (row(norm_pre_mix, l), wga, wgb, row(rw_r_k, l), row(rw_ln_w, l), row(rw_ln_b, l), e,
                 bf(w_proj_a, l), bf(w_proj_b, l), bf(w_out, l), row(norm_post_mix, l),
                 row(norm_pre_mem, l), bf(w_mq, l))
        tail_w = (bf(w_mo, l), row(norm_post_mem, l), row(norm_pre_mlp, l), bf(w_ff_up, l),
                  bf(w_ff_down, l), row(norm_post_mlp, l))

        q, ckv_p, kpe_p, rw_p, k, v, r_, lw_, k2_, v_, na_, b_, g_ = _proj_in(
            y_p, *proj_w, ck_p, sk_p, wk_cols, wv_cols, jnp.zeros((1, RW_COLS), F32), *prep_w, prompt=True)
        oa_p = _mla_prompt(q, k, v)
        yb_p, wkv_p = _rwkv_chunked(r_, lw_, k2_, v_, na_, b_)
        x1_p, qm_p = _mix_out(y_p, oa_p, yb_p, r_, k2_, v_, g_, *mix_w)
        mk_p, mv_p = _mem_kv(mem_prompt.reshape(-1, d), row(mem_norm, l), bf(w_mk, l), bf(w_mv, l))
        om_p = _mem_attn_shared(qm_p, mk_p, mv_p)
        y_p = _tail(x1_p, om_p, *tail_w)

        q, ckv_s, kpe_s, rw_s, qabs, r_, lw_, k2_, v_, na_, b_, g_ = _proj_in(
            y_s, *proj_w, ck_s, sk_s, wk_rows, wv, state_shift[l], *prep_w, prompt=False)
        oa_s = _mla_decode(page_table, jnp.transpose(qabs, (1, 0, 2)), jnp.transpose(q, (1, 0, 2)),
                           ckv_s, kpe_s, wv, cache_ckv, cache_kpe_t, l)
        yb_s, wkv_s = _rwkv_step(state_wkv[l], r_, lw_, k2_, v_, na_, b_)
        x1_s, qm_s = _mix_out(y_s, oa_s, yb_s, r_, k2_, v_, g_, *mix_w)
        om_s = _mem_attn_rows(qm_s, cache_mem_k[l], cache_mem_v[l])
        y_s = _tail(x1_s, om_s, *tail_w)

        mem_heads = (n_p, -1, X_HEADS, d // X_HEADS)
        layer_outs = (ckv_p.reshape(n_p, seq, KV_RANK), kpe_p.reshape(n_p, seq, A_ROPE),
                      wkv_p[None], rw_p[-1:], mk_p.reshape(mem_heads), mv_p.reshape(mem_heads),
                      ckv_s.reshape(n_s, 1, KV_RANK), kpe_s.reshape(n_s, 1, A_ROPE), wkv_s, rw_s)
        for lst, val in zip(outs, layer_outs):
            lst.append(val)
    return (y_p.reshape(n_p, seq, d), y_s.reshape(n_s, 1, d)) + tuple(jnp.stack(o) for o in outs)
```

```python
import functools
import math

import jax
import jax.numpy as jnp
from jax import lax
from jax.experimental import pallas as pl
from jax.experimental.pallas import tpu as pltpu

F32 = jnp.float32
BF16 = jnp.bfloat16

A_HEADS = 8
A_NOPE = 64
A_ROPE = 32
A_VDIM = 64
Q_RANK = 384
KV_RANK = 256
ROPE_THETA = 10000.0
B_HEADS = 8
B_HDIM = 64
B_WIDTH = B_HEADS * B_HDIM
DECAY_LORA = 64
AAA_LORA = 64
GATE_LORA = 160
LORA_IN = DECAY_LORA + AAA_LORA + GATE_LORA
GN_EPS = 64e-5
X_HEADS = 4
NORM_EPS = 1e-6
RW_COLS = 3 * B_WIDTH + LORA_IN

LANES = 128
SUBLANES = 8
QK_PAD = 128
RW_PAD = 15 * LANES
RWKV_CHUNK = 64
RWKV_BLOCK = 256
ATTN_Q_BLOCK = 1024
ATTN_KV_BLOCK = 1024
DECODE_PAGES = 32
DECODE_SLOTS = 4
ROW_TILE = 512
VMEM_LIMIT = 48 * 1024 * 1024
QK_SCALE = (A_NOPE + A_ROPE) ** -0.5 * math.log2(math.e)


def _params(*sem):
    return pltpu.CompilerParams(dimension_semantics=sem, vmem_limit_bytes=VMEM_LIMIT)


def _full(shape):
    zeros = (0,) * len(shape)
    return pl.BlockSpec(shape, lambda *_: zeros)


def _const(shape):
    zeros = (0,) * len(shape)
    return pl.BlockSpec(shape, lambda *_: zeros, pipeline_mode=pl.Buffered(1))


def _rows(tm, width):
    return pl.BlockSpec((tm, width), lambda i: (i, 0))


def _rms(x, g):
    return x * lax.rsqrt(jnp.mean(x * x, axis=-1, keepdims=True) + NORM_EPS) * g


def _dot(a, b):
    return jnp.dot(a.astype(BF16), b.astype(BF16), preferred_element_type=F32)


def _dot_nt(a, b):
    return lax.dot_general(a.astype(BF16), b.astype(BF16), (((1,), (1,)), ((), ())),
                           preferred_element_type=F32)


def _dot_tn(a, b):
    return lax.dot_general(a.astype(BF16), b.astype(BF16), (((0,), (0,)), ((), ())),
                           preferred_element_type=F32)


def _sigmoid(x):
    return 1.0 / (1.0 + jnp.exp(-x))


def _seg_sum(x, e):
    hi = x.astype(BF16)
    lo = (x - hi.astype(F32)).astype(BF16)
    w = e.shape[0]
    parts = [jnp.dot(hi[:, c:c + w], e, preferred_element_type=F32)
             + jnp.dot(lo[:, c:c + w], e, preferred_element_type=F32) for c in range(0, x.shape[1], w)]
    return jnp.concatenate(parts, axis=1)


def _rwkv_inputs(rw, shifted, mu, w0, a0, wl, k_k, k_a, e):
    xs = rw + (shifted - rw) * mu
    r = xs[:, :B_WIDTH]
    k = xs[:, B_WIDTH:2 * B_WIDTH]
    v = xs[:, 2 * B_WIDTH:3 * B_WIDTH]
    lo = xs[:, 3 * B_WIDTH:]
    lane = lax.broadcasted_iota(jnp.int32, lo.shape, 1)
    t = jnp.where(lane < DECAY_LORA, jnp.tanh(lo),
                  jnp.where(lane < DECAY_LORA + AAA_LORA, lo, _sigmoid(lo)))
    l3 = jnp.dot(t.astype(BF16), wl, preferred_element_type=F32)
    z = -(w0 + l3[:, :B_WIDTH])
    softplus = jnp.maximum(z, 0.0) + jnp.log(1.0 + jnp.exp(-jnp.abs(z)))
    w = -softplus - 0.5
    a = _sigmoid(a0 + l3[:, B_WIDTH:2 * B_WIDTH])
    kkr = k * k_k
    kk = kkr / jnp.maximum(jnp.sqrt(_seg_sum(kkr * kkr, e)), 1e-12)
    lw = -jnp.exp(w)
    return r, lw, k * (1.0 + (a - 1.0) * k_a), v, -kk, kk * a, l3[:, 2 * B_WIDTH:]


def _proj_in_kernel(prompt, x_ref, g_ref, wlat_ref, wrw_ref, qn_ref, kvn_ref, wqa_ref, wqb_ref,
                    ck_ref, sk_ref, wk_ref, wv_ref, first_ref, mu_ref, w0_ref, a0_ref, wl_ref, kk_ref,
                    ka_ref, e_ref, *rest):
    if prompt:
        q_ref, ckv_ref, kpe_ref, rw_ref, k_ref, v_ref = rest[:6]
        prev_ref = rest[-1]
    else:
        q_ref, ckv_ref, kpe_ref, rw_ref, qabs_ref = rest[:5]
    rwkv_refs = rest[6:13] if prompt else rest[5:12]
    if prompt:
        @pl.when(pl.program_id(0) == 0)
        def _():
            prev_ref[...] = jnp.broadcast_to(first_ref[...], prev_ref.shape)

    h = _rms(x_ref[...], g_ref[...]).astype(BF16)
    lat = jnp.dot(h, wlat_ref[...], preferred_element_type=F32)
    cq = lat[:, :Q_RANK]
    ckv = _rms(lat[:, Q_RANK:Q_RANK + KV_RANK], kvn_ref[...])
    ck = ck_ref[...].T
    sk = sk_ref[...].T
    kpe = lat[:, 640:768] * ck + lat[:, 768:896] * sk
    ckv_ref[...] = ckv
    kpe_ref[...] = kpe[:, A_NOPE:A_NOPE + A_ROPE]
    rw = jnp.dot(h, wrw_ref[...], preferred_element_type=F32)[:, :RW_COLS]
    if prompt:
        ridx = lax.broadcasted_iota(jnp.int32, rw.shape, 0)
        shifted = jnp.where(ridx == 0, prev_ref[SUBLANES - 1:SUBLANES, :], pltpu.roll(rw, 1, axis=0))
        last_rows = rw[rw.shape[0] - SUBLANES:, :]
        prev_ref[...] = last_rows
        rw_ref[...] = last_rows
    else:
        shifted = first_ref[...]
        rw_ref[...] = rw
    vals = _rwkv_inputs(rw, shifted, mu_ref[...], w0_ref[...], a0_ref[...], wl_ref[...], kk_ref[...],
                        ka_ref[...], e_ref[...])
    for ref, val in zip(rwkv_refs, vals):
        ref[...] = val
    qn = _rms(cq, qn_ref[...]).astype(BF16)
    qa = jnp.dot(qn, wqa_ref[...], preferred_element_type=F32)
    qb = jnp.dot(qn, wqb_ref[...], preferred_element_type=F32)
    lane = lax.broadcasted_iota(jnp.int32, ck.shape, 1)
    cq_t = QK_SCALE * (ck + jnp.where(lane < A_NOPE, 1.0, 0.0))
    sq_t = QK_SCALE * sk
    ckv_b = ckv.astype(BF16)
    if prompt:
        kn = jnp.dot(ckv_b, wk_ref[...], preferred_element_type=F32)
        vn = jnp.dot(ckv_b, wv_ref[...], preferred_element_type=F32)
        ones_col = jnp.where(lane == A_VDIM, 1.0, 0.0)
    for hh in range(A_HEADS):
        sl = slice(hh * QK_PAD, (hh + 1) * QK_PAD)
        qh = (qa[:, sl] * cq_t + qb[:, sl] * sq_t).astype(BF16)
        q_ref[hh] = qh
        if prompt:
            k_ref[hh] = (kn[:, sl] + kpe).astype(BF16)
            v_ref[hh] = (vn[:, sl] + ones_col).T.astype(BF16)
        else:
            qabs_ref[hh] = jnp.dot(qh, wk_ref[hh], preferred_element_type=F32)


def _proj_in(x, g, wlat, wrw, qn, kvn, wqa, wqb, ck, sk, wk, wv, first, *rwkv_w, prompt):
    m, d = x.shape
    tm = min(ROW_TILE, m)
    heads_out = lambda w, dt: jax.ShapeDtypeStruct((A_HEADS, m, w), dt)
    heads_spec = lambda w: pl.BlockSpec((A_HEADS, tm, w), lambda i: (0, i, 0))
    rw_rows = SUBLANES if prompt else m
    out_shape = [heads_out(QK_PAD, BF16), jax.ShapeDtypeStruct((m, KV_RANK), F32),
                 jax.ShapeDtypeStruct((m, A_ROPE), F32), jax.ShapeDtypeStruct((rw_rows, RW_COLS), F32)]
    out_specs = [heads_spec(QK_PAD), _rows(tm, KV_RANK), _rows(tm, A_ROPE),
                 _full((SUBLANES, RW_COLS)) if prompt else _rows(tm, RW_COLS)]
    if prompt:
        out_shape += [heads_out(QK_PAD, BF16), jax.ShapeDtypeStruct((A_HEADS, QK_PAD, m), BF16)]
        out_specs += [heads_spec(QK_PAD), pl.BlockSpec((A_HEADS, QK_PAD, tm), lambda i: (0, 0, i))]
    else:
        out_shape += [heads_out(KV_RANK, F32)]
        out_specs += [heads_spec(KV_RANK)]
    out_shape += [jax.ShapeDtypeStruct((m, B_WIDTH), F32)] * 7
    out_specs += [_rows(tm, B_WIDTH)] * 7
    first_spec = _full(first.shape) if prompt else _rows(tm, RW_COLS)
    table_spec = pl.BlockSpec((LANES, tm), lambda i: (0, i))
    return pl.pallas_call(
        functools.partial(_proj_in_kernel, prompt),
        grid=(m // tm,),
        in_specs=[_rows(tm, d), _const(g.shape), _const(wlat.shape), _const(wrw.shape), _const(qn.shape),
                  _const(kvn.shape), _const(wqa.shape), _const(wqb.shape), table_spec,
                  table_spec, _const(wk.shape), _const(wv.shape), first_spec]
                 + [_const(w.shape) for w in rwkv_w],
        out_specs=out_specs, out_shape=out_shape,
        scratch_shapes=[pltpu.VMEM((SUBLANES, RW_COLS), F32)] if prompt else [],
        compiler_params=_params("arbitrary"), name="proj_in",
    )(x, g, wlat, wrw, qn, kvn, wqa, wqb, ck, sk, wk, wv, first, *rwkv_w)


def _rwkv_chunk_kernel(r_ref, lw_ref, k2_ref, v_ref, na_ref, b_ref, y_ref, sout_ref, s_ref):
    C = RWKV_CHUNK
    n_chunks = r_ref.shape[0] // C
    pairs = B_HEADS // 2

    @pl.when(pl.program_id(0) == 0)
    def _():
        s_ref[...] = jnp.zeros_like(s_ref)

    row = lax.broadcasted_iota(jnp.int32, (2 * C, 2 * C), 0)
    col = lax.broadcasted_iota(jnp.int32, (2 * C, 2 * C), 1)
    tok_r = jnp.bitwise_and(row, C - 1)
    tok_c = jnp.bitwise_and(col, C - 1)
    strict = tok_c < tok_r
    incl = tok_c <= tok_r
    eye = jnp.where(row == col, 1.0, 0.0)
    crow = lax.broadcasted_iota(jnp.int32, (C, C), 0)
    ccol = lax.broadcasted_iota(jnp.int32, (C, C), 1)
    cum = jnp.where(ccol <= crow, 1.0, 0.0).astype(BF16)
    head0 = lax.broadcasted_iota(jnp.int32, (C, LANES), 1) < B_HDIM

    def stack(x):
        return jnp.concatenate([jnp.where(head0, x, 0.0), jnp.where(head0, 0.0, x)], axis=0)

    units = [(ci, p) for ci in range(n_chunks) for p in range(pairs)]

    G_all = []
    for ci in range(n_chunks):
        lw = lw_ref[ci * C:(ci + 1) * C, :]
        l_hi = lw.astype(BF16)
        rem = lw - l_hi.astype(F32)
        l_mid = rem.astype(BF16)
        l_lo = (rem - l_mid.astype(F32)).astype(BF16)
        g3 = jnp.dot(cum, jnp.concatenate([l_hi, l_mid, l_lo], axis=1), preferred_element_type=F32)
        G_all.append(g3[:, :B_WIDTH] + g3[:, B_WIDTH:2 * B_WIDTH] + g3[:, 2 * B_WIDTH:])

    ops = {}
    for ci, p in units:
        rows = slice(ci * C, (ci + 1) * C)
        sl = slice(p * LANES, (p + 1) * LANES)
        lw = lw_ref[rows, sl]
        G = G_all[ci][:, sl]
        GC = G[C - 1:C, :]
        e_pos = jnp.exp(G)
        e_neg = jnp.exp(-G)
        e_prev = jnp.exp(G - lw)
        e_tail = jnp.exp(GC - G)
        r, k2, v = r_ref[rows, sl], k2_ref[rows, sl], v_ref[rows, sl]
        na, b = na_ref[rows, sl], b_ref[rows, sl]
        ops[ci, p] = dict(
            AR=jnp.concatenate([stack(na * e_prev), stack(r * e_pos)], axis=0).astype(BF16),
            BK=jnp.concatenate([stack(b * e_neg), stack(k2 * e_neg)], axis=0).astype(BF16),
            Bh=stack(b * e_tail).astype(BF16), Kh=stack(k2 * e_tail).astype(BF16),
            Vs=stack(v).astype(BF16), decay=jnp.exp(GC))

    AA = {u: _dot_nt(ops[u]["AR"], ops[u]["BK"]) for u in units}
    X, Pw, AakArk, Arb = {}, {}, {}, {}
    for u in units:
        aa = AA[u]
        a_ab = jnp.where(strict, aa[:2 * C, :2 * C], 0.0)
        AakArk[u] = jnp.concatenate([jnp.where(strict, aa[:2 * C, 2 * C:], 0.0),
                                     jnp.where(incl, aa[2 * C:, 2 * C:], 0.0)], axis=0).astype(BF16)
        Arb[u] = jnp.where(incl, aa[2 * C:, :2 * C], 0.0).astype(BF16)
        X[u] = eye + a_ab
        Pw[u] = a_ab
    sq = {u: _dot(Pw[u], Pw[u]) for u in units}
    n = 2
    while 2 * n < C:
        both = {u: _dot(jnp.concatenate([X[u], sq[u]], axis=0), sq[u]) for u in units}
        for u in units:
            X[u] = X[u] + both[u][:2 * C]
            sq[u] = both[u][2 * C:]
        n *= 2
    last = {u: _dot(X[u], sq[u]) for u in units}
    for u in units:
        X[u] = (X[u] + last[u]).astype(BF16)
    AV = {u: _dot(AakArk[u], ops[u]["Vs"]) for u in units}
    VK = {u: _dot_tn(ops[u]["Vs"], ops[u]["Kh"]) for u in units}

    S = [s_ref[p] for p in range(pairs)]
    for ci in range(n_chunks):
        SS = [_dot_nt(ops[ci, p]["AR"], S[p]) for p in range(pairs)]
        U = [_dot(X[ci, p], SS[p][:2 * C] + AV[ci, p][:2 * C]) for p in range(pairs)]
        YU = [_dot(Arb[ci, p], U[p]) for p in range(pairs)]
        UB = [_dot_tn(U[p], ops[ci, p]["Bh"]) for p in range(pairs)]
        for p in range(pairs):
            Y = SS[p][2 * C:] + YU[p] + AV[ci, p][2 * C:]
            y_ref[ci * C:(ci + 1) * C, p * LANES:(p + 1) * LANES] = Y[:C] + Y[C:]
            S[p] = S[p] * ops[ci, p]["decay"] + UB[p] + VK[ci, p]
    for p in range(pairs):
        s_ref[p] = S[p]

    @pl.when(pl.program_id(0) == pl.num_programs(0) - 1)
    def _():
        sout_ref[...] = s_ref[...]


def _rwkv_chunked(r, lw, k2, v, na, b):
    t = r.shape[0]
    spec = _rows(RWKV_BLOCK, B_WIDTH)
    pairs = B_HEADS // 2
    y, s = pl.pallas_call(
        _rwkv_chunk_kernel, grid=(t // RWKV_BLOCK,),
        in_specs=[spec] * 6,
        out_specs=[spec, _full((pairs, LANES, LANES))],
        out_shape=[jax.ShapeDtypeStruct((t, B_WIDTH), F32),
                   jax.ShapeDtypeStruct((pairs, LANES, LANES), F32)],
        scratch_shapes=[pltpu.VMEM((pairs, LANES, LANES), F32)],
        compiler_params=_params("arbitrary"), name="rwkv_chunked",
    )(r, lw, k2, v, na, b)
    s = s.reshape(pairs, 2, B_HDIM, 2, B_HDIM)
    wkv = jnp.stack([s[:, 0, :, 0, :], s[:, 1, :, 1, :]], axis=1).reshape(B_HEADS, B_HDIM, B_HDIM)
    return y, wkv


def _rwkv_step_kernel(s_ref, r_ref, lw_ref, k2_ref, na_ref, b_ref, v_ref, sout_ref, y_ref):
    S = s_ref[...]
    sa = jnp.sum(S * na_ref[...], axis=1, keepdims=True)
    S = S * jnp.exp(lw_ref[...]) + sa * b_ref[...] + v_ref[...] * k2_ref[...]
    sout_ref[...] = S
    y_ref[...] = jnp.sum(S * r_ref[...], axis=1, keepdims=True)


def _rwkv_step(state, r, lw, k2, v, na, b):
    n = state.shape[0]
    keyed = lambda x: x.T.reshape(B_HEADS, 1, B_HDIM, n)
    kspec = pl.BlockSpec((None, 1, B_HDIM, n), lambda h: (h, 0, 0, 0))
    vspec = pl.BlockSpec((None, B_HDIM, 1, n), lambda h: (h, 0, 0, 0))
    sspec = pl.BlockSpec((None, B_HDIM, B_HDIM, n), lambda h: (h, 0, 0, 0))
    s_new, y = pl.pallas_call(
        _rwkv_step_kernel, grid=(B_HEADS,),
        in_specs=[sspec, kspec, kspec, kspec, kspec, kspec, vspec],
        out_specs=[sspec, vspec],
        out_shape=[jax.ShapeDtypeStruct((B_HEADS, B_HDIM, B_HDIM, n), F32),
                   jax.ShapeDtypeStruct((B_HEADS, B_HDIM, 1, n), F32)],
        compiler_params=_params("parallel"), name="rwkv_step",
    )(jnp.transpose(state, (1, 2, 3, 0)), keyed(r), keyed(lw), keyed(k2), keyed(na), keyed(b),
      v.T.reshape(B_HEADS, B_HDIM, 1, n))
    return y.reshape(B_WIDTH, n).T, jnp.transpose(s_new, (3, 0, 1, 2))


def _mla_prompt_kernel(q_ref, k_ref, v_ref, o_ref, sa_ref, sb_ref):
    qi = pl.program_id(1)
    TQ, TK = ATTN_Q_BLOCK, ATTN_KV_BLOCK
    q = (q_ref[0], q_ref[1])
    n_full = (qi * TQ) // TK

    def scores(ks, s_ref):
        start = pl.multiple_of(ks * TK, TK)
        for hh in range(2):
            s_ref[hh] = _dot_nt(k_ref[hh, pl.ds(start, TK), :], q[hh])

    def update(carry, s, ks, width):
        start = pl.multiple_of(ks * TK, TK)
        pr, alpha, m_out = [], [], []
        for hh in range(2):
            m_new = jnp.maximum(carry[hh][0], jnp.max(s[hh], axis=0, keepdims=True))
            alpha.append(jnp.exp2(carry[hh][0] - m_new))
            pr.append(jnp.exp2(s[hh] - m_new).astype(BF16))
            m_out.append(m_new)
        pv = [jnp.dot(v_ref[hh, :, pl.ds(start, width)], pr[hh], preferred_element_type=F32)
              for hh in range(2)]
        return tuple((m_out[hh], carry[hh][1] * alpha[hh] + pv[hh]) for hh in range(2))

    def full_block(carry, ks, s_ref, next_ref):
        scores(ks + 1, next_ref)
        return update(carry, [s_ref[hh] for hh in range(2)], ks, TK)

    def body(j, carry):
        carry = full_block(carry, 2 * j, sa_ref, sb_ref)
        return full_block(carry, 2 * j + 1, sb_ref, sa_ref)

    scores(0, sa_ref)
    init = tuple((jnp.full((1, TQ), -jnp.inf, F32), jnp.zeros((QK_PAD, TQ), F32)) for _ in range(2))
    carry = lax.fori_loop(0, n_full // 2, body, init)

    def finish(widths, odd):
        cr, s_ref = carry, sa_ref
        if odd:
            cr, s_ref = full_block(cr, n_full - 1, sa_ref, sb_ref), sb_ref
        width = widths * TQ
        key = lax.broadcasted_iota(jnp.int32, (width, TQ), 0)
        qry = lax.broadcasted_iota(jnp.int32, (width, TQ), 1)
        keep = key <= qry + (widths - 1) * TQ
        s = [jnp.where(keep, s_ref[hh, :width, :], -jnp.inf) for hh in range(2)]
        (_, a0), (_, a1) = update(cr, s, n_full, width)
        o0 = a0[:A_VDIM] / a0[A_VDIM:A_VDIM + 1]
        o1 = a1[:A_VDIM] / a1[A_VDIM:A_VDIM + 1]
        o_ref[...] = jnp.concatenate([o0, o1], axis=0).T.astype(o_ref.dtype)

    for widths in range(1, TK // TQ + 1):
        for odd in (False, True):
            here = jnp.logical_and(qi * TQ - n_full * TK == (widths - 1) * TQ, (n_full % 2 == 1) == odd)
            pl.when(here)(functools.partial(finish, widths, odd))


def _mla_prompt(q, k, v):
    t = q.shape[1]
    TQ, TK = ATTN_Q_BLOCK, ATTN_KV_BLOCK
    assert TK % TQ == 0 and t % TK == 0
    k_spec = pl.BlockSpec((2, t, QK_PAD), lambda p, i: (p, 0, 0), pipeline_mode=pl.Buffered(1))
    v_spec = pl.BlockSpec((2, QK_PAD, t), lambda p, i: (p, 0, 0), pipeline_mode=pl.Buffered(1))
    return pl.pallas_call(
        _mla_prompt_kernel, grid=(A_HEADS // 2, t // TQ),
        in_specs=[pl.BlockSpec((2, TQ, QK_PAD), lambda p, i: (p, i, 0)), k_spec, v_spec],
        out_specs=pl.BlockSpec((TQ, LANES), lambda p, i: (i, p)),
        out_shape=jax.ShapeDtypeStruct((t, A_HEADS * A_VDIM), BF16),
        scratch_shapes=[pltpu.VMEM((2, TK, TQ), F32), pltpu.VMEM((2, TK, TQ), F32)],
        compiler_params=_params("parallel", "arbitrary"), name="mla_prompt",
    )(q, k, v)


def _mla_decode_kernel(layer, pt_ref, qabs_ref, q_ref, cnew_ref, knew_ref, wuv_ref, ckv_hbm, kpe_hbm,
                       o_ref, cbuf, kbuf, kcat, pcat, csem, ksem):
    P = DECODE_PAGES
    n = pl.program_id(0)
    n_steps = pt_ref.shape[1] // P
    page = cbuf.shape[2]

    def page_copies(pid, slot, i):
        return (pltpu.make_async_copy(ckv_hbm.at[layer, pid], cbuf.at[slot, i], csem.at[slot]),
                pltpu.make_async_copy(kpe_hbm.at[layer, pid], kbuf.at[slot, i], ksem.at[slot]))

    def start(req, step, slot):
        for i in range(P):
            for cp in page_copies(pt_ref[req, step * P + i], slot, i):
                cp.start()

    def wait(slot):
        for i in range(P):
            for cp in page_copies(0, slot, i):
                cp.wait()

    ahead = DECODE_SLOTS - 1

    @pl.when(n == 0)
    def _():
        for step in range(ahead):
            start(0, step, step % DECODE_SLOTS)

    qa = qabs_ref[...]
    qpe = q_ref[:, A_NOPE:A_NOPE + A_ROPE]
    m = jnp.full((A_HEADS, 1), -jnp.inf, F32)
    l = jnp.zeros((A_HEADS, 1), F32)
    acc = jnp.zeros((A_HEADS, KV_RANK), F32)
    pending = None
    for step in range(n_steps):
        fetch = step + ahead
        if fetch < n_steps:
            start(n, fetch, fetch % DECODE_SLOTS)
        else:
            @pl.when(n + 1 < pl.num_programs(0))
            def _():
                start(n + 1, fetch - n_steps, fetch % DECODE_SLOTS)
        wait(step % DECODE_SLOTS)
        slot = step % 2
        for i in range(P):
            kcat[slot, i * page:(i + 1) * page, :] = cbuf[step % DECODE_SLOTS, i].astype(BF16)
            pcat[slot, :, i * page:(i + 1) * page] = kbuf[step % DECODE_SLOTS, i].astype(BF16)
        s = _dot_nt(qa, kcat[slot]) + jnp.dot(qpe, pcat[slot], preferred_element_type=F32)
        if pending is not None:
            pr_prev, alpha_prev, slot_prev = pending
            acc = acc * alpha_prev + jnp.dot(pr_prev, kcat[slot_prev], preferred_element_type=F32)
        m_new = jnp.maximum(m, jnp.max(s, axis=-1, keepdims=True))
        alpha = jnp.exp2(m - m_new)
        pr = jnp.exp2(s - m_new)
        l = l * alpha + jnp.sum(pr, axis=-1, keepdims=True)
        m = m_new
        pending = (pr.astype(BF16), alpha, slot)
    pr_prev, alpha_prev, slot_prev = pending
    acc = acc * alpha_prev + jnp.dot(pr_prev, kcat[slot_prev], preferred_element_type=F32)

    cnew = cnew_ref[...]
    s_self = (jnp.sum(qa * cnew, axis=-1, keepdims=True)
              + jnp.sum(qpe.astype(F32) * knew_ref[...], axis=-1, keepdims=True))
    m_fin = jnp.maximum(m, s_self)
    al = jnp.exp2(m - m_fin)
    p_self = jnp.exp2(s_self - m_fin)
    o_lat = (acc * al + p_self * cnew) / (l * al + p_self)
    res = _dot(o_lat, wuv_ref[...])
    hrow = lax.broadcasted_iota(jnp.int32, res.shape, 0)
    hcol = lax.broadcasted_iota(jnp.int32, res.shape, 1) // A_VDIM
    o_ref[...] = jnp.sum(jnp.where(hrow == hcol, res, 0.0), axis=0, keepdims=True).astype(o_ref.dtype)


def _mla_decode(page_table, qabs, q, ckv_new, kpe_new, wuv, cache_ckv, cache_kpe_t, layer):
    n, n_pages = page_table.shape
    P = DECODE_PAGES
    page = cache_ckv.shape[2]
    slots = DECODE_SLOTS
    assert (n_pages // P) % slots == 0, "buffer slots rotate per step and must line up across requests"
    req = lambda shape: pl.BlockSpec((None,) + shape, lambda b, pt: (b, 0, 0))
    hbm = pl.BlockSpec(memory_space=pl.ANY)
    grid_spec = pltpu.PrefetchScalarGridSpec(
        num_scalar_prefetch=1, grid=(n,),
        in_specs=[req((A_HEADS, KV_RANK)), req((A_HEADS, QK_PAD)), req((1, KV_RANK)), req((1, A_ROPE)),
                  pl.BlockSpec(wuv.shape, lambda b, pt: (0, 0)), hbm, hbm],
        out_specs=req((1, A_HEADS * A_VDIM)),
        scratch_shapes=[pltpu.VMEM((slots, P, page, KV_RANK), F32), pltpu.VMEM((slots, P, A_ROPE, page), F32),
                        pltpu.VMEM((2, P * page, KV_RANK), BF16), pltpu.VMEM((2, A_ROPE, P * page), BF16),
                        pltpu.SemaphoreType.DMA((slots,)), pltpu.SemaphoreType.DMA((slots,))])
    out = pl.pallas_call(
        functools.partial(_mla_decode_kernel, layer), grid_spec=grid_spec,
        out_shape=jax.ShapeDtypeStruct((n, 1, A_HEADS * A_VDIM), BF16),
        compiler_params=_params("arbitrary"), name="mla_decode",
    )(page_table, qabs, q, ckv_new[:, None, :], kpe_new[:, None, :], wuv, cache_ckv, cache_kpe_t)
    return out[:, 0, :]


def _mix_out_kernel(x_ref, oa_ref, y_ref, r_ref, k2_ref, v_ref, g_ref, gpre_ref, wga_ref, wgb_ref,
                    rk_ref, lnw_ref, lnb_ref, e_ref, wpa_ref, wpb_ref, wout_ref, gpost_ref, gmem_ref,
                    wmq_ref, x1_ref, qm_ref):
    x = x_ref[...]
    h = _rms(x, gpre_ref[...]).astype(BF16)
    gate_a = _sigmoid(jnp.dot(h, wga_ref[...], preferred_element_type=F32))
    gate_b = _sigmoid(jnp.dot(h, wgb_ref[...], preferred_element_type=F32))
    e = e_ref[...]
    y = y_ref[...]
    inv = 1.0 / B_HDIM
    d = y - _seg_sum(y, e) * inv
    var = _seg_sum(d * d, e) * inv
    yn = d * lax.rsqrt(var + GN_EPS) * lnw_ref[...] + lnb_ref[...]
    v = v_ref[...]
    bonus = _seg_sum(r_ref[...] * k2_ref[...] * rk_ref[...], e) * v
    ob = (yn + bonus) * g_ref[...]
    merged = gate_a * _dot(oa_ref[...], wpa_ref[...]) + gate_b * _dot(ob, wpb_ref[...])
    x1 = x + _rms(_dot(merged, wout_ref[...]), gpost_ref[...])
    x1_ref[...] = x1
    qm = _dot(_rms(x1, gmem_ref[...]), wmq_ref[...]) * ((x.shape[-1] // X_HEADS) ** -0.5)
    qm_ref[...] = qm.astype(BF16)


def _mix_out(x, oa, y, r, k2, v, g, *weights):
    m, d = x.shape
    tm = min(ROW_TILE, m)
    wide = _rows(tm, d)
    half = _rows(tm, B_WIDTH)
    return pl.pallas_call(
        _mix_out_kernel, grid=(m // tm,),
        in_specs=[wide] + [half] * 6 + [_const(w.shape) for w in weights],
        out_specs=[wide, wide],
        out_shape=[jax.ShapeDtypeStruct((m, d), F32), jax.ShapeDtypeStruct((m, d), BF16)],
        compiler_params=_params("parallel"), name="mix_out",
    )(x, oa, y, r, k2, v, g, *weights)


def _mem_kv_kernel(mem_ref, g_ref, wk_ref, wv_ref, k_ref, v_ref):
    h = _rms(mem_ref[...], g_ref[...]).astype(BF16)
    k_ref[...] = jnp.dot(h, wk_ref[...], preferred_element_type=F32)
    v_ref[...] = jnp.dot(h, wv_ref[...], preferred_element_type=F32)


def _mem_kv(mem, g, wk, wv):
    m, d = mem.shape
    o = jax.ShapeDtypeStruct((m, d), F32)
    return pl.pallas_call(
        _mem_kv_kernel, grid=(1,),
        in_specs=[_full(mem.shape), _full(g.shape), _full(wk.shape), _full(wv.shape)],
        out_specs=[_full((m, d))] * 2, out_shape=[o, o],
        compiler_params=_params("arbitrary"), name="mem_kv",
    )(mem, g, wk, wv)


def _softmax_rows(s):
    pr = jnp.exp(s - jnp.max(s, axis=-1, keepdims=True))
    return pr / jnp.sum(pr, axis=-1, keepdims=True)


def _mem_attn_shared_kernel(q_ref, k_ref, v_ref, o_ref):
    hd = q_ref.shape[-1] // X_HEADS
    heads = [slice(hh * hd, (hh + 1) * hd) for hh in range(X_HEADS)]
    s = [_dot_nt(q_ref[:, sl], k_ref[:, sl]) for sl in heads]
    pr = [_softmax_rows(sh) for sh in s]
    o = [_dot(pr[hh], v_ref[:, heads[hh]]) for hh in range(X_HEADS)]
    for hh in range(X_HEADS):
        o_ref[:, heads[hh]] = o[hh].astype(o_ref.dtype)


def _mem_attn_shared(q, mk, mv):
    m, d = q.shape
    tm = min(ROW_TILE, m)
    return pl.pallas_call(
        _mem_attn_shared_kernel, grid=(m // tm,),
        in_specs=[_rows(tm, d), _const(mk.shape), _const(mv.shape)],
        out_specs=_rows(tm, d), out_shape=jax.ShapeDtypeStruct((m, d), BF16),
        compiler_params=_params("parallel"), name="mem_attn_shared",
    )(q, mk, mv)


def _mem_attn_rows_kernel(q_ref, k_ref, v_ref, o_ref):
    for i in range(q_ref.shape[0]):
        prod = k_ref[i] * q_ref[i][None]
        s = jnp.sum(jnp.sum(prod, axis=1), axis=-1, keepdims=True)
        pr = jnp.exp(s - jnp.max(s, axis=0, keepdims=True))
        w = pr / jnp.sum(pr, axis=0, keepdims=True)
        o_ref[i] = jnp.sum(w[:, None] * v_ref[i], axis=0)


def _mem_attn_rows(q, mk, mv):
    n, mem, heads, hd = mk.shape
    nb = 4
    tiles = hd // LANES
    tiled = lambda x, lead: jnp.swapaxes(x.reshape(lead + (heads, tiles, LANES)), -2, -3)
    qspec = pl.BlockSpec((nb, tiles, heads, LANES), lambda i: (i, 0, 0, 0))
    kspec = pl.BlockSpec((nb, mem, tiles, heads, LANES), lambda i: (i, 0, 0, 0, 0))
    out = pl.pallas_call(
        _mem_attn_rows_kernel, grid=(n // nb,),
        in_specs=[qspec, kspec, kspec], out_specs=qspec,
        out_shape=jax.ShapeDtypeStruct((n, tiles, heads, LANES), F32),
        compiler_params=_params("parallel"), name="mem_attn_rows",
    )(tiled(q.astype(F32), (n,)), tiled(mk, (n, mem)), tiled(mv, (n, mem)))
    return jnp.swapaxes(out, 1, 2).reshape(n, heads * hd).astype(BF16)


def _tail_kernel(x1_ref, om_ref, wmo_ref, gpm_ref, gmlp_ref, wup_ref, wdn_ref, gpost_ref, y_ref):
    x2 = x1_ref[...] + _rms(jnp.dot(om_ref[...], wmo_ref[...], preferred_element_type=F32), gpm_ref[...])
    h = _rms(x2, gmlp_ref[...]).astype(BF16)
    u = jnp.maximum(jnp.dot(h, wup_ref[...], preferred_element_type=F32), 0.0)
    ff = jnp.dot((u * u).astype(BF16), wdn_ref[...], preferred_element_type=F32)
    y_ref[...] = x2 + _rms(ff, gpost_ref[...])


def _tail(x1, om, *weights):
    m, d = x1.shape
    tm = min(ROW_TILE, m)
    return pl.pallas_call(
        _tail_kernel, grid=(m // tm,),
        in_specs=[_rows(tm, d), _rows(tm, d)] + [_const(w.shape) for w in weights],
        out_specs=_rows(tm, d), out_shape=jax.ShapeDtypeStruct((m, d), F32),
        compiler_params=_params("parallel"), name="tail",
    )(x1, om, *weights)


def _rot_cols(w):
    half = w.shape[-1] // 2
    return jnp.concatenate([-w[..., half:], w[..., :half]], axis=-1)


def _rope_tables(pos):
    inv = ROPE_THETA ** (-jnp.arange(0, A_ROPE, 2, dtype=F32) / A_ROPE)
    ang = inv[:, None] * pos.astype(F32)[None, :]
    z_lo = jnp.zeros((A_NOPE, pos.shape[0]), F32)
    z_hi = jnp.zeros((QK_PAD - A_NOPE - A_ROPE, pos.shape[0]), F32)
    cos, sin = jnp.cos(ang), jnp.sin(ang)
    return (jnp.concatenate([z_lo, cos, cos, z_hi], axis=0),
            jnp.concatenate([z_lo, sin, sin, z_hi], axis=0))


def _prep_layer(l, w_in, w_uq, w_uk, w_uv, rw_decay_up, rw_a_up, rw_g_up):
    d = w_in.shape[1]
    wi = w_in[l]
    o_rw = Q_RANK + KV_RANK + A_ROPE
    o_ga = o_rw + RW_COLS
    w_kpe = wi[:, Q_RANK + KV_RANK:o_rw]
    z = lambda n: jnp.zeros((d, n), F32)
    pad_hi = QK_PAD - A_NOPE - A_ROPE
    wlat = jnp.concatenate([wi[:, :Q_RANK + KV_RANK], z(A_NOPE), w_kpe, z(pad_hi),
                            z(A_NOPE), _rot_cols(w_kpe), z(pad_hi)], axis=1).astype(BF16)
    wrw = jnp.pad(wi[:, o_rw:o_ga], ((0, 0), (0, RW_PAD - RW_COLS))).astype(BF16)
    wga = wi[:, o_ga:o_ga + d].astype(BF16)
    wgb = wi[:, o_ga + d:].astype(BF16)
    uq = w_uq[l].reshape(Q_RANK, A_HEADS, A_NOPE + A_ROPE)
    nope, pe = uq[..., :A_NOPE], uq[..., A_NOPE:]
    zq = lambda n: jnp.zeros((Q_RANK, A_HEADS, n), F32)
    wqa = jnp.concatenate([nope, pe, zq(pad_hi)], axis=-1).reshape(Q_RANK, -1).astype(BF16)
    wqb = jnp.concatenate([zq(A_NOPE), _rot_cols(pe), zq(pad_hi)], axis=-1).reshape(Q_RANK, -1).astype(BF16)
    pad_cols = lambda w: jnp.pad(w, ((0, 0), (0, 0), (0, QK_PAD - w.shape[-1]))).reshape(KV_RANK, -1).astype(BF16)
    wk_cols = pad_cols(w_uk[l])
    wv_cols = pad_cols(w_uv[l])
    wk_rows = jnp.pad(jnp.transpose(w_uk[l], (1, 2, 0)), ((0, 0), (0, QK_PAD - A_NOPE), (0, 0))).astype(BF16)
    wv = w_uv[l].reshape(KV_RANK, -1).astype(BF16)
    wl = jnp.zeros((LORA_IN, 3 * B_WIDTH), F32)
    wl = wl.at[:DECAY_LORA, :B_WIDTH].set(rw_decay_up[l])
    wl = wl.at[DECAY_LORA:DECAY_LORA + AAA_LORA, B_WIDTH:2 * B_WIDTH].set(rw_a_up[l])
    wl = wl.at[DECAY_LORA + AAA_LORA:, 2 * B_WIDTH:].set(rw_g_up[l])
    return wlat, wrw, wga, wgb, wqa, wqb, wk_cols, wv_cols, wk_rows, wv, wl.astype(BF16)


def kernel(x_prompt, x_sample, mem_prompt, cache_ckv, cache_kpe, state_wkv, state_shift, cache_mem_k, cache_mem_v, page_table, norm_pre_mix, w_in, q_norm, w_uq, kv_norm, w_uk, w_uv, rw_mu, rw_w0, rw_decay_up, rw_a0, rw_a_up, rw_g_up, rw_k_k, rw_k_a, rw_r_k, rw_ln_w, rw_ln_b, w_proj_a, w_proj_b, w_out, norm_post_mix, norm_pre_mem, mem_norm, w_mq, w_mk, w_mv, w_mo, norm_post_mem, norm_pre_mlp, w_ff_up, w_ff_down, norm_post_mlp):
    depth = w_in.shape[0]
    n_p, seq, d = x_prompt.shape
    n_s, dec_seq, _ = x_sample.shape
    assert n_p == 1 and dec_seq == 1, "one prompt sequence and one new token per decode request"
    past_len = page_table.shape[1] * cache_ckv.shape[2]
    ck_p, sk_p = _rope_tables(jnp.arange(seq))
    ck_s, sk_s = _rope_tables(jnp.full((n_s,), past_len))
    seg = jnp.arange(2 * LANES) // B_HDIM
    e = (seg[:, None] == seg[None, :]).astype(BF16)
    cache_kpe_t = jnp.swapaxes(cache_kpe, 2, 3)
    row = lambda p, l: p[l].reshape(1, -1)
    bf = lambda p, l: p[l].astype(BF16)

    y_p = x_prompt.reshape(seq, d)
    y_s = x_sample.reshape(n_s, d)
    outs = [[] for _ in range(10)]
    for l in range(depth):
        wlat, wrw, wga, wgb, wqa, wqb, wk_cols, wv_cols, wk_rows, wv, wl = _prep_layer(
            l, w_in, w_uq, w_uk, w_uv, rw_decay_up, rw_a_up, rw_g_up)
        proj_w = (row(norm_pre_mix, l), wlat, wrw, row(q_norm, l), row(kv_norm, l), wqa, wqb)
        prep_w = (row(rw_mu, l), row(rw_w0, l), row(rw_a0, l), wl, row(rw_k_k, l), row(rw_k_a, l), e)
        mix_w = (row(norm_pre_mix, l), wga, wgb, row(rw_r_k, l), row(rw_ln_w, l), row(rw_ln_b, l), e,
                 bf(w_proj_a, l), bf(w_proj_b, l), bf(w_out, l), row(norm_post_mix, l),
                 row(norm_pre_mem, l), bf(w_mq, l))
        tail_w = (bf(w_mo, l), row(norm_post_mem, l), row(norm_pre_mlp, l), bf(w_ff_up, l),
                  bf(w_ff_down, l), row(norm_post_mlp, l))

        q, ckv_p, kpe_p, rw_p, k, v, r_, lw_, k2_, v_, na_, b_, g_ = _proj_in(
            y_p, *proj_w, ck_p, sk_p, wk_cols, wv_cols, jnp.zeros((1, RW_COLS), F32), *prep_w, prompt=True)
        oa_p = _mla_prompt(q, k, v)
        yb_p, wkv_p = _rwkv_chunked(r_, lw_, k2_, v_, na_, b_)
        x1_p, qm_p = _mix_out(y_p, oa_p, yb_p, r_, k2_, v_, g_, *mix_w)
        mk_p, mv_p = _mem_kv(mem_prompt.reshape(-1, d), row(mem_norm, l), bf(w_mk, l), bf(w_mv, l))
        om_p = _mem_attn_shared(qm_p, mk_p, mv_p)
        y_p = _tail(x1_p, om_p, *tail_w)

        q, ckv_s, kpe_s, rw_s, qabs, r_, lw_, k2_, v_, na_, b_, g_ = _proj_in(
            y_s, *proj_w, ck_s, sk_s, wk_rows, wv, state_shift[l], *prep_w, prompt=False)
        oa_s = _mla_decode(page_table, jnp.transpose(qabs, (1, 0, 2)), jnp.transpose(q, (1, 0, 2)),
                           ckv_s, kpe_s, wv, cache_ckv, cache_kpe_t, l)
        yb_s, wkv_s = _rwkv_step(state_wkv[l], r_, lw_, k2_, v_, na_, b_)
        x1_s, qm_s = _mix_out(y_s, oa_s, yb_s, r_, k2_, v_, g_, *mix_w)
        om_s = _mem_attn_rows(qm_s, cache_mem_k[l], cache_mem_v[l])
        y_s = _tail(x1_s, om_s, *tail_w)

        mem_heads = (n_p, -1, X_HEADS, d // X_HEADS)
        layer_outs = (ckv_p.reshape(n_p, seq, KV_RANK), kpe_p.reshape(n_p, seq, A_ROPE),
                      wkv_p[None], rw_p[-1:], mk_p.reshape(mem_heads), mv_p.reshape(mem_heads),
                      ckv_s.reshape(n_s, 1, KV_RANK), kpe_s.reshape(n_s, 1, A_ROPE), wkv_s, rw_s)
        for lst, val in zip(outs, layer_outs):
            lst.append(val)
    return (y_p.reshape(n_p, seq, d), y_s.reshape(n_s, 1, d)) + tuple(jnp.stack(o) for o in outs)
```

```python
import functools
import math

import jax
import jax.numpy as jnp
from jax import lax
from jax.experimental import pallas as pl
from jax.experimental.pallas import tpu as pltpu

F32 = jnp.float32
BF16 = jnp.bfloat16

A_HEADS = 8
A_NOPE = 64
A_ROPE = 32
A_VDIM = 64
Q_RANK = 384
KV_RANK = 256
ROPE_THETA = 10000.0
B_HEADS = 8
B_HDIM = 64
B_WIDTH = B_HEADS * B_HDIM
DECAY_LORA = 64
AAA_LORA = 64
GATE_LORA = 160
LORA_IN = DECAY_LORA + AAA_LORA + GATE_LORA
GN_EPS = 64e-5
X_HEADS = 4
NORM_EPS = 1e-6
RW_COLS = 3 * B_WIDTH + LORA_IN

LANES = 128
SUBLANES = 8
QK_PAD = 128
RW_PAD = 15 * LANES
RWKV_CHUNK = 64
RWKV_BLOCK = 512
RWKV_GROUP = 2
ATTN_Q_BLOCK = 1024
ATTN_KV_BLOCK = 1024
DECODE_PAGES = 32
DECODE_SLOTS = 4
ROW_TILE = 512
VMEM_LIMIT = 48 * 1024 * 1024
QK_SCALE = (A_NOPE + A_ROPE) ** -0.5 * math.log2(math.e)


def _params(*sem):
    return pltpu.CompilerParams(dimension_semantics=sem, vmem_limit_bytes=VMEM_LIMIT)


def _full(shape):
    zeros = (0,) * len(shape)
    return pl.BlockSpec(shape, lambda *_: zeros)


def _const(shape):
    zeros = (0,) * len(shape)
    return pl.BlockSpec(shape, lambda *_: zeros, pipeline_mode=pl.Buffered(1))


def _rows(tm, width):
    return pl.BlockSpec((tm, width), lambda i: (i, 0))


def _rms(x, g):
    return x * lax.rsqrt(jnp.mean(x * x, axis=-1, keepdims=True) + NORM_EPS) * g


def _dot(a, b):
    return jnp.dot(a.astype(BF16), b.astype(BF16), preferred_element_type=F32)


def _dot_nt(a, b):
    return lax.dot_general(a.astype(BF16), b.astype(BF16), (((1,), (1,)), ((), ())),
                           preferred_element_type=F32)


def _dot_tn(a, b):
    return lax.dot_general(a.astype(BF16), b.astype(BF16), (((0,), (0,)), ((), ())),
                           preferred_element_type=F32)


def _sigmoid(x):
    return 1.0 / (1.0 + jnp.exp(-x))


def _seg_sum(x, e):
    hi = x.astype(BF16)
    lo = (x - hi.astype(F32)).astype(BF16)
    w = e.shape[0]
    parts = [jnp.dot(hi[:, c:c + w], e, preferred_element_type=F32)
             + jnp.dot(lo[:, c:c + w], e, preferred_element_type=F32) for c in range(0, x.shape[1], w)]
    return jnp.concatenate(parts, axis=1)


def _rwkv_inputs(rw, shifted, mu, w0, a0, wl, k_k, k_a, e):
    xs = rw + (shifted - rw) * mu
    r = xs[:, :B_WIDTH]
    k = xs[:, B_WIDTH:2 * B_WIDTH]
    v = xs[:, 2 * B_WIDTH:3 * B_WIDTH]
    lo = xs[:, 3 * B_WIDTH:]
    lane = lax.broadcasted_iota(jnp.int32, lo.shape, 1)
    t = jnp.where(lane < DECAY_LORA, jnp.tanh(lo),
                  jnp.where(lane < DECAY_LORA + AAA_LORA, lo, _sigmoid(lo)))
    l3 = jnp.dot(t.astype(BF16), wl, preferred_element_type=F32)
    z = -(w0 + l3[:, :B_WIDTH])
    softplus = jnp.maximum(z, 0.0) + jnp.log(1.0 + jnp.exp(-jnp.abs(z)))
    w = -softplus - 0.5
    a = _sigmoid(a0 + l3[:, B_WIDTH:2 * B_WIDTH])
    kkr = k * k_k
    kk = kkr / jnp.maximum(jnp.sqrt(_seg_sum(kkr * kkr, e)), 1e-12)
    lw = -jnp.exp(w)
    return r, lw, k * (1.0 + (a - 1.0) * k_a), v, -kk, kk * a, l3[:, 2 * B_WIDTH:]


def _proj_in_kernel(prompt, x_ref, g_ref, wlat_ref, wrw_ref, qn_ref, kvn_ref, wqa_ref, wqb_ref,
                    ck_ref, sk_ref, wk_ref, wv_ref, first_ref, mu_ref, w0_ref, a0_ref, wl_ref, kk_ref,
                    ka_ref, e_ref, *rest):
    if prompt:
        q_ref, ckv_ref, kpe_ref, rw_ref, k_ref, v_ref = rest[:6]
        prev_ref = rest[-1]
    else:
        q_ref, ckv_ref, kpe_ref, rw_ref, qabs_ref = rest[:5]
    rwkv_refs = rest[6:13] if prompt else rest[5:12]
    if prompt:
        @pl.when(pl.program_id(0) == 0)
        def _():
            prev_ref[...] = jnp.broadcast_to(first_ref[...], prev_ref.shape)

    h = _rms(x_ref[...], g_ref[...]).astype(BF16)
    lat = jnp.dot(h, wlat_ref[...], preferred_element_type=F32)
    cq = lat[:, :Q_RANK]
    ckv = _rms(lat[:, Q_RANK:Q_RANK + KV_RANK], kvn_ref[...])
    ck = ck_ref[...].T
    sk = sk_ref[...].T
    kpe = lat[:, 640:768] * ck + lat[:, 768:896] * sk
    ckv_ref[...] = ckv
    kpe_ref[...] = kpe[:, A_NOPE:A_NOPE + A_ROPE]
    rw = jnp.dot(h, wrw_ref[...], preferred_element_type=F32)[:, :RW_COLS]
    if prompt:
        ridx = lax.broadcasted_iota(jnp.int32, rw.shape, 0)
        shifted = jnp.where(ridx == 0, prev_ref[SUBLANES - 1:SUBLANES, :], pltpu.roll(rw, 1, axis=0))
        last_rows = rw[rw.shape[0] - SUBLANES:, :]
        prev_ref[...] = last_rows
        rw_ref[...] = last_rows
    else:
        shifted = first_ref[...]
        rw_ref[...] = rw
    vals = _rwkv_inputs(rw, shifted, mu_ref[...], w0_ref[...], a0_ref[...], wl_ref[...], kk_ref[...],
                        ka_ref[...], e_ref[...])
    for ref, val in zip(rwkv_refs, vals):
        ref[...] = val
    qn = _rms(cq, qn_ref[...]).astype(BF16)
    qa = jnp.dot(qn, wqa_ref[...], preferred_element_type=F32)
    qb = jnp.dot(qn, wqb_ref[...], preferred_element_type=F32)
    lane = lax.broadcasted_iota(jnp.int32, ck.shape, 1)
    cq_t = QK_SCALE * (ck + jnp.where(lane < A_NOPE, 1.0, 0.0))
    sq_t = QK_SCALE * sk
    ckv_b = ckv.astype(BF16)
    if prompt:
        kn = jnp.dot(ckv_b, wk_ref[...], preferred_element_type=F32)
        vn = jnp.dot(ckv_b, wv_ref[...], preferred_element_type=F32)
        ones_col = jnp.where(lane == A_VDIM, 1.0, 0.0)
    for hh in range(A_HEADS):
        sl = slice(hh * QK_PAD, (hh + 1) * QK_PAD)
        qh = (qa[:, sl] * cq_t + qb[:, sl] * sq_t).astype(BF16)
        q_ref[hh] = qh
        if prompt:
            k_ref[hh] = (kn[:, sl] + kpe).astype(BF16)
            v_ref[hh] = (vn[:, sl] + ones_col).T.astype(BF16)
        else:
            qabs_ref[hh] = jnp.dot(qh, wk_ref[hh], preferred_element_type=F32)


def _proj_in(x, g, wlat, wrw, qn, kvn, wqa, wqb, ck, sk, wk, wv, first, *rwkv_w, prompt):
    m, d = x.shape
    tm = min(ROW_TILE, m)
    heads_out = lambda w, dt: jax.ShapeDtypeStruct((A_HEADS, m, w), dt)
    heads_spec = lambda w: pl.BlockSpec((A_HEADS, tm, w), lambda i: (0, i, 0))
    rw_rows = SUBLANES if prompt else m
    out_shape = [heads_out(QK_PAD, BF16), jax.ShapeDtypeStruct((m, KV_RANK), F32),
                 jax.ShapeDtypeStruct((m, A_ROPE), F32), jax.ShapeDtypeStruct((rw_rows, RW_COLS), F32)]
    out_specs = [heads_spec(QK_PAD), _rows(tm, KV_RANK), _rows(tm, A_ROPE),
                 _full((SUBLANES, RW_COLS)) if prompt else _rows(tm, RW_COLS)]
    if prompt:
        out_shape += [heads_out(QK_PAD, BF16), jax.ShapeDtypeStruct((A_HEADS, QK_PAD, m), BF16)]
        out_specs += [heads_spec(QK_PAD), pl.BlockSpec((A_HEADS, QK_PAD, tm), lambda i: (0, 0, i))]
    else:
        out_shape += [heads_out(KV_RANK, F32)]
        out_specs += [heads_spec(KV_RANK)]
    out_shape += [jax.ShapeDtypeStruct((m, B_WIDTH), F32)] * 7
    out_specs += [_rows(tm, B_WIDTH)] * 7
    first_spec = _full(first.shape) if prompt else _rows(tm, RW_COLS)
    table_spec = pl.BlockSpec((LANES, tm), lambda i: (0, i))
    return pl.pallas_call(
        functools.partial(_proj_in_kernel, prompt),
        grid=(m // tm,),
        in_specs=[_rows(tm, d), _const(g.shape), _const(wlat.shape), _const(wrw.shape), _const(qn.shape),
                  _const(kvn.shape), _const(wqa.shape), _const(wqb.shape), table_spec,
                  table_spec, _const(wk.shape), _const(wv.shape), first_spec]
                 + [_const(w.shape) for w in rwkv_w],
        out_specs=out_specs, out_shape=out_shape,
        scratch_shapes=[pltpu.VMEM((SUBLANES, RW_COLS), F32)] if prompt else [],
        compiler_params=_params("arbitrary"), name="proj_in",
    )(x, g, wlat, wrw, qn, kvn, wqa, wqb, ck, sk, wk, wv, first, *rwkv_w)


def _rwkv_chunk_kernel(r_ref, lw_ref, k2_ref, v_ref, na_ref, b_ref, y_ref, sout_ref, s_ref):
    C = RWKV_CHUNK
    n_chunks = r_ref.shape[0] // C
    pairs = B_HEADS // 2

    @pl.when(pl.program_id(0) == 0)
    def _():
        s_ref[...] = jnp.zeros_like(s_ref)

    row = lax.broadcasted_iota(jnp.int32, (2 * C, 2 * C), 0)
    col = lax.broadcasted_iota(jnp.int32, (2 * C, 2 * C), 1)
    tok_r = jnp.bitwise_and(row, C - 1)
    tok_c = jnp.bitwise_and(col, C - 1)
    strict = tok_c < tok_r
    incl = tok_c <= tok_r
    eye = jnp.where(row == col, 1.0, 0.0)
    crow = lax.broadcasted_iota(jnp.int32, (C, C), 0)
    ccol = lax.broadcasted_iota(jnp.int32, (C, C), 1)
    cum = jnp.where(ccol <= crow, 1.0, 0.0).astype(BF16)
    head0 = lax.broadcasted_iota(jnp.int32, (C, LANES), 1) < B_HDIM

    def stack(x):
        return jnp.concatenate([jnp.where(head0, x, 0.0), jnp.where(head0, 0.0, x)], axis=0)

    ops, X, Arb, AV, VK = {}, {}, {}, {}, {}
    S = [s_ref[p] for p in range(pairs)]

    def independent(chunks):
        units = [(ci, p) for ci in chunks for p in range(pairs)]
        G_all = {}
        for ci in chunks:
            lw = lw_ref[ci * C:(ci + 1) * C, :]
            l_hi = lw.astype(BF16)
            rem = lw - l_hi.astype(F32)
            l_mid = rem.astype(BF16)
            l_lo = (rem - l_mid.astype(F32)).astype(BF16)
            g3 = jnp.dot(cum, jnp.concatenate([l_hi, l_mid, l_lo], axis=1), preferred_element_type=F32)
            G_all[ci] = g3[:, :B_WIDTH] + g3[:, B_WIDTH:2 * B_WIDTH] + g3[:, 2 * B_WIDTH:]
        yield
        for ci, p in units:
            rows = slice(ci * C, (ci + 1) * C)
            sl = slice(p * LANES, (p + 1) * LANES)
            lw = lw_ref[rows, sl]
            G = G_all[ci][:, sl]
            GC = G[C - 1:C, :]
            e_pos = jnp.exp(G)
            e_neg = jnp.exp(-G)
            e_prev = jnp.exp(G - lw)
            e_tail = jnp.exp(GC - G)
            r, k2, v = r_ref[rows, sl], k2_ref[rows, sl], v_ref[rows, sl]
            na, b = na_ref[rows, sl], b_ref[rows, sl]
            ops[ci, p] = dict(
                AR=jnp.concatenate([stack(na * e_prev), stack(r * e_pos)], axis=0).astype(BF16),
                BK=jnp.concatenate([stack(b * e_neg), stack(k2 * e_neg)], axis=0).astype(BF16),
                Bh=stack(b * e_tail).astype(BF16), Kh=stack(k2 * e_tail).astype(BF16),
                Vs=stack(v).astype(BF16), decay=jnp.exp(GC))
        AA = {u: _dot_nt(ops[u]["AR"], ops[u]["BK"]) for u in units}
        yield
        AakArk, sq = {}, {}
        for u in units:
            aa = AA[u]
            a_ab = jnp.where(strict, aa[:2 * C, :2 * C], 0.0)
            AakArk[u] = jnp.concatenate([jnp.where(strict, aa[:2 * C, 2 * C:], 0.0),
                                         jnp.where(incl, aa[2 * C:, 2 * C:], 0.0)], axis=0).astype(BF16)
            Arb[u] = jnp.where(incl, aa[2 * C:, :2 * C], 0.0).astype(BF16)
            X[u] = eye + a_ab
            sq[u] = a_ab
        sq = {u: _dot(sq[u], sq[u]) for u in units}
        yield
        n = 2
        while 2 * n < C:
            both = {u: _dot(jnp.concatenate([X[u], sq[u]], axis=0), sq[u]) for u in units}
            yield
            for u in units:
                X[u] = X[u] + both[u][:2 * C]
                sq[u] = both[u][2 * C:]
            n *= 2
        last = {u: _dot(X[u], sq[u]) for u in units}
        yield
        for u in units:
            X[u] = (X[u] + last[u]).astype(BF16)
            AV[u] = _dot(AakArk[u], ops[u]["Vs"])
            VK[u] = _dot_tn(ops[u]["Vs"], ops[u]["Kh"])
        yield

    def dependent(chunks):
        for ci in chunks:
            SS = [_dot_nt(ops[ci, p]["AR"], S[p]) for p in range(pairs)]
            yield
            U = [_dot(X[ci, p], SS[p][:2 * C] + AV[ci, p][:2 * C]) for p in range(pairs)]
            yield
            YU = [_dot(Arb[ci, p], U[p]) for p in range(pairs)]
            UB = [_dot_tn(U[p], ops[ci, p]["Bh"]) for p in range(pairs)]
            yield
            for p in range(pairs):
                Y = SS[p][2 * C:] + YU[p] + AV[ci, p][2 * C:]
                y_ref[ci * C:(ci + 1) * C, p * LANES:(p + 1) * LANES] = Y[:C] + Y[C:]
                S[p] = S[p] * ops[ci, p]["decay"] + UB[p] + VK[ci, p]

    def interleave(*stages):
        live = list(stages)
        while live:
            for gen in list(live):
                if next(gen, live) is live:
                    live.remove(gen)

    groups = [list(range(g, min(g + RWKV_GROUP, n_chunks))) for g in range(0, n_chunks, RWKV_GROUP)]
    interleave(independent(groups[0]))
    for prev, cur in zip(groups[:-1], groups[1:]):
        interleave(independent(cur), dependent(prev))
    interleave(dependent(groups[-1]))
    for p in range(pairs):
        s_ref[p] = S[p]

    @pl.when(pl.program_id(0) == pl.num_programs(0) - 1)
    def _():
        sout_ref[...] = s_ref[...]


def _rwkv_chunked(r, lw, k2, v, na, b):
    t = r.shape[0]
    spec = _rows(RWKV_BLOCK, B_WIDTH)
    pairs = B_HEADS // 2
    y, s = pl.pallas_call(
        _rwkv_chunk_kernel, grid=(t // RWKV_BLOCK,),
        in_specs=[spec] * 6,
        out_specs=[spec, _full((pairs, LANES, LANES))],
        out_shape=[jax.ShapeDtypeStruct((t, B_WIDTH), F32),
                   jax.ShapeDtypeStruct((pairs, LANES, LANES), F32)],
        scratch_shapes=[pltpu.VMEM((pairs, LANES, LANES), F32)],
        compiler_params=_params("arbitrary"), name="rwkv_chunked",
    )(r, lw, k2, v, na, b)
    s = s.reshape(pairs, 2, B_HDIM, 2, B_HDIM)
    wkv = jnp.stack([s[:, 0, :, 0, :], s[:, 1, :, 1, :]], axis=1).reshape(B_HEADS, B_HDIM, B_HDIM)
    return y, wkv


def _rwkv_step_kernel(s_ref, r_ref, lw_ref, k2_ref, na_ref, b_ref, v_ref, sout_ref, y_ref):
    S = s_ref[...]
    sa = jnp.sum(S * na_ref[...], axis=1, keepdims=True)
    S = S * jnp.exp(lw_ref[...]) + sa * b_ref[...] + v_ref[...] * k2_ref[...]
    sout_ref[...] = S
    y_ref[...] = jnp.sum(S * r_ref[...], axis=1, keepdims=True)


def _rwkv_step(state, r, lw, k2, v, na, b):
    n = state.shape[0]
    keyed = lambda x: x.T.reshape(B_HEADS, 1, B_HDIM, n)
    kspec = pl.BlockSpec((None, 1, B_HDIM, n), lambda h: (h, 0, 0, 0))
    vspec = pl.BlockSpec((None, B_HDIM, 1, n), lambda h: (h, 0, 0, 0))
    sspec = pl.BlockSpec((None, B_HDIM, B_HDIM, n), lambda h: (h, 0, 0, 0))
    s_new, y = pl.pallas_call(
        _rwkv_step_kernel, grid=(B_HEADS,),
        in_specs=[sspec, kspec, kspec, kspec, kspec, kspec, vspec],
        out_specs=[sspec, vspec],
        out_shape=[jax.ShapeDtypeStruct((B_HEADS, B_HDIM, B_HDIM, n), F32),
                   jax.ShapeDtypeStruct((B_HEADS, B_HDIM, 1, n), F32)],
        compiler_params=_params("parallel"), name="rwkv_step",
    )(jnp.transpose(state, (1, 2, 3, 0)), keyed(r), keyed(lw), keyed(k2), keyed(na), keyed(b),
      v.T.reshape(B_HEADS, B_HDIM, 1, n))
    return y.reshape(B_WIDTH, n).T, jnp.transpose(s_new, (3, 0, 1, 2))


def _mla_prompt_kernel(q_ref, k_ref, v_ref, o_ref, sa_ref, sb_ref):
    qi = pl.program_id(1)
    TQ, TK = ATTN_Q_BLOCK, ATTN_KV_BLOCK
    q = (q_ref[0], q_ref[1])
    n_full = (qi * TQ) // TK

    def scores(ks, s_ref):
        start = pl.multiple_of(ks * TK, TK)
        for hh in range(2):
            s_ref[hh] = _dot_nt(k_ref[hh, pl.ds(start, TK), :], q[hh])

    def update(carry, s, ks, width):
        start = pl.multiple_of(ks * TK, TK)
        pr, alpha, m_out = [], [], []
        for hh in range(2):
            m_new = jnp.maximum(carry[hh][0], jnp.max(s[hh], axis=0, keepdims=True))
            alpha.append(jnp.exp2(carry[hh][0] - m_new))
            pr.append(jnp.exp2(s[hh] - m_new).astype(BF16))
            m_out.append(m_new)
        pv = [jnp.dot(v_ref[hh, :, pl.ds(start, width)], pr[hh], preferred_element_type=F32)
              for hh in range(2)]
        return tuple((m_out[hh], carry[hh][1] * alpha[hh] + pv[hh]) for hh in range(2))

    def full_block(carry, ks, s_ref, next_ref):
        scores(ks + 1, next_ref)
        return update(carry, [s_ref[hh] for hh in range(2)], ks, TK)

    def body(j, carry):
        carry = full_block(carry, 2 * j, sa_ref, sb_ref)
        return full_block(carry, 2 * j + 1, sb_ref, sa_ref)

    scores(0, sa_ref)
    init = tuple((jnp.full((1, TQ), -jnp.inf, F32), jnp.zeros((QK_PAD, TQ), F32)) for _ in range(2))
    carry = lax.fori_loop(0, n_full // 2, body, init)

    def finish(widths, odd):
        cr, s_ref = carry, sa_ref
        if odd:
            cr, s_ref = full_block(cr, n_full - 1, sa_ref, sb_ref), sb_ref
        width = widths * TQ
        key = lax.broadcasted_iota(jnp.int32, (width, TQ), 0)
        qry = lax.broadcasted_iota(jnp.int32, (width, TQ), 1)
        keep = key <= qry + (widths - 1) * TQ
        s = [jnp.where(keep, s_ref[hh, :width, :], -jnp.inf) for hh in range(2)]
        (_, a0), (_, a1) = update(cr, s, n_full, width)
        o0 = a0[:A_VDIM] / a0[A_VDIM:A_VDIM + 1]
        o1 = a1[:A_VDIM] / a1[A_VDIM:A_VDIM + 1]
        o_ref[...] = jnp.concatenate([o0, o1], axis=0).T.astype(o_ref.dtype)

    for widths in range(1, TK // TQ + 1):
        for odd in (False, True):
            here = jnp.logical_and(qi * TQ - n_full * TK == (widths - 1) * TQ, (n_full % 2 == 1) == odd)
            pl.when(here)(functools.partial(finish, widths, odd))


def _mla_prompt(q, k, v):
    t = q.shape[1]
    TQ, TK = ATTN_Q_BLOCK, ATTN_KV_BLOCK
    assert TK % TQ == 0 and t % TK == 0
    k_spec = pl.BlockSpec((2, t, QK_PAD), lambda p, i: (p, 0, 0), pipeline_mode=pl.Buffered(1))
    v_spec = pl.BlockSpec((2, QK_PAD, t), lambda p, i: (p, 0, 0), pipeline_mode=pl.Buffered(1))
    return pl.pallas_call(
        _mla_prompt_kernel, grid=(A_HEADS // 2, t // TQ),
        in_specs=[pl.BlockSpec((2, TQ, QK_PAD), lambda p, i: (p, i, 0)), k_spec, v_spec],
        out_specs=pl.BlockSpec((TQ, LANES), lambda p, i: (i, p)),
        out_shape=jax.ShapeDtypeStruct((t, A_HEADS * A_VDIM), BF16),
        scratch_shapes=[pltpu.VMEM((2, TK, TQ), F32), pltpu.VMEM((2, TK, TQ), F32)],
        compiler_params=_params("parallel", "arbitrary"), name="mla_prompt",
    )(q, k, v)


def _mla_decode_kernel(layer, pt_ref, qabs_ref, q_ref, cnew_ref, knew_ref, wuv_ref, ckv_hbm, kpe_hbm,
                       o_ref, cbuf, kbuf, kcat, pcat, csem, ksem):
    P = DECODE_PAGES
    n = pl.program_id(0)
    n_steps = pt_ref.shape[1] // P
    page = cbuf.shape[2]

    def page_copies(pid, slot, i):
        return (pltpu.make_async_copy(ckv_hbm.at[layer, pid], cbuf.at[slot, i], csem.at[slot]),
                pltpu.make_async_copy(kpe_hbm.at[layer, pid], kbuf.at[slot, i], ksem.at[slot]))

    def start(req, step, slot):
        for i in range(P):
            for cp in page_copies(pt_ref[req, step * P + i], slot, i):
                cp.start()

    def wait(slot):
        for i in range(P):
            for cp in page_copies(0, slot, i):
                cp.wait()

    ahead = DECODE_SLOTS - 1

    @pl.when(n == 0)
    def _():
        for step in range(ahead):
            start(0, step, step % DECODE_SLOTS)

    qa = qabs_ref[...]
    qpe = q_ref[:, A_NOPE:A_NOPE + A_ROPE]
    m = jnp.full((A_HEADS, 1), -jnp.inf, F32)
    l = jnp.zeros((A_HEADS, 1), F32)
    acc = jnp.zeros((A_HEADS, KV_RANK), F32)
    pending = None
    for step in range(n_steps):
        fetch = step + ahead
        if fetch < n_steps:
            start(n, fetch, fetch % DECODE_SLOTS)
        else:
            @pl.when(n + 1 < pl.num_programs(0))
            def _():
                start(n + 1, fetch - n_steps, fetch % DECODE_SLOTS)
        wait(step % DECODE_SLOTS)
        slot = step % 2
        for i in range(P):
            kcat[slot, i * page:(i + 1) * page, :] = cbuf[step % DECODE_SLOTS, i].astype(BF16)
            pcat[slot, :, i * page:(i + 1) * page] = kbuf[step % DECODE_SLOTS, i].astype(BF16)
        s = _dot_nt(qa, kcat[slot]) + jnp.dot(qpe, pcat[slot], preferred_element_type=F32)
        if pending is not None:
            pr_prev, alpha_prev, slot_prev = pending
            acc = acc * alpha_prev + jnp.dot(pr_prev, kcat[slot_prev], preferred_element_type=F32)
        m_new = jnp.maximum(m, jnp.max(s, axis=-1, keepdims=True))
        alpha = jnp.exp2(m - m_new)
        pr = jnp.exp2(s - m_new)
        l = l * alpha + jnp.sum(pr, axis=-1, keepdims=True)
        m = m_new
        pending = (pr.astype(BF16), alpha, slot)
    pr_prev, alpha_prev, slot_prev = pending
    acc = acc * alpha_prev + jnp.dot(pr_prev, kcat[slot_prev], preferred_element_type=F32)

    cnew = cnew_ref[...]
    s_self = (jnp.sum(qa * cnew, axis=-1, keepdims=True)
              + jnp.sum(qpe.astype(F32) * knew_ref[...], axis=-1, keepdims=True))
    m_fin = jnp.maximum(m, s_self)
    al = jnp.exp2(m - m_fin)
    p_self = jnp.exp2(s_self - m_fin)
    o_lat = (acc * al + p_self * cnew) / (l * al + p_self)
    res = _dot(o_lat, wuv_ref[...])
    hrow = lax.broadcasted_iota(jnp.int32, res.shape, 0)
    hcol = lax.broadcasted_iota(jnp.int32, res.shape, 1) // A_VDIM
    o_ref[...] = jnp.sum(jnp.where(hrow == hcol, res, 0.0), axis=0, keepdims=True).astype(o_ref.dtype)


def _mla_decode(page_table, qabs, q, ckv_new, kpe_new, wuv, cache_ckv, cache_kpe_t, layer):
    n, n_pages = page_table.shape
    P = DECODE_PAGES
    page = cache_ckv.shape[2]
    slots = DECODE_SLOTS
    assert (n_pages // P) % slots == 0, "buffer slots rotate per step and must line up across requests"
    req = lambda shape: pl.BlockSpec((None,) + shape, lambda b, pt: (b, 0, 0))
    hbm = pl.BlockSpec(memory_space=pl.ANY)
    grid_spec = pltpu.PrefetchScalarGridSpec(
        num_scalar_prefetch=1, grid=(n,),
        in_specs=[req((A_HEADS, KV_RANK)), req((A_HEADS, QK_PAD)), req((1, KV_RANK)), req((1, A_ROPE)),
                  pl.BlockSpec(wuv.shape, lambda b, pt: (0, 0)), hbm, hbm],
        out_specs=req((1, A_HEADS * A_VDIM)),
        scratch_shapes=[pltpu.VMEM((slots, P, page, KV_RANK), F32), pltpu.VMEM((slots, P, A_ROPE, page), F32),
                        pltpu.VMEM((2, P * page, KV_RANK), BF16), pltpu.VMEM((2, A_ROPE, P * page), BF16),
                        pltpu.SemaphoreType.DMA((slots,)), pltpu.SemaphoreType.DMA((slots,))])
    out = pl.pallas_call(
        functools.partial(_mla_decode_kernel, layer), grid_spec=grid_spec,
        out_shape=jax.ShapeDtypeStruct((n, 1, A_HEADS * A_VDIM), BF16),
        compiler_params=_params("arbitrary"), name="mla_decode",
    )(page_table, qabs, q, ckv_new[:, None, :], kpe_new[:, None, :], wuv, cache_ckv, cache_kpe_t)
    return out[:, 0, :]


def _mix_out_kernel(x_ref, oa_ref, y_ref, r_ref, k2_ref, v_ref, g_ref, gpre_ref, wga_ref, wgb_ref,
                    rk_ref, lnw_ref, lnb_ref, e_ref, wpa_ref, wpb_ref, wout_ref, gpost_ref, gmem_ref,
                    wmq_ref, x1_ref, qm_ref):
    x = x_ref[...]
    h = _rms(x, gpre_ref[...]).astype(BF16)
    gate_a = _sigmoid(jnp.dot(h, wga_ref[...], preferred_element_type=F32))
    gate_b = _sigmoid(jnp.dot(h, wgb_ref[...], preferred_element_type=F32))
    e = e_ref[...]
    y = y_ref[...]
    inv = 1.0 / B_HDIM
    d = y - _seg_sum(y, e) * inv
    var = _seg_sum(d * d, e) * inv
    yn = d * lax.rsqrt(var + GN_EPS) * lnw_ref[...] + lnb_ref[...]
    v = v_ref[...]
    bonus = _seg_sum(r_ref[...] * k2_ref[...] * rk_ref[...], e) * v
    ob = (yn + bonus) * g_ref[...]
    merged = gate_a * _dot(oa_ref[...], wpa_ref[...]) + gate_b * _dot(ob, wpb_ref[...])
    x1 = x + _rms(_dot(merged, wout_ref[...]), gpost_ref[...])
    x1_ref[...] = x1
    qm = _dot(_rms(x1, gmem_ref[...]), wmq_ref[...]) * ((x.shape[-1] // X_HEADS) ** -0.5)
    qm_ref[...] = qm.astype(BF16)


def _mix_out(x, oa, y, r, k2, v, g, *weights):
    m, d = x.shape
    tm = min(ROW_TILE, m)
    wide = _rows(tm, d)
    half = _rows(tm, B_WIDTH)
    return pl.pallas_call(
        _mix_out_kernel, grid=(m // tm,),
        in_specs=[wide] + [half] * 6 + [_const(w.shape) for w in weights],
        out_specs=[wide, wide],
        out_shape=[jax.ShapeDtypeStruct((m, d), F32), jax.ShapeDtypeStruct((m, d), BF16)],
        compiler_params=_params("parallel"), name="mix_out",
    )(x, oa, y, r, k2, v, g, *weights)


def _mem_kv_kernel(mem_ref, g_ref, wk_ref, wv_ref, k_ref, v_ref):
    h = _rms(mem_ref[...], g_ref[...]).astype(BF16)
    k_ref[...] = jnp.dot(h, wk_ref[...], preferred_element_type=F32)
    v_ref[...] = jnp.dot(h, wv_ref[...], preferred_element_type=F32)


def _mem_kv(mem, g, wk, wv):
    m, d = mem.shape
    o = jax.ShapeDtypeStruct((m, d), F32)
    return pl.pallas_call(
        _mem_kv_kernel, grid=(1,),
        in_specs=[_full(mem.shape), _full(g.shape), _full(wk.shape), _full(wv.shape)],
        out_specs=[_full((m, d))] * 2, out_shape=[o, o],
        compiler_params=_params("arbitrary"), name="mem_kv",
    )(mem, g, wk, wv)


def _softmax_rows(s):
    pr = jnp.exp(s - jnp.max(s, axis=-1, keepdims=True))
    return pr / jnp.sum(pr, axis=-1, keepdims=True)


def _mem_attn_shared_kernel(q_ref, k_ref, v_ref, o_ref):
    hd = q_ref.shape[-1] // X_HEADS
    heads = [slice(hh * hd, (hh + 1) * hd) for hh in range(X_HEADS)]
    s = [_dot_nt(q_ref[:, sl], k_ref[:, sl]) for sl in heads]
    pr = [_softmax_rows(sh) for sh in s]
    o = [_dot(pr[hh], v_ref[:, heads[hh]]) for hh in range(X_HEADS)]
    for hh in range(X_HEADS):
        o_ref[:, heads[hh]] = o[hh].astype(o_ref.dtype)


def _mem_attn_shared(q, mk, mv):
    m, d = q.shape
    tm = min(ROW_TILE, m)
    return pl.pallas_call(
        _mem_attn_shared_kernel, grid=(m // tm,),
        in_specs=[_rows(tm, d), _const(mk.shape), _const(mv.shape)],
        out_specs=_rows(tm, d), out_shape=jax.ShapeDtypeStruct((m, d), BF16),
        compiler_params=_params("parallel"), name="mem_attn_shared",
    )(q, mk, mv)


def _mem_attn_rows_kernel(q_ref, k_ref, v_ref, o_ref):
    for i in range(q_ref.shape[0]):
        prod = k_ref[i] * q_ref[i][None]
        s = jnp.sum(jnp.sum(prod, axis=1), axis=-1, keepdims=True)
        pr = jnp.exp(s - jnp.max(s, axis=0, keepdims=True))
        w = pr / jnp.sum(pr, axis=0, keepdims=True)
        o_ref[i] = jnp.sum(w[:, None] * v_ref[i], axis=0)


def _mem_attn_rows(q, mk, mv):
    n, mem, heads, hd = mk.shape
    nb = 4
    tiles = hd // LANES
    tiled = lambda x, lead: jnp.swapaxes(x.reshape(lead + (heads, tiles, LANES)), -2, -3)
    qspec = pl.BlockSpec((nb, tiles, heads, LANES), lambda i: (i, 0, 0, 0))
    kspec = pl.BlockSpec((nb, mem, tiles, heads, LANES), lambda i: (i, 0, 0, 0, 0))
    out = pl.pallas_call(
        _mem_attn_rows_kernel, grid=(n // nb,),
        in_specs=[qspec, kspec, kspec], out_specs=qspec,
        out_shape=jax.ShapeDtypeStruct((n, tiles, heads, LANES), F32),
        compiler_params=_params("parallel"), name="mem_attn_rows",
    )(tiled(q.astype(F32), (n,)), tiled(mk, (n, mem)), tiled(mv, (n, mem)))
    return jnp.swapaxes(out, 1, 2).reshape(n, heads * hd).astype(BF16)


def _tail_kernel(x1_ref, om_ref, wmo_ref, gpm_ref, gmlp_ref, wup_ref, wdn_ref, gpost_ref, y_ref):
    x2 = x1_ref[...] + _rms(jnp.dot(om_ref[...], wmo_ref[...], preferred_element_type=F32), gpm_ref[...])
    h = _rms(x2, gmlp_ref[...]).astype(BF16)
    u = jnp.maximum(jnp.dot(h, wup_ref[...], preferred_element_type=F32), 0.0)
    ff = jnp.dot((u * u).astype(BF16), wdn_ref[...], preferred_element_type=F32)
    y_ref[...] = x2 + _rms(ff, gpost_ref[...])


def _tail(x1, om, *weights):
    m, d = x1.shape
    tm = min(ROW_TILE, m)
    return pl.pallas_call(
        _tail_kernel, grid=(m // tm,),
        in_specs=[_rows(tm, d), _rows(tm, d)] + [_const(w.shape) for w in weights],
        out_specs=_rows(tm, d), out_shape=jax.ShapeDtypeStruct((m, d), F32),
        compiler_params=_params("parallel"), name="tail",
    )(x1, om, *weights)


def _rot_cols(w):
    half = w.shape[-1] // 2
    return jnp.concatenate([-w[..., half:], w[..., :half]], axis=-1)


def _rope_tables(pos):
    inv = ROPE_THETA ** (-jnp.arange(0, A_ROPE, 2, dtype=F32) / A_ROPE)
    ang = inv[:, None] * pos.astype(F32)[None, :]
    z_lo = jnp.zeros((A_NOPE, pos.shape[0]), F32)
    z_hi = jnp.zeros((QK_PAD - A_NOPE - A_ROPE, pos.shape[0]), F32)
    cos, sin = jnp.cos(ang), jnp.sin(ang)
    return (jnp.concatenate([z_lo, cos, cos, z_hi], axis=0),
            jnp.concatenate([z_lo, sin, sin, z_hi], axis=0))


def _prep_layer(l, w_in, w_uq, w_uk, w_uv, rw_decay_up, rw_a_up, rw_g_up):
    d = w_in.shape[1]
    wi = w_in[l]
    o_rw = Q_RANK + KV_RANK + A_ROPE
    o_ga = o_rw + RW_COLS
    w_kpe = wi[:, Q_RANK + KV_RANK:o_rw]
    z = lambda n: jnp.zeros((d, n), F32)
    pad_hi = QK_PAD - A_NOPE - A_ROPE
    wlat = jnp.concatenate([wi[:, :Q_RANK + KV_RANK], z(A_NOPE), w_kpe, z(pad_hi),
                            z(A_NOPE), _rot_cols(w_kpe), z(pad_hi)], axis=1).astype(BF16)
    wrw = jnp.pad(wi[:, o_rw:o_ga], ((0, 0), (0, RW_PAD - RW_COLS))).astype(BF16)
    wga = wi[:, o_ga:o_ga + d].astype(BF16)
    wgb = wi[:, o_ga + d:].astype(BF16)
    uq = w_uq[l].reshape(Q_RANK, A_HEADS, A_NOPE + A_ROPE)
    nope, pe = uq[..., :A_NOPE], uq[..., A_NOPE:]
    zq = lambda n: jnp.zeros((Q_RANK, A_HEADS, n), F32)
    wqa = jnp.concatenate([nope, pe, zq(pad_hi)], axis=-1).reshape(Q_RANK, -1).astype(BF16)
    wqb = jnp.concatenate([zq(A_NOPE), _rot_cols(pe), zq(pad_hi)], axis=-1).reshape(Q_RANK, -1).astype(BF16)
    pad_cols = lambda w: jnp.pad(w, ((0, 0), (0, 0), (0, QK_PAD - w.shape[-1]))).reshape(KV_RANK, -1).astype(BF16)
    wk_cols = pad_cols(w_uk[l])
    wv_cols = pad_cols(w_uv[l])
    wk_rows = jnp.pad(jnp.transpose(w_uk[l], (1, 2, 0)), ((0, 0), (0, QK_PAD - A_NOPE), (0, 0))).astype(BF16)
    wv = w_uv[l].reshape(KV_RANK, -1).astype(BF16)
    wl = jnp.zeros((LORA_IN, 3 * B_WIDTH), F32)
    wl = wl.at[:DECAY_LORA, :B_WIDTH].set(rw_decay_up[l])
    wl = wl.at[DECAY_LORA:DECAY_LORA + AAA_LORA, B_WIDTH:2 * B_WIDTH].set(rw_a_up[l])
    wl = wl.at[DECAY_LORA + AAA_LORA:, 2 * B_WIDTH:].set(rw_g_up[l])
    return wlat, wrw, wga, wgb, wqa, wqb, wk_cols, wv_cols, wk_rows, wv, wl.astype(BF16)


def kernel(x_prompt, x_sample, mem_prompt, cache_ckv, cache_kpe, state_wkv, state_shift, cache_mem_k, cache_mem_v, page_table, norm_pre_mix, w_in, q_norm, w_uq, kv_norm, w_uk, w_uv, rw_mu, rw_w0, rw_decay_up, rw_a0, rw_a_up, rw_g_up, rw_k_k, rw_k_a, rw_r_k, rw_ln_w, rw_ln_b, w_proj_a, w_proj_b, w_out, norm_post_mix, norm_pre_mem, mem_norm, w_mq, w_mk, w_mv, w_mo, norm_post_mem, norm_pre_mlp, w_ff_up, w_ff_down, norm_post_mlp):
    depth = w_in.shape[0]
    n_p, seq, d = x_prompt.shape
    n_s, dec_seq, _ = x_sample.shape
    assert n_p == 1 and dec_seq == 1, "one prompt sequence and one new token per decode request"
    past_len = page_table.shape[1] * cache_ckv.shape[2]
    ck_p, sk_p = _rope_tables(jnp.arange(seq))
    ck_s, sk_s = _rope_tables(jnp.full((n_s,), past_len))
    seg = jnp.arange(2 * LANES) // B_HDIM
    e = (seg[:, None] == seg[None, :]).astype(BF16)
    cache_kpe_t = jnp.swapaxes(cache_kpe, 2, 3)
    row = lambda p, l: p[l].reshape(1, -1)
    bf = lambda p, l: p[l].astype(BF16)

    y_p = x_prompt.reshape(seq, d)
    y_s = x_sample.reshape(n_s, d)
    outs = [[] for _ in range(10)]
    for l in range(depth):
        wlat, wrw, wga, wgb, wqa, wqb, wk_cols, wv_cols, wk_rows, wv, wl = _prep_layer(
            l, w_in, w_uq, w_uk, w_uv, rw_decay_up, rw_a_up, rw_g_up)
        proj_w = (row(norm_pre_mix, l), wlat, wrw, row(q_norm, l), row(kv_norm, l), wqa, wqb)
        prep_w = (row(rw_mu, l), row(rw_w0, l), row(rw_a0, l), wl, row(rw_k_k, l), row(rw_k_a, l), e)
        mix_w = (row(norm_pre_mix, l), wga, wgb, row(rw_r_k, l), row(rw_ln_w, l), row(rw_ln_b, l), e,
                 bf(w_proj_a, l), bf(w_proj_b, l), bf(w_out, l), row(norm_post_mix, l),
                 row(norm_pre_mem, l), bf(w_mq, l))
        tail_w = (bf(w_mo, l), row(norm_post_mem, l), row(norm_pre_mlp, l), bf(w_ff_up, l),
                  bf(w_ff_down, l), row(norm_post_mlp, l))

        q, ckv_p, kpe_p, rw_p, k, v, r_, lw_, k2_, v_, na_, b_, g_ = _proj_in(
            y_p, *proj_w, ck_p, sk_p, wk_cols, wv_cols, jnp.zeros((1, RW_COLS), F32), *prep_w, prompt=True)
        oa_p = _mla_prompt(q, k, v)
        yb_p, wkv_p = _rwkv_chunked(r_, lw_, k2_, v_, na_, b_)
        x1_p, qm_p = _mix_out(y_p, oa_p, yb_p, r_, k2_, v_, g_, *mix_w)
        mk_p, mv_p = _mem_kv(mem_prompt.reshape(-1, d), row(mem_norm, l), bf(w_mk, l), bf(w_mv, l))
        om_p = _mem_attn_shared(qm_p, mk_p, mv_p)
        y_p = _tail(x1_p, om_p, *tail_w)

        q, ckv_s, kpe_s, rw_s, qabs, r_, lw_, k2_, v_, na_, b_, g_ = _proj_in(
            y_s, *proj_w, ck_s, sk_s, wk_rows, wv, state_shift[l], *prep_w, prompt=False)
        oa_s = _mla_decode(page_table, jnp.transpose(qabs, (1, 0, 2)), jnp.transpose(q, (1, 0, 2)),
                           ckv_s, kpe_s, wv, cache_ckv, cache_kpe_t, l)
        yb_s, wkv_s = _rwkv_step(state_wkv[l], r_, lw_, k2_, v_, na_, b_)
        x1_s, qm_s = _mix_out(y_s, oa_s, yb_s, r_, k2_, v_, g_, *mix_w)
        om_s = _mem_attn_rows(qm_s, cache_mem_k[l], cache_mem_v[l])
        y_s = _tail(x1_s, om_s, *tail_w)

        mem_heads = (n_p, -1, X_HEADS, d // X_HEADS)
        layer_outs = (ckv_p.reshape(n_p, seq, KV_RANK), kpe_p.reshape(n_p, seq, A_ROPE),
                      wkv_p[None], rw_p[-1:], mk_p.reshape(mem_heads), mv_p.reshape(mem_heads),
                      ckv_s.reshape(n_s, 1, KV_RANK), kpe_s.reshape(n_s, 1, A_ROPE), wkv_s, rw_s)
        for lst, val in zip(outs, layer_outs):
            lst.append(val)
    return (y_p.reshape(n_p, seq, d), y_s.reshape(n_s, 1, d)) + tuple(jnp.stack(o) for o in outs)
```

```python
import functools
import math

import jax
import jax.numpy as jnp
from jax import lax
from jax.experimental import pallas as pl
from jax.experimental.pallas import tpu as pltpu

F32 = jnp.float32
BF16 = jnp.bfloat16

A_HEADS = 8
A_NOPE = 64
A_ROPE = 32
A_VDIM = 64
Q_RANK = 384
KV_RANK = 256
ROPE_THETA = 10000.0
B_HEADS = 8
B_HDIM = 64
B_WIDTH = B_HEADS * B_HDIM
DECAY_LORA = 64
AAA_LORA = 64
GATE_LORA = 160
LORA_IN = DECAY_LORA + AAA_LORA + GATE_LORA
GN_EPS = 64e-5
X_HEADS = 4
NORM_EPS = 1e-6
RW_COLS = 3 * B_WIDTH + LORA_IN

LANES = 128
SUBLANES = 8
QK_PAD = 128
RW_PAD = 15 * LANES
RWKV_CHUNK = 64
RWKV_BLOCK = 512
RWKV_GROUP = 2
ATTN_Q_BLOCK = 1024
ATTN_KV_BLOCK = 1024
DECODE_PAGES = 32
DECODE_SLOTS = 4
ROW_TILE = 512
VMEM_LIMIT = 48 * 1024 * 1024
QK_SCALE = (A_NOPE + A_ROPE) ** -0.5 * math.log2(math.e)


def _params(*sem):
    return pltpu.CompilerParams(dimension_semantics=sem, vmem_limit_bytes=VMEM_LIMIT)


def _full(shape):
    zeros = (0,) * len(shape)
    return pl.BlockSpec(shape, lambda *_: zeros)


def _const(shape):
    zeros = (0,) * len(shape)
    return pl.BlockSpec(shape, lambda *_: zeros, pipeline_mode=pl.Buffered(1))


def _rows(tm, width):
    return pl.BlockSpec((tm, width), lambda i: (i, 0))


def _rms(x, g):
    return x * lax.rsqrt(jnp.mean(x * x, axis=-1, keepdims=True) + NORM_EPS) * g


def _dot(a, b):
    return jnp.dot(a.astype(BF16), b.astype(BF16), preferred_element_type=F32)


def _dot_nt(a, b):
    return lax.dot_general(a.astype(BF16), b.astype(BF16), (((1,), (1,)), ((), ())),
                           preferred_element_type=F32)


def _dot_tn(a, b):
    return lax.dot_general(a.astype(BF16), b.astype(BF16), (((0,), (0,)), ((), ())),
                           preferred_element_type=F32)


def _sigmoid(x):
    return 1.0 / (1.0 + jnp.exp(-x))


def _seg_sum(x, e):
    hi = x.astype(BF16)
    lo = (x - hi.astype(F32)).astype(BF16)
    w = e.shape[0]
    parts = [jnp.dot(hi[:, c:c + w], e, preferred_element_type=F32)
             + jnp.dot(lo[:, c:c + w], e, preferred_element_type=F32) for c in range(0, x.shape[1], w)]
    return jnp.concatenate(parts, axis=1)


def _rwkv_inputs(rw, shifted, mu, w0, a0, wl, k_k, k_a, e):
    xs = rw + (shifted - rw) * mu
    r = xs[:, :B_WIDTH]
    k = xs[:, B_WIDTH:2 * B_WIDTH]
    v = xs[:, 2 * B_WIDTH:3 * B_WIDTH]
    lo = xs[:, 3 * B_WIDTH:]
    lane = lax.broadcasted_iota(jnp.int32, lo.shape, 1)
    t = jnp.where(lane < DECAY_LORA, jnp.tanh(lo),
                  jnp.where(lane < DECAY_LORA + AAA_LORA, lo, _sigmoid(lo)))
    l3 = jnp.dot(t.astype(BF16), wl, preferred_element_type=F32)
    z = -(w0 + l3[:, :B_WIDTH])
    softplus = jnp.maximum(z, 0.0) + jnp.log(1.0 + jnp.exp(-jnp.abs(z)))
    w = -softplus - 0.5
    a = _sigmoid(a0 + l3[:, B_WIDTH:2 * B_WIDTH])
    kkr = k * k_k
    kk = kkr / jnp.maximum(jnp.sqrt(_seg_sum(kkr * kkr, e)), 1e-12)
    lw = -jnp.exp(w)
    return r, lw, k * (1.0 + (a - 1.0) * k_a), v, -kk, kk * a, l3[:, 2 * B_WIDTH:]


def _proj_in_kernel(prompt, x_ref, g_ref, wlat_ref, wrw_ref, qn_ref, kvn_ref, wqa_ref, wqb_ref,
                    ck_ref, sk_ref, wk_ref, wv_ref, first_ref, mu_ref, w0_ref, a0_ref, wl_ref, kk_ref,
                    ka_ref, e_ref, *rest):
    if prompt:
        q_ref, ckv_ref, kpe_ref, rw_ref, k_ref, v_ref = rest[:6]
        prev_ref = rest[-1]
    else:
        q_ref, ckv_ref, kpe_ref, rw_ref, qabs_ref = rest[:5]
    rwkv_refs = rest[6:13] if prompt else rest[5:12]
    if prompt:
        @pl.when(pl.program_id(0) == 0)
        def _():
            prev_ref[...] = jnp.broadcast_to(first_ref[...], prev_ref.shape)

    h = _rms(x_ref[...], g_ref[...]).astype(BF16)
    lat = jnp.dot(h, wlat_ref[...], preferred_element_type=F32)
    cq = lat[:, :Q_RANK]
    ckv = _rms(lat[:, Q_RANK:Q_RANK + KV_RANK], kvn_ref[...])
    ck = ck_ref[...].T
    sk = sk_ref[...].T
    kpe = lat[:, 640:768] * ck + lat[:, 768:896] * sk
    ckv_ref[...] = ckv
    kpe_ref[...] = kpe[:, A_NOPE:A_NOPE + A_ROPE]
    rw = jnp.dot(h, wrw_ref[...], preferred_element_type=F32)[:, :RW_COLS]
    if prompt:
        ridx = lax.broadcasted_iota(jnp.int32, rw.shape, 0)
        shifted = jnp.where(ridx == 0, prev_ref[SUBLANES - 1:SUBLANES, :], pltpu.roll(rw, 1, axis=0))
        last_rows = rw[rw.shape[0] - SUBLANES:, :]
        prev_ref[...] = last_rows
        rw_ref[...] = last_rows
    else:
        shifted = first_ref[...]
        rw_ref[...] = rw
    vals = _rwkv_inputs(rw, shifted, mu_ref[...], w0_ref[...], a0_ref[...], wl_ref[...], kk_ref[...],
                        ka_ref[...], e_ref[...])
    for ref, val in zip(rwkv_refs, vals):
        ref[...] = val
    qn = _rms(cq, qn_ref[...]).astype(BF16)
    qa = jnp.dot(qn, wqa_ref[...], preferred_element_type=F32)
    qb = jnp.dot(qn, wqb_ref[...], preferred_element_type=F32)
    lane = lax.broadcasted_iota(jnp.int32, ck.shape, 1)
    cq_t = QK_SCALE * (ck + jnp.where(lane < A_NOPE, 1.0, 0.0))
    sq_t = QK_SCALE * sk
    ckv_b = ckv.astype(BF16)
    if prompt:
        kn = jnp.dot(ckv_b, wk_ref[...], preferred_element_type=F32)
        vn = jnp.dot(ckv_b, wv_ref[...], preferred_element_type=F32)
        ones_col = jnp.where(lane == A_VDIM, 1.0, 0.0)
    for hh in range(A_HEADS):
        sl = slice(hh * QK_PAD, (hh + 1) * QK_PAD)
        qh = (qa[:, sl] * cq_t + qb[:, sl] * sq_t).astype(BF16)
        q_ref[hh] = qh
        if prompt:
            k_ref[hh] = (kn[:, sl] + kpe).astype(BF16)
            v_ref[hh] = (vn[:, sl] + ones_col).T.astype(BF16)
        else:
            qabs_ref[hh] = jnp.dot(qh, wk_ref[hh], preferred_element_type=F32)


def _proj_in(x, g, wlat, wrw, qn, kvn, wqa, wqb, ck, sk, wk, wv, first, *rwkv_w, prompt):
    m, d = x.shape
    tm = min(ROW_TILE, m)
    heads_out = lambda w, dt: jax.ShapeDtypeStruct((A_HEADS, m, w), dt)
    heads_spec = lambda w: pl.BlockSpec((A_HEADS, tm, w), lambda i: (0, i, 0))
    rw_rows = SUBLANES if prompt else m
    out_shape = [heads_out(QK_PAD, BF16), jax.ShapeDtypeStruct((m, KV_RANK), F32),
                 jax.ShapeDtypeStruct((m, A_ROPE), F32), jax.ShapeDtypeStruct((rw_rows, RW_COLS), F32)]
    out_specs = [heads_spec(QK_PAD), _rows(tm, KV_RANK), _rows(tm, A_ROPE),
                 _full((SUBLANES, RW_COLS)) if prompt else _rows(tm, RW_COLS)]
    if prompt:
        out_shape += [heads_out(QK_PAD, BF16), jax.ShapeDtypeStruct((A_HEADS, QK_PAD, m), BF16)]
        out_specs += [heads_spec(QK_PAD), pl.BlockSpec((A_HEADS, QK_PAD, tm), lambda i: (0, 0, i))]
    else:
        out_shape += [heads_out(KV_RANK, F32)]
        out_specs += [heads_spec(KV_RANK)]
    out_shape += [jax.ShapeDtypeStruct((m, B_WIDTH), F32)] * 7
    out_specs += [_rows(tm, B_WIDTH)] * 7
    first_spec = _full(first.shape) if prompt else _rows(tm, RW_COLS)
    table_spec = pl.BlockSpec((LANES, tm), lambda i: (0, i))
    return pl.pallas_call(
        functools.partial(_proj_in_kernel, prompt),
        grid=(m // tm,),
        in_specs=[_rows(tm, d), _const(g.shape), _const(wlat.shape), _const(wrw.shape), _const(qn.shape),
                  _const(kvn.shape), _const(wqa.shape), _const(wqb.shape), table_spec,
                  table_spec, _const(wk.shape), _const(wv.shape), first_spec]
                 + [_const(w.shape) for w in rwkv_w],
        out_specs=out_specs, out_shape=out_shape,
        scratch_shapes=[pltpu.VMEM((SUBLANES, RW_COLS), F32)] if prompt else [],
        compiler_params=_params("arbitrary"), name="proj_in",
    )(x, g, wlat, wrw, qn, kvn, wqa, wqb, ck, sk, wk, wv, first, *rwkv_w)


def _rwkv_chunk_kernel(r_ref, lw_ref, k2_ref, v_ref, na_ref, b_ref, y_ref, sout_ref, s_ref):
    C = RWKV_CHUNK
    n_chunks = r_ref.shape[0] // C
    pairs = B_HEADS // 2

    @pl.when(pl.program_id(0) == 0)
    def _():
        s_ref[...] = jnp.zeros_like(s_ref)

    row = lax.broadcasted_iota(jnp.int32, (2 * C, 2 * C), 0)
    col = lax.broadcasted_iota(jnp.int32, (2 * C, 2 * C), 1)
    tok_r = jnp.bitwise_and(row, C - 1)
    tok_c = jnp.bitwise_and(col, C - 1)
    strict = tok_c < tok_r
    incl = tok_c <= tok_r
    eye = jnp.where(row == col, 1.0, 0.0)
    crow = lax.broadcasted_iota(jnp.int32, (C, C), 0)
    ccol = lax.broadcasted_iota(jnp.int32, (C, C), 1)
    cum = jnp.where(ccol <= crow, 1.0, 0.0).astype(BF16)
    head0 = lax.broadcasted_iota(jnp.int32, (C, LANES), 1) < B_HDIM

    def stack(x):
        return jnp.concatenate([jnp.where(head0, x, 0.0), jnp.where(head0, 0.0, x)], axis=0)

    ops, X, Arb, AV, VK = {}, {}, {}, {}, {}
    S = [s_ref[p] for p in range(pairs)]

    def independent(chunks):
        units = [(ci, p) for ci in chunks for p in range(pairs)]
        G_all = {}
        for ci in chunks:
            lw = lw_ref[ci * C:(ci + 1) * C, :]
            l_hi = lw.astype(BF16)
            rem = lw - l_hi.astype(F32)
            l_mid = rem.astype(BF16)
            l_lo = (rem - l_mid.astype(F32)).astype(BF16)
            g3 = jnp.dot(cum, jnp.concatenate([l_hi, l_mid, l_lo], axis=1), preferred_element_type=F32)
            G_all[ci] = g3[:, :B_WIDTH] + g3[:, B_WIDTH:2 * B_WIDTH] + g3[:, 2 * B_WIDTH:]
        yield
        for ci, p in units:
            rows = slice(ci * C, (ci + 1) * C)
            sl = slice(p * LANES, (p + 1) * LANES)
            lw = lw_ref[rows, sl]
            G = G_all[ci][:, sl]
            GC = G[C - 1:C, :]
            e_pos = jnp.exp(G)
            e_neg = jnp.exp(-G)
            e_prev = jnp.exp(G - lw)
            e_tail = jnp.exp(GC - G)
            r, k2, v = r_ref[rows, sl], k2_ref[rows, sl], v_ref[rows, sl]
            na, b = na_ref[rows, sl], b_ref[rows, sl]
            ops[ci, p] = dict(
                AR=jnp.concatenate([stack(na * e_prev), stack(r * e_pos)], axis=0).astype(BF16),
                BK=jnp.concatenate([stack(b * e_neg), stack(k2 * e_neg)], axis=0).astype(BF16),
                Bh=stack(b * e_tail).astype(BF16), Kh=stack(k2 * e_tail).astype(BF16),
                Vs=stack(v).astype(BF16), decay=jnp.exp(GC))
        AA = {u: _dot_nt(ops[u]["AR"], ops[u]["BK"]) for u in units}
        yield
        AakArk, sq = {}, {}
        for u in units:
            aa = AA[u]
            a_ab = jnp.where(strict, aa[:2 * C, :2 * C], 0.0)
            AakArk[u] = jnp.concatenate([jnp.where(strict, aa[:2 * C, 2 * C:], 0.0),
                                         jnp.where(incl, aa[2 * C:, 2 * C:], 0.0)], axis=0).astype(BF16)
            Arb[u] = jnp.where(incl, aa[2 * C:, :2 * C], 0.0).astype(BF16)
            X[u] = eye + a_ab
            sq[u] = a_ab
        sq = {u: _dot(sq[u], sq[u]) for u in units}
        yield
        n = 2
        while 2 * n < C:
            both = {u: _dot(jnp.concatenate([X[u], sq[u]], axis=0), sq[u]) for u in units}
            yield
            for u in units:
                X[u] = X[u] + both[u][:2 * C]
                sq[u] = both[u][2 * C:]
            n *= 2
        last = {u: _dot(X[u], sq[u]) for u in units}
        yield
        for u in units:
            X[u] = (X[u] + last[u]).astype(BF16)
            AV[u] = _dot(AakArk[u], ops[u]["Vs"])
            VK[u] = _dot_tn(ops[u]["Vs"], ops[u]["Kh"])
        yield

    def dependent(chunks):
        for ci in chunks:
            SS = [_dot_nt(ops[ci, p]["AR"], S[p]) for p in range(pairs)]
            yield
            U = [_dot(X[ci, p], SS[p][:2 * C] + AV[ci, p][:2 * C]) for p in range(pairs)]
            yield
            YU = [_dot(Arb[ci, p], U[p]) for p in range(pairs)]
            UB = [_dot_tn(U[p], ops[ci, p]["Bh"]) for p in range(pairs)]
            yield
            for p in range(pairs):
                Y = SS[p][2 * C:] + YU[p] + AV[ci, p][2 * C:]
                y_ref[ci * C:(ci + 1) * C, p * LANES:(p + 1) * LANES] = Y[:C] + Y[C:]
                S[p] = S[p] * ops[ci, p]["decay"] + UB[p] + VK[ci, p]

    def interleave(*stages):
        live = list(stages)
        while live:
            for gen in list(live):
                if next(gen, live) is live:
                    live.remove(gen)

    groups = [list(range(g, min(g + RWKV_GROUP, n_chunks))) for g in range(0, n_chunks, RWKV_GROUP)]
    interleave(independent(groups[0]))
    for prev, cur in zip(groups[:-1], groups[1:]):
        interleave(independent(cur), dependent(prev))
    interleave(dependent(groups[-1]))
    for p in range(pairs):
        s_ref[p] = S[p]

    @pl.when(pl.program_id(0) == pl.num_programs(0) - 1)
    def _():
        sout_ref[...] = s_ref[...]


def _rwkv_chunked(r, lw, k2, v, na, b):
    t = r.shape[0]
    spec = _rows(RWKV_BLOCK, B_WIDTH)
    pairs = B_HEADS // 2
    y, s = pl.pallas_call(
        _rwkv_chunk_kernel, grid=(t // RWKV_BLOCK,),
        in_specs=[spec] * 6,
        out_specs=[spec, _full((pairs, LANES, LANES))],
        out_shape=[jax.ShapeDtypeStruct((t, B_WIDTH), F32),
                   jax.ShapeDtypeStruct((pairs, LANES, LANES), F32)],
        scratch_shapes=[pltpu.VMEM((pairs, LANES, LANES), F32)],
        compiler_params=_params("arbitrary"), name="rwkv_chunked",
    )(r, lw, k2, v, na, b)
    s = s.reshape(pairs, 2, B_HDIM, 2, B_HDIM)
    wkv = jnp.stack([s[:, 0, :, 0, :], s[:, 1, :, 1, :]], axis=1).reshape(B_HEADS, B_HDIM, B_HDIM)
    return y, wkv


def _rwkv_step_kernel(s_ref, r_ref, lw_ref, k2_ref, na_ref, b_ref, v_ref, sout_ref, y_ref):
    S = s_ref[...]
    sa = jnp.sum(S * na_ref[...], axis=1, keepdims=True)
    S = S * jnp.exp(lw_ref[...]) + sa * b_ref[...] + v_ref[...] * k2_ref[...]
    sout_ref[...] = S
    y_ref[...] = jnp.sum(S * r_ref[...], axis=1, keepdims=True)


def _rwkv_step(state, r, lw, k2, v, na, b):
    n = state.shape[0]
    keyed = lambda x: x.T.reshape(B_HEADS, 1, B_HDIM, n)
    kspec = pl.BlockSpec((None, 1, B_HDIM, n), lambda h: (h, 0, 0, 0))
    vspec = pl.BlockSpec((None, B_HDIM, 1, n), lambda h: (h, 0, 0, 0))
    sspec = pl.BlockSpec((None, B_HDIM, B_HDIM, n), lambda h: (h, 0, 0, 0))
    s_new, y = pl.pallas_call(
        _rwkv_step_kernel, grid=(B_HEADS,),
        in_specs=[sspec, kspec, kspec, kspec, kspec, kspec, vspec],
        out_specs=[sspec, vspec],
        out_shape=[jax.ShapeDtypeStruct((B_HEADS, B_HDIM, B_HDIM, n), F32),
                   jax.ShapeDtypeStruct((B_HEADS, B_HDIM, 1, n), F32)],
        compiler_params=_params("parallel"), name="rwkv_step",
    )(jnp.transpose(state, (1, 2, 3, 0)), keyed(r), keyed(lw), keyed(k2), keyed(na), keyed(b),
      v.T.reshape(B_HEADS, B_HDIM, 1, n))
    return y.reshape(B_WIDTH, n).T, jnp.transpose(s_new, (3, 0, 1, 2))


def _mla_prompt_kernel(q_ref, k_ref, v_ref, o_ref, sa_ref, sb_ref):
    qi = pl.program_id(1)
    TQ, TK = ATTN_Q_BLOCK, ATTN_KV_BLOCK
    q = (q_ref[0], q_ref[1])
    n_full = (qi * TQ) // TK

    def scores(ks, s_ref):
        start = pl.multiple_of(ks * TK, TK)
        for hh in range(2):
            s_ref[hh] = _dot_nt(k_ref[hh, pl.ds(start, TK), :], q[hh])

    def update(carry, s, ks, width):
        start = pl.multiple_of(ks * TK, TK)
        pr, alpha, m_out = [], [], []
        for hh in range(2):
            m_new = jnp.maximum(carry[hh][0], jnp.max(s[hh], axis=0, keepdims=True))
            alpha.append(jnp.exp2(carry[hh][0] - m_new))
            pr.append(jnp.exp2(s[hh] - m_new).astype(BF16))
            m_out.append(m_new)
        pv = [jnp.dot(v_ref[hh, :, pl.ds(start, width)], pr[hh], preferred_element_type=F32)
              for hh in range(2)]
        return tuple((m_out[hh], carry[hh][1] * alpha[hh] + pv[hh]) for hh in range(2))

    def full_block(carry, ks, s_ref, next_ref):
        scores(ks + 1, next_ref)
        return update(carry, [s_ref[hh] for hh in range(2)], ks, TK)

    def body(j, carry):
        carry = full_block(carry, 2 * j, sa_ref, sb_ref)
        return full_block(carry, 2 * j + 1, sb_ref, sa_ref)

    scores(0, sa_ref)
    init = tuple((jnp.full((1, TQ), -jnp.inf, F32), jnp.zeros((QK_PAD, TQ), F32)) for _ in range(2))
    carry = lax.fori_loop(0, n_full // 2, body, init)

    def finish(widths, odd):
        cr, s_ref = carry, sa_ref
        if odd:
            cr, s_ref = full_block(cr, n_full - 1, sa_ref, sb_ref), sb_ref
        width = widths * TQ
        key = lax.broadcasted_iota(jnp.int32, (width, TQ), 0)
        qry = lax.broadcasted_iota(jnp.int32, (width, TQ), 1)
        keep = key <= qry + (widths - 1) * TQ
        s = [jnp.where(keep, s_ref[hh, :width, :], -jnp.inf) for hh in range(2)]
        (_, a0), (_, a1) = update(cr, s, n_full, width)
        o0 = a0[:A_VDIM] / a0[A_VDIM:A_VDIM + 1]
        o1 = a1[:A_VDIM] / a1[A_VDIM:A_VDIM + 1]
        o_ref[...] = jnp.concatenate([o0, o1], axis=0).T.astype(o_ref.dtype)

    for widths in range(1, TK // TQ + 1):
        for odd in (False, True):
            here = jnp.logical_and(qi * TQ - n_full * TK == (widths - 1) * TQ, (n_full % 2 == 1) == odd)
            pl.when(here)(functools.partial(finish, widths, odd))


def _mla_prompt(q, k, v):
    t = q.shape[1]
    TQ, TK = ATTN_Q_BLOCK, ATTN_KV_BLOCK
    assert TK % TQ == 0 and t % TK == 0
    k_spec = pl.BlockSpec((2, t, QK_PAD), lambda p, i: (p, 0, 0), pipeline_mode=pl.Buffered(1))
    v_spec = pl.BlockSpec((2, QK_PAD, t), lambda p, i: (p, 0, 0), pipeline_mode=pl.Buffered(1))
    return pl.pallas_call(
        _mla_prompt_kernel, grid=(A_HEADS // 2, t // TQ),
        in_specs=[pl.BlockSpec((2, TQ, QK_PAD), lambda p, i: (p, i, 0)), k_spec, v_spec],
        out_specs=pl.BlockSpec((TQ, LANES), lambda p, i: (i, p)),
        out_shape=jax.ShapeDtypeStruct((t, A_HEADS * A_VDIM), BF16),
        scratch_shapes=[pltpu.VMEM((2, TK, TQ), F32), pltpu.VMEM((2, TK, TQ), F32)],
        compiler_params=_params("parallel", "arbitrary"), name="mla_prompt",
    )(q, k, v)


def _mla_decode_kernel(layer, pt_ref, qabs_ref, q_ref, cnew_ref, knew_ref, wuv_ref, ckv_hbm, kpe_hbm,
                       o_ref, cbuf, kbuf, kcat, pcat, csem, ksem):
    P = DECODE_PAGES
    n = pl.program_id(0)
    n_steps = pt_ref.shape[1] // P
    page = cbuf.shape[2]

    def page_copies(pid, slot, i):
        return (pltpu.make_async_copy(ckv_hbm.at[layer, pid], cbuf.at[slot, i], csem.at[slot]),
                pltpu.make_async_copy(kpe_hbm.at[layer, pid], kbuf.at[slot, i], ksem.at[slot]))

    def start(req, step, slot):
        for i in range(P):
            for cp in page_copies(pt_ref[req, step * P + i], slot, i):
                cp.start()

    def wait(slot):
        for i in range(P):
            for cp in page_copies(0, slot, i):
                cp.wait()

    ahead = DECODE_SLOTS - 1

    @pl.when(n == 0)
    def _():
        for step in range(ahead):
            start(0, step, step % DECODE_SLOTS)

    qa = qabs_ref[...]
    qpe = q_ref[:, A_NOPE:A_NOPE + A_ROPE]
    m = jnp.full((A_HEADS, 1), -jnp.inf, F32)
    l = jnp.zeros((A_HEADS, 1), F32)
    acc = jnp.zeros((A_HEADS, KV_RANK), F32)
    pending = None
    for step in range(n_steps):
        fetch = step + ahead
        if fetch < n_steps:
            start(n, fetch, fetch % DECODE_SLOTS)
        else:
            @pl.when(n + 1 < pl.num_programs(0))
            def _():
                start(n + 1, fetch - n_steps, fetch % DECODE_SLOTS)
        wait(step % DECODE_SLOTS)
        slot = step % 2
        for i in range(P):
            kcat[slot, i * page:(i + 1) * page, :] = cbuf[step % DECODE_SLOTS, i].astype(BF16)
            pcat[slot, :, i * page:(i + 1) * page] = kbuf[step % DECODE_SLOTS, i].astype(BF16)
        s = _dot_nt(qa, kcat[slot]) + jnp.dot(qpe, pcat[slot], preferred_element_type=F32)
        if pending is not None:
            pr_prev, alpha_prev, slot_prev = pending
            acc = acc * alpha_prev + jnp.dot(pr_prev, kcat[slot_prev], preferred_element_type=F32)
        m_new = jnp.maximum(m, jnp.max(s, axis=-1, keepdims=True))
        alpha = jnp.exp2(m - m_new)
        pr = jnp.exp2(s - m_new)
        l = l * alpha + jnp.sum(pr, axis=-1, keepdims=True)
        m = m_new
        pending = (pr.astype(BF16), alpha, slot)
    pr_prev, alpha_prev, slot_prev = pending
    acc = acc * alpha_prev + jnp.dot(pr_prev, kcat[slot_prev], preferred_element_type=F32)

    cnew = cnew_ref[...]
    s_self = (jnp.sum(qa * cnew, axis=-1, keepdims=True)
              + jnp.sum(qpe.astype(F32) * knew_ref[...], axis=-1, keepdims=True))
    m_fin = jnp.maximum(m, s_self)
    al = jnp.exp2(m - m_fin)
    p_self = jnp.exp2(s_self - m_fin)
    o_lat = (acc * al + p_self * cnew) / (l * al + p_self)
    res = _dot(o_lat, wuv_ref[...])
    hrow = lax.broadcasted_iota(jnp.int32, res.shape, 0)
    hcol = lax.broadcasted_iota(jnp.int32, res.shape, 1) // A_VDIM
    o_ref[...] = jnp.sum(jnp.where(hrow == hcol, res, 0.0), axis=0, keepdims=True).astype(o_ref.dtype)


def _mla_decode(page_table, qabs, q, ckv_new, kpe_new, wuv, cache_ckv, cache_kpe_t, layer):
    n, n_pages = page_table.shape
    P = DECODE_PAGES
    page = cache_ckv.shape[2]
    slots = DECODE_SLOTS
    assert (n_pages // P) % slots == 0, "buffer slots rotate per step and must line up across requests"
    req = lambda shape: pl.BlockSpec((None,) + shape, lambda b, pt: (b, 0, 0))
    hbm = pl.BlockSpec(memory_space=pl.ANY)
    grid_spec = pltpu.PrefetchScalarGridSpec(
        num_scalar_prefetch=1, grid=(n,),
        in_specs=[req((A_HEADS, KV_RANK)), req((A_HEADS, QK_PAD)), req((1, KV_RANK)), req((1, A_ROPE)),
                  pl.BlockSpec(wuv.shape, lambda b, pt: (0, 0)), hbm, hbm],
        out_specs=req((1, A_HEADS * A_VDIM)),
        scratch_shapes=[pltpu.VMEM((slots, P, page, KV_RANK), F32), pltpu.VMEM((slots, P, A_ROPE, page), F32),
                        pltpu.VMEM((2, P * page, KV_RANK), BF16), pltpu.VMEM((2, A_ROPE, P * page), BF16),
                        pltpu.SemaphoreType.DMA((slots,)), pltpu.SemaphoreType.DMA((slots,))])
    out = pl.pallas_call(
        functools.partial(_mla_decode_kernel, layer), grid_spec=grid_spec,
        out_shape=jax.ShapeDtypeStruct((n, 1, A_HEADS * A_VDIM), BF16),
        compiler_params=_params("arbitrary"), name="mla_decode",
    )(page_table, qabs, q, ckv_new[:, None, :], kpe_new[:, None, :], wuv, cache_ckv, cache_kpe_t)
    return out[:, 0, :]


def _mix_out_kernel(shared_mem, x_ref, oa_ref, y_ref, r_ref, k2_ref, v_ref, g_ref, gpre_ref, wga_ref,
                    wgb_ref, rk_ref, lnw_ref, lnb_ref, e_ref, wpa_ref, wpb_ref, wout_ref, gpost_ref,
                    gmem_ref, wmq_ref, *rest):
    x = x_ref[...]
    h = _rms(x, gpre_ref[...]).astype(BF16)
    gate_a = _sigmoid(jnp.dot(h, wga_ref[...], preferred_element_type=F32))
    gate_b = _sigmoid(jnp.dot(h, wgb_ref[...], preferred_element_type=F32))
    e = e_ref[...]
    y = y_ref[...]
    inv = 1.0 / B_HDIM
    d = y - _seg_sum(y, e) * inv
    var = _seg_sum(d * d, e) * inv
    yn = d * lax.rsqrt(var + GN_EPS) * lnw_ref[...] + lnb_ref[...]
    v = v_ref[...]
    bonus = _seg_sum(r_ref[...] * k2_ref[...] * rk_ref[...], e) * v
    ob = (yn + bonus) * g_ref[...]
    merged = gate_a * _dot(oa_ref[...], wpa_ref[...]) + gate_b * _dot(ob, wpb_ref[...])
    x1 = x + _rms(_dot(merged, wout_ref[...]), gpost_ref[...])
    qm = (_dot(_rms(x1, gmem_ref[...]), wmq_ref[...]) * ((x.shape[-1] // X_HEADS) ** -0.5)).astype(BF16)
    if shared_mem:
        k_ref, v_ref, x1_ref, om_ref = rest
        _mem_attn_shared(qm, k_ref, v_ref, om_ref)
    else:
        x1_ref, qm_ref = rest
        qm_ref[...] = qm
    x1_ref[...] = x1


def _mix_out(x, oa, y, r, k2, v, g, *weights, shared_mem=()):
    m, d = x.shape
    tm = min(ROW_TILE, m)
    wide = _rows(tm, d)
    half = _rows(tm, B_WIDTH)
    consts = tuple(weights) + tuple(shared_mem)
    return pl.pallas_call(
        functools.partial(_mix_out_kernel, bool(shared_mem)), grid=(m // tm,),
        in_specs=[wide] + [half] * 6 + [_const(w.shape) for w in consts],
        out_specs=[wide, wide],
        out_shape=[jax.ShapeDtypeStruct((m, d), F32), jax.ShapeDtypeStruct((m, d), BF16)],
        compiler_params=_params("parallel"), name="mix_out",
    )(x, oa, y, r, k2, v, g, *consts)


def _mem_kv_kernel(mem_ref, g_ref, wk_ref, wv_ref, k_ref, v_ref):
    h = _rms(mem_ref[...], g_ref[...]).astype(BF16)
    k_ref[...] = jnp.dot(h, wk_ref[...], preferred_element_type=F32)
    v_ref[...] = jnp.dot(h, wv_ref[...], preferred_element_type=F32)


def _mem_kv(mem, g, wk, wv):
    m, d = mem.shape
    o = jax.ShapeDtypeStruct((m, d), F32)
    return pl.pallas_call(
        _mem_kv_kernel, grid=(1,),
        in_specs=[_full(mem.shape), _full(g.shape), _full(wk.shape), _full(wv.shape)],
        out_specs=[_full((m, d))] * 2, out_shape=[o, o],
        compiler_params=_params("arbitrary"), name="mem_kv",
    )(mem, g, wk, wv)


def _softmax_rows(s):
    pr = jnp.exp(s - jnp.max(s, axis=-1, keepdims=True))
    return pr / jnp.sum(pr, axis=-1, keepdims=True)


def _mem_attn_shared(q, k_ref, v_ref, o_ref):
    hd = q.shape[-1] // X_HEADS
    heads = [slice(hh * hd, (hh + 1) * hd) for hh in range(X_HEADS)]
    s = [_dot_nt(q[:, sl], k_ref[:, sl]) for sl in heads]
    pr = [_softmax_rows(sh) for sh in s]
    o = [_dot(pr[hh], v_ref[:, heads[hh]]) for hh in range(X_HEADS)]
    for hh in range(X_HEADS):
        o_ref[:, heads[hh]] = o[hh].astype(o_ref.dtype)


def _mem_attn_rows_kernel(q_ref, k_ref, v_ref, o_ref):
    for i in range(q_ref.shape[0]):
        prod = k_ref[i] * q_ref[i][None]
        s = jnp.sum(jnp.sum(prod, axis=1), axis=-1, keepdims=True)
        pr = jnp.exp(s - jnp.max(s, axis=0, keepdims=True))
        w = pr / jnp.sum(pr, axis=0, keepdims=True)
        o_ref[i] = jnp.sum(w[:, None] * v_ref[i], axis=0)


def _mem_attn_rows(q, mk, mv):
    n, mem, heads, hd = mk.shape
    nb = 4
    tiles = hd // LANES
    tiled = lambda x, lead: jnp.swapaxes(x.reshape(lead + (heads, tiles, LANES)), -2, -3)
    qspec = pl.BlockSpec((nb, tiles, heads, LANES), lambda i: (i, 0, 0, 0))
    kspec = pl.BlockSpec((nb, mem, tiles, heads, LANES), lambda i: (i, 0, 0, 0, 0))
    out = pl.pallas_call(
        _mem_attn_rows_kernel, grid=(n // nb,),
        in_specs=[qspec, kspec, kspec], out_specs=qspec,
        out_shape=jax.ShapeDtypeStruct((n, tiles, heads, LANES), F32),
        compiler_params=_params("parallel"), name="mem_attn_rows",
    )(tiled(q.astype(F32), (n,)), tiled(mk, (n, mem)), tiled(mv, (n, mem)))
    return jnp.swapaxes(out, 1, 2).reshape(n, heads * hd).astype(BF16)


def _tail_kernel(x1_ref, om_ref, wmo_ref, gpm_ref, gmlp_ref, wup_ref, wdn_ref, gpost_ref, y_ref):
    x2 = x1_ref[...] + _rms(jnp.dot(om_ref[...], wmo_ref[...], preferred_element_type=F32), gpm_ref[...])
    h = _rms(x2, gmlp_ref[...]).astype(BF16)
    u = jnp.maximum(jnp.dot(h, wup_ref[...], preferred_element_type=F32), 0.0)
    ff = jnp.dot((u * u).astype(BF16), wdn_ref[...], preferred_element_type=F32)
    y_ref[...] = x2 + _rms(ff, gpost_ref[...])


def _tail(x1, om, *weights):
    m, d = x1.shape
    tm = min(ROW_TILE, m)
    return pl.pallas_call(
        _tail_kernel, grid=(m // tm,),
        in_specs=[_rows(tm, d), _rows(tm, d)] + [_const(w.shape) for w in weights],
        out_specs=_rows(tm, d), out_shape=jax.ShapeDtypeStruct((m, d), F32),
        compiler_params=_params("parallel"), name="tail",
    )(x1, om, *weights)


def _rot_cols(w):
    half = w.shape[-1] // 2
    return jnp.concatenate([-w[..., half:], w[..., :half]], axis=-1)


def _rope_tables(pos):
    inv = ROPE_THETA ** (-jnp.arange(0, A_ROPE, 2, dtype=F32) / A_ROPE)
    ang = inv[:, None] * pos.astype(F32)[None, :]
    z_lo = jnp.zeros((A_NOPE, pos.shape[0]), F32)
    z_hi = jnp.zeros((QK_PAD - A_NOPE - A_ROPE, pos.shape[0]), F32)
    cos, sin = jnp.cos(ang), jnp.sin(ang)
    return (jnp.concatenate([z_lo, cos, cos, z_hi], axis=0),
            jnp.concatenate([z_lo, sin, sin, z_hi], axis=0))


def _prep_layer(l, w_in, w_uq, w_uk, w_uv, rw_decay_up, rw_a_up, rw_g_up):
    d = w_in.shape[1]
    wi = w_in[l]
    o_rw = Q_RANK + KV_RANK + A_ROPE
    o_ga = o_rw + RW_COLS
    w_kpe = wi[:, Q_RANK + KV_RANK:o_rw]
    z = lambda n: jnp.zeros((d, n), F32)
    pad_hi = QK_PAD - A_NOPE - A_ROPE
    wlat = jnp.concatenate([wi[:, :Q_RANK + KV_RANK], z(A_NOPE), w_kpe, z(pad_hi),
                            z(A_NOPE), _rot_cols(w_kpe), z(pad_hi)], axis=1).astype(BF16)
    wrw = jnp.pad(wi[:, o_rw:o_ga], ((0, 0), (0, RW_PAD - RW_COLS))).astype(BF16)
    wga = wi[:, o_ga:o_ga + d].astype(BF16)
    wgb = wi[:, o_ga + d:].astype(BF16)
    uq = w_uq[l].reshape(Q_RANK, A_HEADS, A_NOPE + A_ROPE)
    nope, pe = uq[..., :A_NOPE], uq[..., A_NOPE:]
    zq = lambda n: jnp.zeros((Q_RANK, A_HEADS, n), F32)
    wqa = jnp.concatenate([nope, pe, zq(pad_hi)], axis=-1).reshape(Q_RANK, -1).astype(BF16)
    wqb = jnp.concatenate([zq(A_NOPE), _rot_cols(pe), zq(pad_hi)], axis=-1).reshape(Q_RANK, -1).astype(BF16)
    pad_cols = lambda w: jnp.pad(w, ((0, 0), (0, 0), (0, QK_PAD - w.shape[-1]))).reshape(KV_RANK, -1).astype(BF16)
    wk_cols = pad_cols(w_uk[l])
    wv_cols = pad_cols(w_uv[l])
    wk_rows = jnp.pad(jnp.transpose(w_uk[l], (1, 2, 0)), ((0, 0), (0, QK_PAD - A_NOPE), (0, 0))).astype(BF16)
    wv = w_uv[l].reshape(KV_RANK, -1).astype(BF16)
    wl = jnp.zeros((LORA_IN, 3 * B_WIDTH), F32)
    wl = wl.at[:DECAY_LORA, :B_WIDTH].set(rw_decay_up[l])
    wl = wl.at[DECAY_LORA:DECAY_LORA + AAA_LORA, B_WIDTH:2 * B_WIDTH].set(rw_a_up[l])
    wl = wl.at[DECAY_LORA + AAA_LORA:, 2 * B_WIDTH:].set(rw_g_up[l])
    return wlat, wrw, wga, wgb, wqa, wqb, wk_cols, wv_cols, wk_rows, wv, wl.astype(BF16)


def kernel(x_prompt, x_sample, mem_prompt, cache_ckv, cache_kpe, state_wkv, state_shift, cache_mem_k, cache_mem_v, page_table, norm_pre_mix, w_in, q_norm, w_uq, kv_norm, w_uk, w_uv, rw_mu, rw_w0, rw_decay_up, rw_a0, rw_a_up, rw_g_up, rw_k_k, rw_k_a, rw_r_k, rw_ln_w, rw_ln_b, w_proj_a, w_proj_b, w_out, norm_post_mix, norm_pre_mem, mem_norm, w_mq, w_mk, w_mv, w_mo, norm_post_mem, norm_pre_mlp, w_ff_up, w_ff_down, norm_post_mlp):
    depth = w_in.shape[0]
    n_p, seq, d = x_prompt.shape
    n_s, dec_seq, _ = x_sample.shape
    assert n_p == 1 and dec_seq == 1, "one prompt sequence and one new token per decode request"
    past_len = page_table.shape[1] * cache_ckv.shape[2]
    ck_p, sk_p = _rope_tables(jnp.arange(seq))
    ck_s, sk_s = _rope_tables(jnp.full((n_s,), past_len))
    seg = jnp.arange(2 * LANES) // B_HDIM
    e = (seg[:, None] == seg[None, :]).astype(BF16)
    cache_kpe_t = jnp.swapaxes(cache_kpe, 2, 3)
    row = lambda p, l: p[l].reshape(1, -1)
    bf = lambda p, l: p[l].astype(BF16)

    y_p = x_prompt.reshape(seq, d)
    y_s = x_sample.reshape(n_s, d)
    outs = [[] for _ in range(10)]
    for l in range(depth):
        wlat, wrw, wga, wgb, wqa, wqb, wk_cols, wv_cols, wk_rows, wv, wl = _prep_layer(
            l, w_in, w_uq, w_uk, w_uv, rw_decay_up, rw_a_up, rw_g_up)
        proj_w = (row(norm_pre_mix, l), wlat, wrw, row(q_norm, l), row(kv_norm, l), wqa, wqb)
        prep_w = (row(rw_mu, l), row(rw_w0, l), row(rw_a0, l), wl, row(rw_k_k, l), row(rw_k_a, l), e)
        mix_w = (row(norm_pre_mix, l), wga, wgb, row(rw_r_k, l), row(rw_ln_w, l), row(rw_ln_b, l), e,
                 bf(w_proj_a, l), bf(w_proj_b, l), bf(w_out, l), row(norm_post_mix, l),
                 row(norm_pre_mem, l), bf(w_mq, l))
        tail_w = (bf(w_mo, l), row(norm_post_mem, l), row(norm_pre_mlp, l), bf(w_ff_up, l),
                  bf(w_ff_down, l), row(norm_post_mlp, l))

        q, ckv_p, kpe_p, rw_p, k, v, r_, lw_, k2_, v_, na_, b_, g_ = _proj_in(
            y_p, *proj_w, ck_p, sk_p, wk_cols, wv_cols, jnp.zeros((1, RW_COLS), F32), *prep_w, prompt=True)
        oa_p = _mla_prompt(q, k, v)
        yb_p, wkv_p = _rwkv_chunked(r_, lw_, k2_, v_, na_, b_)
        mk_p, mv_p = _mem_kv(mem_prompt.reshape(-1, d), row(mem_norm, l), bf(w_mk, l), bf(w_mv, l))
        x1_p, om_p = _mix_out(y_p, oa_p, yb_p, r_, k2_, v_, g_, *mix_w, shared_mem=(mk_p, mv_p))
        y_p = _tail(x1_p, om_p, *tail_w)

        q, ckv_s, kpe_s, rw_s, qabs, r_, lw_, k2_, v_, na_, b_, g_ = _proj_in(
            y_s, *proj_w, ck_s, sk_s, wk_rows, wv, state_shift[l], *prep_w, prompt=False)
        oa_s = _mla_decode(page_table, jnp.transpose(qabs, (1, 0, 2)), jnp.transpose(q, (1, 0, 2)),
                           ckv_s, kpe_s, wv, cache_ckv, cache_kpe_t, l)
        yb_s, wkv_s = _rwkv_step(state_wkv[l], r_, lw_, k2_, v_, na_, b_)
        x1_s, qm_s = _mix_out(y_s, oa_s, yb_s, r_, k2_, v_, g_, *mix_w)
        om_s = _mem_attn_rows(qm_s, cache_mem_k[l], cache_mem_v[l])
        y_s = _tail(x1_s, om_s, *tail_w)

        mem_heads = (n_p, -1, X_HEADS, d // X_HEADS)
        layer_outs = (ckv_p.reshape(n_p, seq, KV_RANK), kpe_p.reshape(n_p, seq, A_ROPE),
                      wkv_p[None], rw_p[-1:], mk_p.reshape(mem_heads), mv_p.reshape(mem_heads),
                      ckv_s.reshape(n_s, 1, KV_RANK), kpe_s.reshape(n_s, 1, A_ROPE), wkv_s, rw_s)
        for lst, val in zip(outs, layer_outs):
            lst.append(val)
    return (y_p.reshape(n_p, seq, d), y_s.reshape(n_s, 1, d)) + tuple(jnp.stack(o) for o in outs)
```

```python
import functools
import math

import jax
import jax.numpy as jnp
from jax import lax
from jax.experimental import pallas as pl
from jax.experimental.pallas import tpu as pltpu

F32 = jnp.float32
BF16 = jnp.bfloat16

A_HEADS = 8
A_NOPE = 64
A_ROPE = 32
A_VDIM = 64
Q_RANK = 384
KV_RANK = 256
ROPE_THETA = 10000.0
B_HEADS = 8
B_HDIM = 64
B_WIDTH = B_HEADS * B_HDIM
DECAY_LORA = 64
AAA_LORA = 64
GATE_LORA = 160
LORA_IN = DECAY_LORA + AAA_LORA + GATE_LORA
GN_EPS = 64e-5
X_HEADS = 4
NORM_EPS = 1e-6
RW_COLS = 3 * B_WIDTH + LORA_IN

LANES = 128
SUBLANES = 8
QK_PAD = 128
RW_PAD = 15 * LANES
RWKV_CHUNK = 64
RWKV_BLOCK = 512
RWKV_GROUP = 2
ATTN_Q_BLOCK = 1024
ATTN_KV_BLOCK = 1024
DECODE_PAGES = 32
DECODE_SLOTS = 4
ROW_TILE = 512
VMEM_LIMIT = 48 * 1024 * 1024
QK_SCALE = (A_NOPE + A_ROPE) ** -0.5 * math.log2(math.e)


def _params(*sem):
    return pltpu.CompilerParams(dimension_semantics=sem, vmem_limit_bytes=VMEM_LIMIT)


def _full(shape):
    zeros = (0,) * len(shape)
    return pl.BlockSpec(shape, lambda *_: zeros)


def _const(shape):
    zeros = (0,) * len(shape)
    return pl.BlockSpec(shape, lambda *_: zeros, pipeline_mode=pl.Buffered(1))


def _rows(tm, width):
    return pl.BlockSpec((tm, width), lambda i: (i, 0))


def _rms(x, g):
    return x * lax.rsqrt(jnp.mean(x * x, axis=-1, keepdims=True) + NORM_EPS) * g


def _dot(a, b):
    return jnp.dot(a.astype(BF16), b.astype(BF16), preferred_element_type=F32)


def _dot_nt(a, b):
    return lax.dot_general(a.astype(BF16), b.astype(BF16), (((1,), (1,)), ((), ())),
                           preferred_element_type=F32)


def _dot_tn(a, b):
    return lax.dot_general(a.astype(BF16), b.astype(BF16), (((0,), (0,)), ((), ())),
                           preferred_element_type=F32)


def _sigmoid(x):
    return 1.0 / (1.0 + jnp.exp(-x))


def _seg_sum(x, e):
    hi = x.astype(BF16)
    lo = (x - hi.astype(F32)).astype(BF16)
    w = e.shape[0]
    parts = [jnp.dot(hi[:, c:c + w], e, preferred_element_type=F32)
             + jnp.dot(lo[:, c:c + w], e, preferred_element_type=F32) for c in range(0, x.shape[1], w)]
    return jnp.concatenate(parts, axis=1)


def _rwkv_inputs(rw, shifted, mu, w0, a0, wl, k_k, k_a, e):
    xs = rw + (shifted - rw) * mu
    r = xs[:, :B_WIDTH]
    k = xs[:, B_WIDTH:2 * B_WIDTH]
    v = xs[:, 2 * B_WIDTH:3 * B_WIDTH]
    lo = xs[:, 3 * B_WIDTH:]
    lane = lax.broadcasted_iota(jnp.int32, lo.shape, 1)
    t = jnp.where(lane < DECAY_LORA, jnp.tanh(lo),
                  jnp.where(lane < DECAY_LORA + AAA_LORA, lo, _sigmoid(lo)))
    l3 = jnp.dot(t.astype(BF16), wl, preferred_element_type=F32)
    z = -(w0 + l3[:, :B_WIDTH])
    softplus = jnp.maximum(z, 0.0) + jnp.log(1.0 + jnp.exp(-jnp.abs(z)))
    w = -softplus - 0.5
    a = _sigmoid(a0 + l3[:, B_WIDTH:2 * B_WIDTH])
    kkr = k * k_k
    kk = kkr / jnp.maximum(jnp.sqrt(_seg_sum(kkr * kkr, e)), 1e-12)
    lw = -jnp.exp(w)
    return r, lw, k * (1.0 + (a - 1.0) * k_a), v, -kk, kk * a, l3[:, 2 * B_WIDTH:]


def _proj_in_kernel(prompt, x_ref, g_ref, wlat_ref, wrw_ref, qn_ref, kvn_ref, wqa_ref, wqb_ref,
                    ck_ref, sk_ref, wk_ref, wv_ref, first_ref, mu_ref, w0_ref, a0_ref, wl_ref, kk_ref,
                    ka_ref, e_ref, *rest):
    if prompt:
        q_ref, ckv_ref, kpe_ref, rw_ref, k_ref, v_ref = rest[:6]
        prev_ref = rest[-1]
    else:
        q_ref, ckv_ref, kpe_ref, rw_ref, qabs_ref = rest[:5]
    rwkv_refs = rest[6:13] if prompt else rest[5:12]
    if prompt:
        @pl.when(pl.program_id(0) == 0)
        def _():
            prev_ref[...] = jnp.broadcast_to(first_ref[...], prev_ref.shape)

    h = _rms(x_ref[...], g_ref[...]).astype(BF16)
    lat = jnp.dot(h, wlat_ref[...], preferred_element_type=F32)
    cq = lat[:, :Q_RANK]
    ckv = _rms(lat[:, Q_RANK:Q_RANK + KV_RANK], kvn_ref[...])
    ck = ck_ref[...].T
    sk = sk_ref[...].T
    kpe = lat[:, 640:768] * ck + lat[:, 768:896] * sk
    ckv_ref[...] = ckv
    kpe_ref[...] = kpe[:, A_NOPE:A_NOPE + A_ROPE]
    rw = jnp.dot(h, wrw_ref[...], preferred_element_type=F32)[:, :RW_COLS]
    if prompt:
        ridx = lax.broadcasted_iota(jnp.int32, rw.shape, 0)
        shifted = jnp.where(ridx == 0, prev_ref[SUBLANES - 1:SUBLANES, :], pltpu.roll(rw, 1, axis=0))
        last_rows = rw[rw.shape[0] - SUBLANES:, :]
        prev_ref[...] = last_rows
        rw_ref[...] = last_rows
    else:
        shifted = first_ref[...]
        rw_ref[...] = rw
    vals = _rwkv_inputs(rw, shifted, mu_ref[...], w0_ref[...], a0_ref[...], wl_ref[...], kk_ref[...],
                        ka_ref[...], e_ref[...])
    for ref, val in zip(rwkv_refs, vals):
        ref[...] = val
    qn = _rms(cq, qn_ref[...]).astype(BF16)
    qa = jnp.dot(qn, wqa_ref[...], preferred_element_type=F32)
    qb = jnp.dot(qn, wqb_ref[...], preferred_element_type=F32)
    lane = lax.broadcasted_iota(jnp.int32, ck.shape, 1)
    cq_t = QK_SCALE * (ck + jnp.where(lane < A_NOPE, 1.0, 0.0))
    sq_t = QK_SCALE * sk
    ckv_b = ckv.astype(BF16)
    if prompt:
        kn = jnp.dot(ckv_b, wk_ref[...], preferred_element_type=F32)
        vn = jnp.dot(ckv_b, wv_ref[...], preferred_element_type=F32)
        ones_col = jnp.where(lane == A_VDIM, 1.0, 0.0)
    for hh in range(A_HEADS):
        sl = slice(hh * QK_PAD, (hh + 1) * QK_PAD)
        qh = (qa[:, sl] * cq_t + qb[:, sl] * sq_t).astype(BF16)
        q_ref[hh] = qh
        if prompt:
            k_ref[hh] = (kn[:, sl] + kpe).astype(BF16)
            v_ref[hh] = (vn[:, sl] + ones_col).T.astype(BF16)
        else:
            qabs_ref[hh] = jnp.dot(qh, wk_ref[hh], preferred_element_type=F32)


def _proj_in(x, g, wlat, wrw, qn, kvn, wqa, wqb, ck, sk, wk, wv, first, *rwkv_w, prompt):
    m, d = x.shape
    tm = min(ROW_TILE, m)
    heads_out = lambda w, dt: jax.ShapeDtypeStruct((A_HEADS, m, w), dt)
    heads_spec = lambda w: pl.BlockSpec((A_HEADS, tm, w), lambda i: (0, i, 0))
    rw_rows = SUBLANES if prompt else m
    out_shape = [heads_out(QK_PAD, BF16), jax.ShapeDtypeStruct((m, KV_RANK), F32),
                 jax.ShapeDtypeStruct((m, A_ROPE), F32), jax.ShapeDtypeStruct((rw_rows, RW_COLS), F32)]
    out_specs = [heads_spec(QK_PAD), _rows(tm, KV_RANK), _rows(tm, A_ROPE),
                 _full((SUBLANES, RW_COLS)) if prompt else _rows(tm, RW_COLS)]
    if prompt:
        out_shape += [heads_out(QK_PAD, BF16), jax.ShapeDtypeStruct((A_HEADS, QK_PAD, m), BF16)]
        out_specs += [heads_spec(QK_PAD), pl.BlockSpec((A_HEADS, QK_PAD, tm), lambda i: (0, 0, i))]
    else:
        out_shape += [heads_out(KV_RANK, F32)]
        out_specs += [heads_spec(KV_RANK)]
    out_shape += [jax.ShapeDtypeStruct((m, B_WIDTH), F32)] * 7
    out_specs += [_rows(tm, B_WIDTH)] * 7
    first_spec = _full(first.shape) if prompt else _rows(tm, RW_COLS)
    table_spec = pl.BlockSpec((LANES, tm), lambda i: (0, i))
    return pl.pallas_call(
        functools.partial(_proj_in_kernel, prompt),
        grid=(m // tm,),
        in_specs=[_rows(tm, d), _const(g.shape), _const(wlat.shape), _const(wrw.shape), _const(qn.shape),
                  _const(kvn.shape), _const(wqa.shape), _const(wqb.shape), table_spec,
                  table_spec, _const(wk.shape), _const(wv.shape), first_spec]
                 + [_const(w.shape) for w in rwkv_w],
        out_specs=out_specs, out_shape=out_shape,
        scratch_shapes=[pltpu.VMEM((SUBLANES, RW_COLS), F32)] if prompt else [],
        compiler_params=_params("arbitrary"), name="proj_in",
    )(x, g, wlat, wrw, qn, kvn, wqa, wqb, ck, sk, wk, wv, first, *rwkv_w)


def _rwkv_chunk_kernel(r_ref, lw_ref, k2_ref, v_ref, na_ref, b_ref, y_ref, sout_ref, s_ref):
    C = RWKV_CHUNK
    n_chunks = r_ref.shape[0] // C
    pairs = B_HEADS // 2

    @pl.when(pl.program_id(0) == 0)
    def _():
        s_ref[...] = jnp.zeros_like(s_ref)

    row = lax.broadcasted_iota(jnp.int32, (2 * C, 2 * C), 0)
    col = lax.broadcasted_iota(jnp.int32, (2 * C, 2 * C), 1)
    tok_r = jnp.bitwise_and(row, C - 1)
    tok_c = jnp.bitwise_and(col, C - 1)
    strict = tok_c < tok_r
    incl = tok_c <= tok_r
    eye = jnp.where(row == col, 1.0, 0.0)
    crow = lax.broadcasted_iota(jnp.int32, (C, C), 0)
    ccol = lax.broadcasted_iota(jnp.int32, (C, C), 1)
    cum = jnp.where(ccol <= crow, 1.0, 0.0).astype(BF16)
    head0 = lax.broadcasted_iota(jnp.int32, (C, LANES), 1) < B_HDIM

    def stack(x):
        return jnp.concatenate([jnp.where(head0, x, 0.0), jnp.where(head0, 0.0, x)], axis=0)

    ops, X, Arb, AV, VK = {}, {}, {}, {}, {}
    S = [s_ref[p] for p in range(pairs)]

    def independent(chunks):
        units = [(ci, p) for ci in chunks for p in range(pairs)]
        G_all = {}
        for ci in chunks:
            lw = lw_ref[ci * C:(ci + 1) * C, :]
            l_hi = lw.astype(BF16)
            rem = lw - l_hi.astype(F32)
            l_mid = rem.astype(BF16)
            l_lo = (rem - l_mid.astype(F32)).astype(BF16)
            g3 = jnp.dot(cum, jnp.concatenate([l_hi, l_mid, l_lo], axis=1), preferred_element_type=F32)
            G_all[ci] = g3[:, :B_WIDTH] + g3[:, B_WIDTH:2 * B_WIDTH] + g3[:, 2 * B_WIDTH:]
        yield
        for ci, p in units:
            rows = slice(ci * C, (ci + 1) * C)
            sl = slice(p * LANES, (p + 1) * LANES)
            lw = lw_ref[rows, sl]
            G = G_all[ci][:, sl]
            GC = G[C - 1:C, :]
            e_pos = jnp.exp(G)
            e_neg = jnp.exp(-G)
            e_prev = jnp.exp(G - lw)
            e_tail = jnp.exp(GC - G)
            r, k2, v = r_ref[rows, sl], k2_ref[rows, sl], v_ref[rows, sl]
            na, b = na_ref[rows, sl], b_ref[rows, sl]
            ops[ci, p] = dict(
                AR=jnp.concatenate([stack(na * e_prev), stack(r * e_pos)], axis=0).astype(BF16),
                BK=jnp.concatenate([stack(b * e_neg), stack(k2 * e_neg)], axis=0).astype(BF16),
                Bh=stack(b * e_tail).astype(BF16), Kh=stack(k2 * e_tail).astype(BF16),
                Vs=stack(v).astype(BF16), decay=jnp.exp(GC))
        AA = {u: _dot_nt(ops[u]["AR"], ops[u]["BK"]) for u in units}
        yield
        AakArk, sq = {}, {}
        for u in units:
            aa = AA[u]
            a_ab = jnp.where(strict, aa[:2 * C, :2 * C], 0.0)
            AakArk[u] = jnp.concatenate([jnp.where(strict, aa[:2 * C, 2 * C:], 0.0),
                                         jnp.where(incl, aa[2 * C:, 2 * C:], 0.0)], axis=0).astype(BF16)
            Arb[u] = jnp.where(incl, aa[2 * C:, :2 * C], 0.0).astype(BF16)
            X[u] = eye + a_ab
            sq[u] = a_ab
        sq = {u: _dot(sq[u], sq[u]) for u in units}
        yield
        n = 2
        while 2 * n < C:
            both = {u: _dot(jnp.concatenate([X[u], sq[u]], axis=0), sq[u]) for u in units}
            yield
            for u in units:
                X[u] = X[u] + both[u][:2 * C]
                sq[u] = both[u][2 * C:]
            n *= 2
        last = {u: _dot(X[u], sq[u]) for u in units}
        yield
        for u in units:
            X[u] = (X[u] + last[u]).astype(BF16)
            AV[u] = _dot(AakArk[u], ops[u]["Vs"])
            VK[u] = _dot_tn(ops[u]["Vs"], ops[u]["Kh"])
        yield

    def dependent(chunks):
        for ci in chunks:
            SS = [_dot_nt(ops[ci, p]["AR"], S[p]) for p in range(pairs)]
            yield
            U = [_dot(X[ci, p], SS[p][:2 * C] + AV[ci, p][:2 * C]) for p in range(pairs)]
            yield
            YU = [_dot(Arb[ci, p], U[p]) for p in range(pairs)]
            UB = [_dot_tn(U[p], ops[ci, p]["Bh"]) for p in range(pairs)]
            yield
            for p in range(pairs):
                Y = SS[p][2 * C:] + YU[p] + AV[ci, p][2 * C:]
                y_ref[ci * C:(ci + 1) * C, p * LANES:(p + 1) * LANES] = Y[:C] + Y[C:]
                S[p] = S[p] * ops[ci, p]["decay"] + UB[p] + VK[ci, p]

    def interleave(*stages):
        live = list(stages)
        while live:
            for gen in list(live):
                if next(gen, live) is live:
                    live.remove(gen)

    groups = [list(range(g, min(g + RWKV_GROUP, n_chunks))) for g in range(0, n_chunks, RWKV_GROUP)]
    interleave(independent(groups[0]))
    for prev, cur in zip(groups[:-1], groups[1:]):
        interleave(independent(cur), dependent(prev))
    interleave(dependent(groups[-1]))
    for p in range(pairs):
        s_ref[p] = S[p]

    @pl.when(pl.program_id(0) == pl.num_programs(0) - 1)
    def _():
        sout_ref[...] = s_ref[...]


def _rwkv_chunked(r, lw, k2, v, na, b):
    t = r.shape[0]
    spec = _rows(RWKV_BLOCK, B_WIDTH)
    pairs = B_HEADS // 2
    y, s = pl.pallas_call(
        _rwkv_chunk_kernel, grid=(t // RWKV_BLOCK,),
        in_specs=[spec] * 6,
        out_specs=[spec, _full((pairs, LANES, LANES))],
        out_shape=[jax.ShapeDtypeStruct((t, B_WIDTH), F32),
                   jax.ShapeDtypeStruct((pairs, LANES, LANES), F32)],
        scratch_shapes=[pltpu.VMEM((pairs, LANES, LANES), F32)],
        compiler_params=_params("arbitrary"), name="rwkv_chunked",
    )(r, lw, k2, v, na, b)
    s = s.reshape(pairs, 2, B_HDIM, 2, B_HDIM)
    wkv = jnp.stack([s[:, 0, :, 0, :], s[:, 1, :, 1, :]], axis=1).reshape(B_HEADS, B_HDIM, B_HDIM)
    return y, wkv


def _rwkv_step_kernel(s_ref, r_ref, lw_ref, k2_ref, na_ref, b_ref, v_ref, sout_ref, y_ref):
    S = s_ref[...]
    sa = jnp.sum(S * na_ref[...], axis=1, keepdims=True)
    S = S * jnp.exp(lw_ref[...]) + sa * b_ref[...] + v_ref[...] * k2_ref[...]
    sout_ref[...] = S
    y_ref[...] = jnp.sum(S * r_ref[...], axis=1, keepdims=True)


def _rwkv_step(state, r, lw, k2, v, na, b):
    n = state.shape[0]
    keyed = lambda x: x.T.reshape(B_HEADS, 1, B_HDIM, n)
    kspec = pl.BlockSpec((None, 1, B_HDIM, n), lambda h: (h, 0, 0, 0))
    vspec = pl.BlockSpec((None, B_HDIM, 1, n), lambda h: (h, 0, 0, 0))
    sspec = pl.BlockSpec((None, B_HDIM, B_HDIM, n), lambda h: (h, 0, 0, 0))
    s_new, y = pl.pallas_call(
        _rwkv_step_kernel, grid=(B_HEADS,),
        in_specs=[sspec, kspec, kspec, kspec, kspec, kspec, vspec],
        out_specs=[sspec, vspec],
        out_shape=[jax.ShapeDtypeStruct((B_HEADS, B_HDIM, B_HDIM, n), F32),
                   jax.ShapeDtypeStruct((B_HEADS, B_HDIM, 1, n), F32)],
        compiler_params=_params("parallel"), name="rwkv_step",
    )(jnp.transpose(state, (1, 2, 3, 0)), keyed(r), keyed(lw), keyed(k2), keyed(na), keyed(b),
      v.T.reshape(B_HEADS, B_HDIM, 1, n))
    return y.reshape(B_WIDTH, n).T, jnp.transpose(s_new, (3, 0, 1, 2))


def _mla_prompt_kernel(q_ref, k_ref, v_ref, o_ref, sa_ref, sb_ref):
    qi = pl.program_id(1)
    TQ, TK = ATTN_Q_BLOCK, ATTN_KV_BLOCK
    q = (q_ref[0], q_ref[1])
    n_full = (qi * TQ) // TK

    def scores(ks, s_ref):
        start = pl.multiple_of(ks * TK, TK)
        for hh in range(2):
            s_ref[hh] = _dot_nt(k_ref[hh, pl.ds(start, TK), :], q[hh])

    def update(carry, s, ks, width):
        start = pl.multiple_of(ks * TK, TK)
        pr, alpha, m_out = [], [], []
        for hh in range(2):
            m_new = jnp.maximum(carry[hh][0], jnp.max(s[hh], axis=0, keepdims=True))
            alpha.append(jnp.exp2(carry[hh][0] - m_new))
            pr.append(jnp.exp2(s[hh] - m_new).astype(BF16))
            m_out.append(m_new)
        pv = [jnp.dot(v_ref[hh, :, pl.ds(start, width)], pr[hh], preferred_element_type=F32)
              for hh in range(2)]
        return tuple((m_out[hh], carry[hh][1] * alpha[hh] + pv[hh]) for hh in range(2))

    def full_block(carry, ks, s_ref, next_ref):
        scores(ks + 1, next_ref)
        return update(carry, [s_ref[hh] for hh in range(2)], ks, TK)

    def body(j, carry):
        carry = full_block(carry, 2 * j, sa_ref, sb_ref)
        return full_block(carry, 2 * j + 1, sb_ref, sa_ref)

    scores(0, sa_ref)
    init = tuple((jnp.full((1, TQ), -jnp.inf, F32), jnp.zeros((QK_PAD, TQ), F32)) for _ in range(2))
    carry = lax.fori_loop(0, n_full // 2, body, init)

    def finish(widths, odd):
        cr, s_ref = carry, sa_ref
        if odd:
            cr, s_ref = full_block(cr, n_full - 1, sa_ref, sb_ref), sb_ref
        width = widths * TQ
        key = lax.broadcasted_iota(jnp.int32, (width, TQ), 0)
        qry = lax.broadcasted_iota(jnp.int32, (width, TQ), 1)
        keep = key <= qry + (widths - 1) * TQ
        s = [jnp.where(keep, s_ref[hh, :width, :], -jnp.inf) for hh in range(2)]
        (_, a0), (_, a1) = update(cr, s, n_full, width)
        o0 = a0[:A_VDIM] / a0[A_VDIM:A_VDIM + 1]
        o1 = a1[:A_VDIM] / a1[A_VDIM:A_VDIM + 1]
        o_ref[...] = jnp.concatenate([o0, o1], axis=0).T.astype(o_ref.dtype)

    for widths in range(1, TK // TQ + 1):
        for odd in (False, True):
            here = jnp.logical_and(qi * TQ - n_full * TK == (widths - 1) * TQ, (n_full % 2 == 1) == odd)
            pl.when(here)(functools.partial(finish, widths, odd))


def _mla_prompt(q, k, v):
    t = q.shape[1]
    TQ, TK = ATTN_Q_BLOCK, ATTN_KV_BLOCK
    assert TK % TQ == 0 and t % TK == 0
    k_spec = pl.BlockSpec((2, t, QK_PAD), lambda p, i: (p, 0, 0), pipeline_mode=pl.Buffered(1))
    v_spec = pl.BlockSpec((2, QK_PAD, t), lambda p, i: (p, 0, 0), pipeline_mode=pl.Buffered(1))
    return pl.pallas_call(
        _mla_prompt_kernel, grid=(A_HEADS // 2, t // TQ),
        in_specs=[pl.BlockSpec((2, TQ, QK_PAD), lambda p, i: (p, i, 0)), k_spec, v_spec],
        out_specs=pl.BlockSpec((TQ, LANES), lambda p, i: (i, p)),
        out_shape=jax.ShapeDtypeStruct((t, A_HEADS * A_VDIM), BF16),
        scratch_shapes=[pltpu.VMEM((2, TK, TQ), F32), pltpu.VMEM((2, TK, TQ), F32)],
        compiler_params=_params("parallel", "arbitrary"), name="mla_prompt",
    )(q, k, v)


def _mla_decode_kernel(layer, pt_ref, qabs_ref, q_ref, cnew_ref, knew_ref, wuv_ref, ckv_hbm, kpe_hbm,
                       o_ref, cbuf, kbuf, kcat, pcat, csem, ksem):
    P = DECODE_PAGES
    n = pl.program_id(0)
    n_steps = pt_ref.shape[1] // P
    page = cbuf.shape[2]

    def page_copies(pid, slot, i):
        return (pltpu.make_async_copy(ckv_hbm.at[layer, pid], cbuf.at[slot, i], csem.at[slot]),
                pltpu.make_async_copy(kpe_hbm.at[layer, pid], kbuf.at[slot, i], ksem.at[slot]))

    def start(req, step, slot):
        for i in range(P):
            for thread, cp in enumerate(page_copies(pt_ref[req, step * P + i], slot, i)):
                cp.start(priority=thread)

    def wait(slot):
        for i in range(P):
            for cp in page_copies(0, slot, i):
                cp.wait()

    ahead = DECODE_SLOTS - 1

    @pl.when(n == 0)
    def _():
        for step in range(ahead):
            start(0, step, step % DECODE_SLOTS)

    qa = qabs_ref[...]
    qpe = q_ref[:, A_NOPE:A_NOPE + A_ROPE]
    m = jnp.full((A_HEADS, 1), -jnp.inf, F32)
    l = jnp.zeros((A_HEADS, 1), F32)
    acc = jnp.zeros((A_HEADS, KV_RANK), F32)
    pending = None
    for step in range(n_steps):
        fetch = step + ahead
        if fetch < n_steps:
            start(n, fetch, fetch % DECODE_SLOTS)
        else:
            @pl.when(n + 1 < pl.num_programs(0))
            def _():
                start(n + 1, fetch - n_steps, fetch % DECODE_SLOTS)
        wait(step % DECODE_SLOTS)
        slot = step % 2
        for i in range(P):
            kcat[slot, i * page:(i + 1) * page, :] = cbuf[step % DECODE_SLOTS, i].astype(BF16)
            pcat[slot, :, i * page:(i + 1) * page] = kbuf[step % DECODE_SLOTS, i].astype(BF16)
        s = _dot_nt(qa, kcat[slot]) + jnp.dot(qpe, pcat[slot], preferred_element_type=F32)
        if pending is not None:
            pr_prev, alpha_prev, slot_prev = pending
            acc = acc * alpha_prev + jnp.dot(pr_prev, kcat[slot_prev], preferred_element_type=F32)
        m_new = jnp.maximum(m, jnp.max(s, axis=-1, keepdims=True))
        alpha = jnp.exp2(m - m_new)
        pr = jnp.exp2(s - m_new)
        l = l * alpha + jnp.sum(pr, axis=-1, keepdims=True)
        m = m_new
        pending = (pr.astype(BF16), alpha, slot)
    pr_prev, alpha_prev, slot_prev = pending
    acc = acc * alpha_prev + jnp.dot(pr_prev, kcat[slot_prev], preferred_element_type=F32)

    cnew = cnew_ref[...]
    s_self = (jnp.sum(qa * cnew, axis=-1, keepdims=True)
              + jnp.sum(qpe.astype(F32) * knew_ref[...], axis=-1, keepdims=True))
    m_fin = jnp.maximum(m, s_self)
    al = jnp.exp2(m - m_fin)
    p_self = jnp.exp2(s_self - m_fin)
    o_lat = (acc * al + p_self * cnew) / (l * al + p_self)
    res = _dot(o_lat, wuv_ref[...])
    hrow = lax.broadcasted_iota(jnp.int32, res.shape, 0)
    hcol = lax.broadcasted_iota(jnp.int32, res.shape, 1) // A_VDIM
    o_ref[...] = jnp.sum(jnp.where(hrow == hcol, res, 0.0), axis=0, keepdims=True).astype(o_ref.dtype)


def _mla_decode(page_table, qabs, q, ckv_new, kpe_new, wuv, cache_ckv, cache_kpe_t, layer):
    n, n_pages = page_table.shape
    P = DECODE_PAGES
    page = cache_ckv.shape[2]
    slots = DECODE_SLOTS
    assert (n_pages // P) % slots == 0, "buffer slots rotate per step and must line up across requests"
    req = lambda shape: pl.BlockSpec((None,) + shape, lambda b, pt: (b, 0, 0))
    hbm = pl.BlockSpec(memory_space=pl.ANY)
    grid_spec = pltpu.PrefetchScalarGridSpec(
        num_scalar_prefetch=1, grid=(n,),
        in_specs=[req((A_HEADS, KV_RANK)), req((A_HEADS, QK_PAD)), req((1, KV_RANK)), req((1, A_ROPE)),
                  pl.BlockSpec(wuv.shape, lambda b, pt: (0, 0)), hbm, hbm],
        out_specs=req((1, A_HEADS * A_VDIM)),
        scratch_shapes=[pltpu.VMEM((slots, P, page, KV_RANK), F32), pltpu.VMEM((slots, P, A_ROPE, page), F32),
                        pltpu.VMEM((2, P * page, KV_RANK), BF16), pltpu.VMEM((2, A_ROPE, P * page), BF16),
                        pltpu.SemaphoreType.DMA((slots,)), pltpu.SemaphoreType.DMA((slots,))])
    out = pl.pallas_call(
        functools.partial(_mla_decode_kernel, layer), grid_spec=grid_spec,
        out_shape=jax.ShapeDtypeStruct((n, 1, A_HEADS * A_VDIM), BF16),
        compiler_params=_params("arbitrary"), name="mla_decode",
    )(page_table, qabs, q, ckv_new[:, None, :], kpe_new[:, None, :], wuv, cache_ckv, cache_kpe_t)
    return out[:, 0, :]


def _mix_out_kernel(shared_mem, x_ref, oa_ref, y_ref, r_ref, k2_ref, v_ref, g_ref, gpre_ref, wga_ref,
                    wgb_ref, rk_ref, lnw_ref, lnb_ref, e_ref, wpa_ref, wpb_ref, wout_ref, gpost_ref,
                    gmem_ref, wmq_ref, *rest):
    x = x_ref[...]
    h = _rms(x, gpre_ref[...]).astype(BF16)
    gate_a = _sigmoid(jnp.dot(h, wga_ref[...], preferred_element_type=F32))
    gate_b = _sigmoid(jnp.dot(h, wgb_ref[...], preferred_element_type=F32))
    e = e_ref[...]
    y = y_ref[...]
    inv = 1.0 / B_HDIM
    d = y - _seg_sum(y, e) * inv
    var = _seg_sum(d * d, e) * inv
    yn = d * lax.rsqrt(var + GN_EPS) * lnw_ref[...] + lnb_ref[...]
    v = v_ref[...]
    bonus = _seg_sum(r_ref[...] * k2_ref[...] * rk_ref[...], e) * v
    ob = (yn + bonus) * g_ref[...]
    merged = gate_a * _dot(oa_ref[...], wpa_ref[...]) + gate_b * _dot(ob, wpb_ref[...])
    x1 = x + _rms(_dot(merged, wout_ref[...]), gpost_ref[...])
    qm = (_dot(_rms(x1, gmem_ref[...]), wmq_ref[...]) * ((x.shape[-1] // X_HEADS) ** -0.5)).astype(BF16)
    if shared_mem:
        k_ref, v_ref, x1_ref, om_ref = rest
        _mem_attn_shared(qm, k_ref, v_ref, om_ref)
    else:
        x1_ref, qm_ref = rest
        qm_ref[...] = qm
    x1_ref[...] = x1


def _mix_out(x, oa, y, r, k2, v, g, *weights, shared_mem=()):
    m, d = x.shape
    tm = min(ROW_TILE, m)
    wide = _rows(tm, d)
    half = _rows(tm, B_WIDTH)
    consts = tuple(weights) + tuple(shared_mem)
    return pl.pallas_call(
        functools.partial(_mix_out_kernel, bool(shared_mem)), grid=(m // tm,),
        in_specs=[wide] + [half] * 6 + [_const(w.shape) for w in consts],
        out_specs=[wide, wide],
        out_shape=[jax.ShapeDtypeStruct((m, d), F32), jax.ShapeDtypeStruct((m, d), BF16)],
        compiler_params=_params("parallel"), name="mix_out",
    )(x, oa, y, r, k2, v, g, *consts)


def _mem_kv_kernel(mem_ref, g_ref, wk_ref, wv_ref, k_ref, v_ref):
    h = _rms(mem_ref[...], g_ref[...]).astype(BF16)
    k_ref[...] = jnp.dot(h, wk_ref[...], preferred_element_type=F32)
    v_ref[...] = jnp.dot(h, wv_ref[...], preferred_element_type=F32)


def _mem_kv(mem, g, wk, wv):
    m, d = mem.shape
    o = jax.ShapeDtypeStruct((m, d), F32)
    return pl.pallas_call(
        _mem_kv_kernel, grid=(1,),
        in_specs=[_full(mem.shape), _full(g.shape), _full(wk.shape), _full(wv.shape)],
        out_specs=[_full((m, d))] * 2, out_shape=[o, o],
        compiler_params=_params("arbitrary"), name="mem_kv",
    )(mem, g, wk, wv)


def _softmax_rows(s):
    pr = jnp.exp(s - jnp.max(s, axis=-1, keepdims=True))
    return pr / jnp.sum(pr, axis=-1, keepdims=True)


def _mem_attn_shared(q, k_ref, v_ref, o_ref):
    hd = q.shape[-1] // X_HEADS
    heads = [slice(hh * hd, (hh + 1) * hd) for hh in range(X_HEADS)]
    s = [_dot_nt(q[:, sl], k_ref[:, sl]) for sl in heads]
    pr = [_softmax_rows(sh) for sh in s]
    o = [_dot(pr[hh], v_ref[:, heads[hh]]) for hh in range(X_HEADS)]
    for hh in range(X_HEADS):
        o_ref[:, heads[hh]] = o[hh].astype(o_ref.dtype)


def _mem_attn_rows_kernel(q_ref, k_ref, v_ref, o_ref):
    for i in range(q_ref.shape[0]):
        prod = k_ref[i] * q_ref[i][None]
        s = jnp.sum(jnp.sum(prod, axis=1), axis=-1, keepdims=True)
        pr = jnp.exp(s - jnp.max(s, axis=0, keepdims=True))
        w = pr / jnp.sum(pr, axis=0, keepdims=True)
        o_ref[i] = jnp.sum(w[:, None] * v_ref[i], axis=0)


def _mem_attn_rows(q, mk, mv):
    n, mem, heads, hd = mk.shape
    nb = 4
    tiles = hd // LANES
    tiled = lambda x, lead: jnp.swapaxes(x.reshape(lead + (heads, tiles, LANES)), -2, -3)
    qspec = pl.BlockSpec((nb, tiles, heads, LANES), lambda i: (i, 0, 0, 0))
    kspec = pl.BlockSpec((nb, mem, tiles, heads, LANES), lambda i: (i, 0, 0, 0, 0))
    out = pl.pallas_call(
        _mem_attn_rows_kernel, grid=(n // nb,),
        in_specs=[qspec, kspec, kspec], out_specs=qspec,
        out_shape=jax.ShapeDtypeStruct((n, tiles, heads, LANES), F32),
        compiler_params=_params("parallel"), name="mem_attn_rows",
    )(tiled(q.astype(F32), (n,)), tiled(mk, (n, mem)), tiled(mv, (n, mem)))
    return jnp.swapaxes(out, 1, 2).reshape(n, heads * hd).astype(BF16)


def _tail_kernel(x1_ref, om_ref, wmo_ref, gpm_ref, gmlp_ref, wup_ref, wdn_ref, gpost_ref, y_ref):
    x2 = x1_ref[...] + _rms(jnp.dot(om_ref[...], wmo_ref[...], preferred_element_type=F32), gpm_ref[...])
    h = _rms(x2, gmlp_ref[...]).astype(BF16)
    u = jnp.maximum(jnp.dot(h, wup_ref[...], preferred_element_type=F32), 0.0)
    ff = jnp.dot((u * u).astype(BF16), wdn_ref[...], preferred_element_type=F32)
    y_ref[...] = x2 + _rms(ff, gpost_ref[...])


def _tail(x1, om, *weights):
    m, d = x1.shape
    tm = min(ROW_TILE, m)
    return pl.pallas_call(
        _tail_kernel, grid=(m // tm,),
        in_specs=[_rows(tm, d), _rows(tm, d)] + [_const(w.shape) for w in weights],
        out_specs=_rows(tm, d), out_shape=jax.ShapeDtypeStruct((m, d), F32),
        compiler_params=_params("parallel"), name="tail",
    )(x1, om, *weights)


def _rot_cols(w):
    half = w.shape[-1] // 2
    return jnp.concatenate([-w[..., half:], w[..., :half]], axis=-1)


def _rope_tables(pos):
    inv = ROPE_THETA ** (-jnp.arange(0, A_ROPE, 2, dtype=F32) / A_ROPE)
    ang = inv[:, None] * pos.astype(F32)[None, :]
    z_lo = jnp.zeros((A_NOPE, pos.shape[0]), F32)
    z_hi = jnp.zeros((QK_PAD - A_NOPE - A_ROPE, pos.shape[0]), F32)
    cos, sin = jnp.cos(ang), jnp.sin(ang)
    return (jnp.concatenate([z_lo, cos, cos, z_hi], axis=0),
            jnp.concatenate([z_lo, sin, sin, z_hi], axis=0))


def _prep_layer(l, w_in, w_uq, w_uk, w_uv, rw_decay_up, rw_a_up, rw_g_up):
    d = w_in.shape[1]
    wi = w_in[l]
    o_rw = Q_RANK + KV_RANK + A_ROPE
    o_ga = o_rw + RW_COLS
    w_kpe = wi[:, Q_RANK + KV_RANK:o_rw]
    z = lambda n: jnp.zeros((d, n), F32)
    pad_hi = QK_PAD - A_NOPE - A_ROPE
    wlat = jnp.concatenate([wi[:, :Q_RANK + KV_RANK], z(A_NOPE), w_kpe, z(pad_hi),
                            z(A_NOPE), _rot_cols(w_kpe), z(pad_hi)], axis=1).astype(BF16)
    wrw = jnp.pad(wi[:, o_rw:o_ga], ((0, 0), (0, RW_PAD - RW_COLS))).astype(BF16)
    wga = wi[:, o_ga:o_ga + d].astype(BF16)
    wgb = wi[:, o_ga + d:].astype(BF16)
    uq = w_uq[l].reshape(Q_RANK, A_HEADS, A_NOPE + A_ROPE)
    nope, pe = uq[..., :A_NOPE], uq[..., A_NOPE:]
    zq = lambda n: jnp.zeros((Q_RANK, A_HEADS, n), F32)
    wqa = jnp.concatenate([nope, pe, zq(pad_hi)], axis=-1).reshape(Q_RANK, -1).astype(BF16)
    wqb = jnp.concatenate([zq(A_NOPE), _rot_cols(pe), zq(pad_hi)], axis=-1).reshape(Q_RANK, -1).astype(BF16)
    pad_cols = lambda w: jnp.pad(w, ((0, 0), (0, 0), (0, QK_PAD - w.shape[-1]))).reshape(KV_RANK, -1).astype(BF16)
    wk_cols = pad_cols(w_uk[l])
    wv_cols = pad_cols(w_uv[l])
    wk_rows = jnp.pad(jnp.transpose(w_uk[l], (1, 2, 0)), ((0, 0), (0, QK_PAD - A_NOPE), (0, 0))).astype(BF16)
    wv = w_uv[l].reshape(KV_RANK, -1).astype(BF16)
    wl = jnp.zeros((LORA_IN, 3 * B_WIDTH), F32)
    wl = wl.at[:DECAY_LORA, :B_WIDTH].set(rw_decay_up[l])
    wl = wl.at[DECAY_LORA:DECAY_LORA + AAA_LORA, B_WIDTH:2 * B_WIDTH].set(rw_a_up[l])
    wl = wl.at[DECAY_LORA + AAA_LORA:, 2 * B_WIDTH:].set(rw_g_up[l])
    return wlat, wrw, wga, wgb, wqa, wqb, wk_cols, wv_cols, wk_rows, wv, wl.astype(BF16)


def kernel(x_prompt, x_sample, mem_prompt, cache_ckv, cache_kpe, state_wkv, state_shift, cache_mem_k, cache_mem_v, page_table, norm_pre_mix, w_in, q_norm, w_uq, kv_norm, w_uk, w_uv, rw_mu, rw_w0, rw_decay_up, rw_a0, rw_a_up, rw_g_up, rw_k_k, rw_k_a, rw_r_k, rw_ln_w, rw_ln_b, w_proj_a, w_proj_b, w_out, norm_post_mix, norm_pre_mem, mem_norm, w_mq, w_mk, w_mv, w_mo, norm_post_mem, norm_pre_mlp, w_ff_up, w_ff_down, norm_post_mlp):
    depth = w_in.shape[0]
    n_p, seq, d = x_prompt.shape
    n_s, dec_seq, _ = x_sample.shape
    assert n_p == 1 and dec_seq == 1, "one prompt sequence and one new token per decode request"
    past_len = page_table.shape[1] * cache_ckv.shape[2]
    ck_p, sk_p = _rope_tables(jnp.arange(seq))
    ck_s, sk_s = _rope_tables(jnp.full((n_s,), past_len))
    seg = jnp.arange(2 * LANES) // B_HDIM
    e = (seg[:, None] == seg[None, :]).astype(BF16)
    cache_kpe_t = jnp.swapaxes(cache_kpe, 2, 3)
    row = lambda p, l: p[l].reshape(1, -1)
    bf = lambda p, l: p[l].astype(BF16)

    y_p = x_prompt.reshape(seq, d)
    y_s = x_sample.reshape(n_s, d)
    outs = [[] for _ in range(10)]
    for l in range(depth):
        wlat, wrw, wga, wgb, wqa, wqb, wk_cols, wv_cols, wk_rows, wv, wl = _prep_layer(
            l, w_in, w_uq, w_uk, w_uv, rw_decay_up, rw_a_up, rw_g_up)
        proj_w = (row(norm_pre_mix, l), wlat, wrw, row(q_norm, l), row(kv_norm, l), wqa, wqb)
        prep_w = (row(rw_mu, l), row(rw_w0, l), row(rw_a0, l), wl, row(rw_k_k, l), row(rw_k_a, l), e)
        mix_w = (row(norm_pre_mix, l), wga, wgb, row(rw_r_k, l), row(rw_ln_w, l), row(rw_ln_b, l), e,
                 bf(w_proj_a, l), bf(w_proj_b, l), bf(w_out, l), row(norm_post_mix, l),
                 row(norm_pre_mem, l), bf(w_mq, l))
        tail_w = (bf(w_mo, l), row(norm_post_mem, l), row(norm_pre_mlp, l), bf(w_ff_up, l),
                  bf(w_ff_down, l), row(norm_post_mlp, l))

        q, ckv_p, kpe_p, rw_p, k, v, r_, lw_, k2_, v_, na_, b_, g_ = _proj_in(
            y_p, *proj_w, ck_p, sk_p, wk_cols, wv_cols, jnp.zeros((1, RW_COLS), F32), *prep_w, prompt=True)
        oa_p = _mla_prompt(q, k, v)
        yb_p, wkv_p = _rwkv_chunked(r_, lw_, k2_, v_, na_, b_)
        mk_p, mv_p = _mem_kv(mem_prompt.reshape(-1, d), row(mem_norm, l), bf(w_mk, l), bf(w_mv, l))
        x1_p, om_p = _mix_out(y_p, oa_p, yb_p, r_, k2_, v_, g_, *mix_w, shared_mem=(mk_p, mv_p))
        y_p = _tail(x1_p, om_p, *tail_w)

        q, ckv_s, kpe_s, rw_s, qabs, r_, lw_, k2_, v_, na_, b_, g_ = _proj_in(
            y_s, *proj_w, ck_s, sk_s, wk_rows, wv, state_shift[l], *prep_w, prompt=False)
        oa_s = _mla_decode(page_table, jnp.transpose(qabs, (1, 0, 2)), jnp.transpose(q, (1, 0, 2)),
                           ckv_s, kpe_s, wv, cache_ckv, cache_kpe_t, l)
        yb_s, wkv_s = _rwkv_step(state_wkv[l], r_, lw_, k2_, v_, na_, b_)
        x1_s, qm_s = _mix_out(y_s, oa_s, yb_s, r_, k2_, v_, g_, *mix_w)
        om_s = _mem_attn_rows(qm_s, cache_mem_k[l], cache_mem_v[l])
        y_s = _tail(x1_s, om_s, *tail_w)

        mem_heads = (n_p, -1, X_HEADS, d // X_HEADS)
        layer_outs = (ckv_p.reshape(n_p, seq, KV_RANK), kpe_p.reshape(n_p, seq, A_ROPE),
                      wkv_p[None], rw_p[-1:], mk_p.reshape(mem_heads), mv_p.reshape(mem_heads),
                      ckv_s.reshape(n_s, 1, KV_RANK), kpe_s.reshape(n_s, 1, A_ROPE), wkv_s, rw_s)
        for lst, val in zip(outs, layer_outs):
            lst.append(val)
    return (y_p.reshape(n_p, seq, d), y_s.reshape(n_s, 1, d)) + tuple(jnp.stack(o) for o in outs)
```
